```python
import jax, jax.numpy as jnp
from jax import lax
import numpy as np

D_MODEL = 1024
BATCH = 2
SEQ = 8192
DEPTH = 2

N_MEM = 256
HEAD_DIM = 64
ROT_DIM = HEAD_DIM // 4
ROPE_THETA = 500000.0
MEM_HEADS = 4
MEM_WIDTH = MEM_HEADS * HEAD_DIM
MIX_WIDTH = D_MODEL
PRIMARY_WIDTH = MIX_WIDTH - MEM_WIDTH
NSA_HEADS = PRIMARY_WIDTH // HEAD_DIM
NSA_KV_GROUPS = 3
NSA_GROUP_SIZE = NSA_HEADS // NSA_KV_GROUPS
NSA_KV_WIDTH = NSA_KV_GROUPS * HEAD_DIM
CMP_BLOCK = 32
CMP_STRIDE = 16
CMP_HIDDEN = 256
SEL_BLOCK = 64
SEL_TOPK = 16
WINDOW = 512
Q_BLOCK = 64
NSA_IN = PRIMARY_WIDTH + 6 * NSA_KV_WIDTH + 3 * NSA_HEADS + MEM_WIDTH
RWKV_HEADS = PRIMARY_WIDTH // HEAD_DIM
DECAY_LORA = 64
AAA_LORA = 64
GATE_LORA = 128
RWKV_SHIFT_W = 3 * PRIMARY_WIDTH + DECAY_LORA + AAA_LORA + GATE_LORA
RWKV_IN = RWKV_SHIFT_W + MEM_WIDTH
RWKV_GN_EPS = HEAD_DIM * 1e-5
FFN_HIDDEN = ((8 * D_MODEL + 3 * 256 - 1) // (3 * 256)) * 256
NORM_EPS = 1e-6
NEG = -1e30
FORCE = 1e6

kernel_name = 'hybrid_nsa_rwkv7_memory_trunk'


def rmsnorm(x, g):
    xf = x.astype(jnp.float32)
    y = xf * lax.rsqrt(jnp.mean(xf * xf, axis=-1, keepdims=True) + NORM_EPS)
    return (y * g.astype(jnp.float32)).astype(x.dtype)


def masked_softmax(s, mask):
    s = jnp.where(mask, s.astype(jnp.float32), NEG)
    m = jnp.max(s, axis=-1, keepdims=True)
    e = jnp.where(mask, jnp.exp(s - m), 0.0)
    return e / jnp.maximum(jnp.sum(e, axis=-1, keepdims=True), 1e-30)


def rope_tables(positions):
    inv = ROPE_THETA ** (-jnp.arange(0, ROT_DIM, 2, dtype=jnp.float32) / ROT_DIM)
    ang = positions.astype(jnp.float32)[..., None] * inv
    return jnp.cos(ang), jnp.sin(ang)


def apply_rope(t, cos, sin):
    half = ROT_DIM // 2
    c = cos[:, :, None, :].astype(t.dtype)
    s = sin[:, :, None, :].astype(t.dtype)
    t1, t2, rest = t[..., :half], t[..., half:ROT_DIM], t[..., ROT_DIM:]
    return jnp.concatenate([t1 * c - t2 * s, t1 * s + t2 * c, rest], axis=-1)


def nsa_mixer(proj, cos, sin, gate_b, cmp_pos_k, cmp_pos_v, cmp_k_w1, cmp_k_w2, cmp_v_w1, cmp_v_w2):
    B, S, _ = proj.shape
    G, Hg, Dh = NSA_KV_GROUPS, NSA_GROUP_SIZE, HEAD_DIM
    offs = list(np.cumsum([PRIMARY_WIDTH] + [NSA_KV_WIDTH] * 6)[:])
    q, kc_raw, vc_raw, ks, vs, kw, vw, gl = jnp.split(proj, offs, axis=-1)
    q = apply_rope(q.reshape(B, S, NSA_HEADS, Dh), cos, sin).reshape(B, S, G, Hg, Dh)
    ks = apply_rope(ks.reshape(B, S, G, Dh), cos, sin)
    kw = apply_rope(kw.reshape(B, S, G, Dh), cos, sin)
    vs = vs.reshape(B, S, G, Dh)
    vw = vw.reshape(B, S, G, Dh)
    gates = jax.nn.sigmoid(gl + gate_b).reshape(B, S, G, Hg, 3)

    n_cmp = (S - CMP_BLOCK) // CMP_STRIDE + 1
    cmp_idx = jnp.arange(n_cmp)[:, None] * CMP_STRIDE + jnp.arange(CMP_BLOCK)[None, :]
    cmp_end = cmp_idx[:, -1]

    def compress(t, pos_emb, w1, w2):
        blocks = t.reshape(B, S, G, Dh)[:, cmp_idx] + pos_emb[None, None, :, None, :]
        h = jax.nn.gelu(jnp.einsum('bnlgd,ldh->bngh', blocks, w1.reshape(CMP_BLOCK, Dh, CMP_HIDDEN)))
        return jnp.einsum('bngh,he->bnge', h, w2)

    kc = compress(kc_raw, cmp_pos_k, cmp_k_w1, cmp_k_w2)
    kc = apply_rope(kc, cos[:, cmp_end], sin[:, cmp_end])
    vc = compress(vc_raw, cmp_pos_v, cmp_v_w1, cmp_v_w2)

    n_sel = S // SEL_BLOCK
    n_topk = min(SEL_TOPK, n_sel)
    ksb = ks.reshape(B, n_sel, SEL_BLOCK, G, Dh).transpose(0, 3, 1, 2, 4)
    vsb = vs.reshape(B, n_sel, SEL_BLOCK, G, Dh).transpose(0, 3, 1, 2, 4)
    ratio = SEL_BLOCK // CMP_STRIDE
    pad_l = (CMP_BLOCK - CMP_STRIDE) // CMP_STRIDE
    n_taps = ratio + CMP_BLOCK // CMP_STRIDE - 1
    tap_w = jnp.asarray(np.array([
        max(0, min((o - pad_l) * CMP_STRIDE + CMP_BLOCK, SEL_BLOCK) - max((o - pad_l) * CMP_STRIDE, 0)) / CMP_STRIDE
        for o in range(n_taps)], dtype=np.float32))
    pad_r = ratio * (n_sel - 1) + n_taps - n_cmp - pad_l
    tap_idx = ratio * jnp.arange(n_sel)[:, None] + jnp.arange(n_taps)[None, :]

    kw_pad = jnp.pad(kw, ((0, 0), (WINDOW, 0), (0, 0), (0, 0)))
    vw_pad = jnp.pad(vw, ((0, 0), (WINDOW, 0), (0, 0), (0, 0)))

    n_qb = S // Q_BLOCK
    qb_all = q.reshape(B, n_qb, Q_BLOCK, G, Hg, Dh).swapaxes(0, 1)
    gb_all = gates.reshape(B, n_qb, Q_BLOCK, G, Hg, 3).swapaxes(0, 1)
    scale = HEAD_DIM ** -0.5
    b_idx = jnp.arange(B)[:, None, None, None]
    g_idx = jnp.arange(G)[None, :, None, None]
    jj = jnp.arange(n_sel)

    def block(args):
        qb, gb, blk = args
        t0 = blk * Q_BLOCK
        tq = t0 + jnp.arange(Q_BLOCK)
        s_c = jnp.einsum('bqghd,bngd->bghqn', qb, kc) * scale
        p_c = masked_softmax(s_c, cmp_end[None, :] <= tq[:, None])
        o_c = jnp.einsum('bghqn,bngd->bqghd', p_c.astype(vc.dtype), vc)
        imp = jnp.pad(p_c.sum(axis=2), ((0, 0), (0, 0), (0, 0), (pad_l, pad_r)))
        imp = jnp.einsum('bgqjt,t->bgqj', imp[..., tap_idx], tap_w)
        cur = tq // SEL_BLOCK
        forced = (jj[None, :] == 0) | (jj[None, :] == cur[:, None]) | (jj[None, :] == cur[:, None] - 1)
        score = jnp.where(forced, FORCE, imp)
        score = jnp.where(jj[None, :] <= cur[:, None], score, -FORCE)
        _, top = lax.top_k(score, n_topk)
        k_sel = ksb[b_idx, g_idx, top]
        v_sel = vsb[b_idx, g_idx, top]
        tok = top[..., None] * SEL_BLOCK + jnp.arange(SEL_BLOCK)
        m_s = (tok <= tq[:, None, None]).reshape(B, G, 1, Q_BLOCK, n_topk * SEL_BLOCK)
        s_s = jnp.einsum('bqghd,bgqkld->bghqkl', qb, k_sel).reshape(B, G, Hg, Q_BLOCK, n_topk * SEL_BLOCK) * scale
        p_s = masked_softmax(s_s, m_s).reshape(B, G, Hg, Q_BLOCK, n_topk, SEL_BLOCK)
        o_s = jnp.einsum('bghqkl,bgqkld->bqghd', p_s.astype(v_sel.dtype), v_sel)
        kwb = lax.dynamic_slice_in_dim(kw_pad, t0, Q_BLOCK + WINDOW, axis=1)
        vwb = lax.dynamic_slice_in_dim(vw_pad, t0, Q_BLOCK + WINDOW, axis=1)
        tk = t0 - WINDOW + jnp.arange(Q_BLOCK + WINDOW)
        m_w = (tk[None, :] <= tq[:, None]) & (tq[:, None] - tk[None, :] < WINDOW) & (tk[None, :] >= 0)
        s_w = jnp.einsum('bqghd,bkgd->bghqk', qb, kwb) * scale
        p_w = masked_softmax(s_w, m_w)
        o_w = jnp.einsum('bghqk,bkgd->bqghd', p_w.astype(vwb.dtype), vwb)
        out = gb[..., 0:1] * o_c + gb[..., 1:2] * o_s + gb[..., 2:3] * o_w
        return out.reshape(B, Q_BLOCK, PRIMARY_WIDTH)

    out = lax.map(block, (qb_all, gb_all, jnp.arange(n_qb)))
    return out.swapaxes(0, 1).reshape(B, S, PRIMARY_WIDTH)


def rwkv7_time_mix(proj, mu, w0, w2, a0, a2, g2, k_k, k_a, r_k, ln_g, ln_b):
    B, S, _ = proj.shape
    H, N = RWKV_HEADS, HEAD_DIM
    prev = jnp.pad(proj, ((0, 0), (1, 0), (0, 0)))[:, :-1]
    xs = proj + (prev - proj) * mu
    offs = list(np.cumsum([PRIMARY_WIDTH] * 3 + [DECAY_LORA, AAA_LORA]))
    r, k, v, wl, al, gl = jnp.split(xs, offs, axis=-1)
    w = -jax.nn.softplus(-(w0 + jnp.tanh(wl) @ w2)) - 0.5
    decay = jnp.exp(-jnp.exp(w.astype(jnp.float32)))
    a = jax.nn.sigmoid(a0 + al @ a2)
    g = jax.nn.sigmoid(gl) @ g2
    kk = (k * k_k).reshape(B, S, H, N).astype(jnp.float32)
    kk = kk * lax.rsqrt(jnp.maximum(jnp.sum(kk * kk, axis=-1, keepdims=True), 1e-24))
    k = k * (1.0 + (a - 1.0) * k_a)
    heads = lambda t: t.reshape(B, S, H, N).astype(jnp.float32)
    rh, kh, vh, ah = heads(r), heads(k), heads(v), heads(a)
    wh = decay.reshape(B, S, H, N)
    tmaj = lambda t: jnp.moveaxis(t, 1, 0)

    def step(state, inp):
        r_t, w_t, k_t, v_t, a_t, b_t = inp
        sa = jnp.einsum('bhij,bhj->bhi', state, a_t)
        state = state * w_t[:, :, None, :] + sa[..., None] * b_t[:, :, None, :] + v_t[..., None] * k_t[:, :, None, :]
        return state, jnp.einsum('bhij,bhj->bhi', state, r_t)

    s0 = jnp.zeros((B, H, N, N), jnp.float32)
    _, y = lax.scan(step, s0, (tmaj(rh), tmaj(wh), tmaj(kh), tmaj(vh), tmaj(-kk), tmaj(kk * ah)))
    y = jnp.moveaxis(y, 0, 1)
    mean = jnp.mean(y, axis=-1, keepdims=True)
    var = jnp.mean(jnp.square(y - mean), axis=-1, keepdims=True)
    y = ((y - mean) * lax.rsqrt(var + RWKV_GN_EPS)).reshape(B, S, PRIMARY_WIDTH) * ln_g + ln_b
    bonus = jnp.sum(rh * kh * r_k, axis=-1, keepdims=True) * vh
    y = y + bonus.reshape(B, S, PRIMARY_WIDTH)
    return (y * g).astype(proj.dtype)


def memory_attention(q_mem, mem_k, mem_v):
    B, S, _ = q_mem.shape
    q = q_mem.reshape(B, S, MEM_HEADS, HEAD_DIM)
    s = jnp.einsum('bshd,bmhd->bhsm', q, mem_k).astype(jnp.float32) * (HEAD_DIM ** -0.5)
    p = jax.nn.softmax(s, axis=-1)
    return jnp.einsum('bhsm,bmhd->bshd', p.astype(mem_v.dtype), mem_v).reshape(B, S, MEM_WIDTH)


def setup_inputs(seed: int = 0) -> dict:
    key = jax.random.key(seed)
    keys = iter(jax.random.split(key, 48))
    nrm = lambda shape, scale: jax.random.normal(next(keys), shape, jnp.float32) * scale
    gain = lambda shape: 1.0 + nrm(shape, 0.05)
    n_nsa = (DEPTH + 1) // 2
    n_rwkv = DEPTH // 2
    D = D_MODEL
    x = nrm((BATCH, SEQ, D), 1.0)
    mem = nrm((BATCH, N_MEM, D), 1.0)
    positions = jax.random.randint(next(keys), (BATCH, 1), 0, 4096, dtype=jnp.int32) + jnp.arange(SEQ, dtype=jnp.int32)[None, :]
    return {
        'x': x,
        'mem': mem,
        'positions': positions,
        'mem_norm_g': gain((D,)),
        'w_mem_kv': nrm((D, 2 * MEM_WIDTH), D ** -0.5),
        'pre_mix_g': gain((DEPTH, D)),
        'post_mix_g': gain((DEPTH, D)),
        'pre_ffn_g': gain((DEPTH, D)),
        'post_ffn_g': gain((DEPTH, D)),
        'w_out': nrm((DEPTH, MIX_WIDTH, D), MIX_WIDTH ** -0.5),
        'w_ffn_in': nrm((DEPTH, D, 2 * FFN_HIDDEN), D ** -0.5),
        'w_ffn_out': nrm((DEPTH, FFN_HIDDEN, D), FFN_HIDDEN ** -0.5),
        'nsa_w_in': nrm((n_nsa, D, NSA_IN), D ** -0.5),
        'nsa_gate_b': nrm((n_nsa, 3 * NSA_HEADS), 0.1),
        'nsa_cmp_pos_k': nrm((n_nsa, CMP_BLOCK, HEAD_DIM), 0.02),
        'nsa_cmp_pos_v': nrm((n_nsa, CMP_BLOCK, HEAD_DIM), 0.02),
        'nsa_cmp_k_w1': nrm((n_nsa, CMP_BLOCK * HEAD_DIM, CMP_HIDDEN), (CMP_BLOCK * HEAD_DIM) ** -0.5),
        'nsa_cmp_k_w2': nrm((n_nsa, CMP_HIDDEN, HEAD_DIM), CMP_HIDDEN ** -0.5),
        'nsa_cmp_v_w1': nrm((n_nsa, CMP_BLOCK * HEAD_DIM, CMP_HIDDEN), (CMP_BLOCK * HEAD_DIM) ** -0.5),
        'nsa_cmp_v_w2': nrm((n_nsa, CMP_HIDDEN, HEAD_DIM), CMP_HIDDEN ** -0.5),
        'rwkv_w_in': nrm((n_rwkv, D, RWKV_IN), D ** -0.5),
        'rwkv_mu': jax.random.uniform(next(keys), (n_rwkv, RWKV_SHIFT_W), jnp.float32),
        'rwkv_w0': jax.random.uniform(next(keys), (n_rwkv, PRIMARY_WIDTH), jnp.float32, -5.0, -0.5),
        'rwkv_w2': nrm((n_rwkv, DECAY_LORA, PRIMARY_WIDTH), 0.5 * DECAY_LORA ** -0.5),
        'rwkv_a0': nrm((n_rwkv, PRIMARY_WIDTH), 0.5),
        'rwkv_a2': nrm((n_rwkv, AAA_LORA, PRIMARY_WIDTH), AAA_LORA ** -0.5),
        'rwkv_g2': nrm((n_rwkv, GATE_LORA, PRIMARY_WIDTH), GATE_LORA ** -0.5),
        'rwkv_k_k': 0.85 + nrm((n_rwkv, PRIMARY_WIDTH), 0.05),
        'rwkv_k_a': gain((n_rwkv, PRIMARY_WIDTH)),
        'rwkv_r_k': nrm((n_rwkv, RWKV_HEADS, HEAD_DIM), 0.1),
        'rwkv_ln_g': gain((n_rwkv, PRIMARY_WIDTH)),
        'rwkv_ln_b': nrm((n_rwkv, PRIMARY_WIDTH), 0.02),
    }


def reference(x, mem, positions, mem_norm_g, w_mem_kv, pre_mix_g, post_mix_g, pre_ffn_g, post_ffn_g,
              w_out, w_ffn_in, w_ffn_out, nsa_w_in, nsa_gate_b, nsa_cmp_pos_k, nsa_cmp_pos_v,
              nsa_cmp_k_w1, nsa_cmp_k_w2, nsa_cmp_v_w1, nsa_cmp_v_w2, rwkv_w_in, rwkv_mu, rwkv_w0,
              rwkv_w2, rwkv_a0, rwkv_a2, rwkv_g2, rwkv_k_k, rwkv_k_a, rwkv_r_k, rwkv_ln_g, rwkv_ln_b):
    B, S, _ = x.shape
    M = mem.shape[1]
    mkv = rmsnorm(mem, mem_norm_g) @ w_mem_kv
    mem_k = mkv[..., :MEM_WIDTH].reshape(B, M, MEM_HEADS, HEAD_DIM)
    mem_v = mkv[..., MEM_WIDTH:].reshape(B, M, MEM_HEADS, HEAD_DIM)
    cos, sin = rope_tables(positions)
    h = x
    for i in range(DEPTH):
        j = i // 2
        hn = rmsnorm(h, pre_mix_g[i])
        if i % 2 == 0:
            proj = hn @ nsa_w_in[j]
            prim = nsa_mixer(proj[..., :NSA_IN - MEM_WIDTH], cos, sin, nsa_gate_b[j], nsa_cmp_pos_k[j],
                             nsa_cmp_pos_v[j], nsa_cmp_k_w1[j], nsa_cmp_k_w2[j], nsa_cmp_v_w1[j], nsa_cmp_v_w2[j])
        else:
            proj = hn @ rwkv_w_in[j]
            prim = rwkv7_time_mix(proj[..., :RWKV_SHIFT_W], rwkv_mu[j], rwkv_w0[j], rwkv_w2[j], rwkv_a0[j],
                                  rwkv_a2[j], rwkv_g2[j], rwkv_k_k[j], rwkv_k_a[j], rwkv_r_k[j],
                                  rwkv_ln_g[j], rwkv_ln_b[j])
        mo = memory_attention(proj[..., -MEM_WIDTH:], mem_k, mem_v)
        mix = jnp.concatenate([prim, mo], axis=-1) @ w_out[i]
        h = h + rmsnorm(mix, post_mix_g[i])
        hn = rmsnorm(h, pre_ffn_g[i])
        gu = hn @ w_ffn_in[i]
        ff = (jax.nn.silu(gu[..., :FFN_HIDDEN]) * gu[..., FFN_HIDDEN:]) @ w_ffn_out[i]
        h = h + rmsnorm(ff, post_ffn_g[i])
    return h
```

```python
import functools

import numpy as np
import jax
import jax.numpy as jnp
from jax import lax
from jax.experimental import pallas as pl
from jax.experimental.pallas import tpu as pltpu

F32 = jnp.float32
BF16 = jnp.bfloat16

HEAD_DIM = 64
ROT_DIM = HEAD_DIM // 4
ROT_HALF = ROT_DIM // 2
ROPE_THETA = 500000.0
MEM_HEADS = 4
MEM_WIDTH = MEM_HEADS * HEAD_DIM
KV_GROUPS = 3
GROUP_HEADS = 4
GROUP_WIDTH = GROUP_HEADS * HEAD_DIM
PRIMARY_WIDTH = KV_GROUPS * GROUP_WIDTH
KV_WIDTH = KV_GROUPS * HEAD_DIM
CMP_BLOCK = 32
CMP_STRIDE = 16
SEL_BLOCK = 64
SEL_TOPK = 16
WINDOW = 512
DECAY_LORA = 64
AAA_LORA = 64
GATE_LORA = 128
RWKV_SHIFT_W = 3 * PRIMARY_WIDTH + DECAY_LORA + AAA_LORA + GATE_LORA
RWKV_GN_EPS = HEAD_DIM * 1e-5
NORM_EPS = 1e-6
NEG = -1e30
FORCE = 1e6
ATTN_SCALE = HEAD_DIM ** -0.5

LANES = 128
SUBLANES = 8
VMEM_LIMIT_BYTES = 48 * 1024 * 1024

ROW_TILE = 512
ATTN_TILE = 256
SCAN_CHUNK = 64
INV_BASE = 8
PREP_TILE = 256

_NT = (((1,), (1,)), ((), ()))
_TN = (((0,), (0,)), ((), ()))


def _params(*sem):
    return pltpu.CompilerParams(dimension_semantics=sem, vmem_limit_bytes=VMEM_LIMIT_BYTES)


def _rms(x, g):
    return x * lax.rsqrt(jnp.mean(x * x, axis=-1, keepdims=True) + NORM_EPS) * g


def _dot(a, b):
    return jnp.dot(a, b, preferred_element_type=F32)


def _dot_nt(a, b):
    return lax.dot_general(a, b, _NT, preferred_element_type=F32)


def _proj_kernel(x_ref, g_ref, w_ref, o_ref, xn_ref):
    @pl.when(pl.program_id(1) == 0)
    def _():
        xn_ref[...] = _rms(x_ref[...], g_ref[...]).astype(BF16)

    o_ref[...] = _dot(xn_ref[...], w_ref[...]).astype(o_ref.dtype)


def _proj_rope_kernel(x_ref, g_ref, w_ref, c_ref, sp_ref, sm_ref, scale_ref, o_ref, xn_ref):
    @pl.when(pl.program_id(1) == 0)
    def _():
        xn_ref[...] = _rms(x_ref[...], g_ref[...]).astype(BF16)

    y = _dot(xn_ref[...], w_ref[...])
    width = y.shape[1]
    reps = width // LANES
    cos = jnp.concatenate([c_ref[...]] * reps, axis=1)
    sin_hi = jnp.concatenate([sp_ref[...]] * reps, axis=1)
    sin_lo = jnp.concatenate([sm_ref[...]] * reps, axis=1)
    y = y * cos + pltpu.roll(y, ROT_HALF, 1) * sin_hi + pltpu.roll(y, width - ROT_HALF, 1) * sin_lo
    o_ref[...] = (y * scale_ref[...]).astype(o_ref.dtype)


def _proj_t_kernel(x_ref, g_ref, wt_ref, b_ref, o_ref, xn_ref, *, gate):
    @pl.when(pl.program_id(1) == 0)
    def _():
        xn_ref[...] = _rms(x_ref[...], g_ref[...]).astype(BF16)

    y = _dot_nt(wt_ref[...], xn_ref[...])
    if gate:
        y = jax.nn.sigmoid(y + b_ref[...])
    o_ref[0] = y.astype(o_ref.dtype)


def _norm_proj(x, g, w, *, tn, out_dtype, rope=None, name):
    t, d = x.shape
    n = w.shape[1]
    tm = min(ROW_TILE, t)
    in_specs = [
        pl.BlockSpec((tm, d), lambda i, j: (i, 0)),
        pl.BlockSpec((1, d), lambda i, j: (0, 0)),
        pl.BlockSpec((d, tn), lambda i, j: (0, j)),
    ]
    args = [x, g.reshape(1, d), w.astype(BF16)]
    if rope is None:
        body = _proj_kernel
    else:
        body = _proj_rope_kernel
        cos, sin_hi, sin_lo, scale = rope
        in_specs += [pl.BlockSpec((tm, LANES), lambda i, j: (i, 0))] * 3
        in_specs += [pl.BlockSpec((1, tn), lambda i, j: (0, j))]
        args += [cos, sin_hi, sin_lo, scale]
    return pl.pallas_call(
        body,
        grid=(t // tm, n // tn),
        in_specs=in_specs,
        out_specs=pl.BlockSpec((tm, tn), lambda i, j: (i, j)),
        out_shape=jax.ShapeDtypeStruct((t, n), out_dtype),
        scratch_shapes=[pltpu.VMEM((tm, d), BF16)],
        compiler_params=_params("parallel", "arbitrary"),
        name=name,
    )(*args)


def _norm_proj_t(x, g, wt, bias, *, tm, tn, out_dtype, gate, name):
    t, d = x.shape
    n = wt.shape[0]
    return pl.pallas_call(
        functools.partial(_proj_t_kernel, gate=gate),
        grid=(t // tm, n // tn),
        in_specs=[
            pl.BlockSpec((tm, d), lambda i, j: (i, 0)),
            pl.BlockSpec((1, d), lambda i, j: (0, 0)),
            pl.BlockSpec((tn, d), lambda i, j: (j, 0)),
            pl.BlockSpec((tn, 1), lambda i, j: (j, 0)),
        ],
        out_specs=pl.BlockSpec((1, tn, tm), lambda i, j: (i, j, 0)),
        out_shape=jax.ShapeDtypeStruct((t // tm, n, tm), out_dtype),
        scratch_shapes=[pltpu.VMEM((tm, d), BF16)],
        compiler_params=_params("parallel", "arbitrary"),
        name=name,
    )(x, g.reshape(1, d), wt.astype(BF16), bias)


def _outproj_kernel(a_ref, b_ref, wa_ref, wb_ref, g_ref, h_ref, o_ref):
    y = _dot(a_ref[...], wa_ref[...]) + _dot(b_ref[...], wb_ref[...])
    o_ref[...] = h_ref[...] + _rms(y, g_ref[...])


def _out_proj(prim, mo, w_out, g, h):
    t, d = h.shape
    tm = min(ROW_TILE, t)
    pw = prim.shape[1]
    return pl.pallas_call(
        _outproj_kernel,
        grid=(t // tm,),
        in_specs=[
            pl.BlockSpec((tm, pw), lambda i: (i, 0)),
            pl.BlockSpec((tm, MEM_WIDTH), lambda i: (i, 0)),
            pl.BlockSpec((pw, d), lambda i: (0, 0)),
            pl.BlockSpec((MEM_WIDTH, d), lambda i: (0, 0)),
            pl.BlockSpec((1, d), lambda i: (0, 0)),
            pl.BlockSpec((tm, d), lambda i: (i, 0)),
        ],
        out_specs=pl.BlockSpec((tm, d), lambda i: (i, 0)),
        out_shape=jax.ShapeDtypeStruct((t, d), F32),
        compiler_params=_params("parallel"),
        name="out_proj",
    )(prim, mo, w_out[:pw].astype(BF16), w_out[pw:].astype(BF16), g.reshape(1, d), h)


def _ffn_kernel(h_ref, g1_ref, wg_ref, wu_ref, wo_ref, g2_ref, o_ref, hn_ref, acc_ref):
    j = pl.program_id(1)

    @pl.when(j == 0)
    def _():
        hn_ref[...] = _rms(h_ref[...], g1_ref[...]).astype(BF16)
        acc_ref[...] = jnp.zeros_like(acc_ref)

    hn = hn_ref[...]
    gate = _dot(hn, wg_ref[...])
    up = _dot(hn, wu_ref[...])
    act = (jax.nn.silu(gate) * up).astype(BF16)
    acc_ref[...] += _dot(act, wo_ref[...])

    @pl.when(j == pl.num_programs(1) - 1)
    def _():
        o_ref[...] = h_ref[...] + _rms(acc_ref[...], g2_ref[...])


def _ffn_chunk(hidden):
    units = hidden // LANES
    for parts in range(2, units + 1):
        if units % parts == 0:
            return (units // parts) * LANES
    return hidden


def _ffn(h, g1, w_in, w_out, g2):
    t, d = h.shape
    hidden = w_out.shape[0]
    th = _ffn_chunk(hidden)
    nh = hidden // th
    tm = min(ROW_TILE, t)
    w_in = w_in.astype(BF16)
    return pl.pallas_call(
        _ffn_kernel,
        grid=(t // tm, nh),
        in_specs=[
            pl.BlockSpec((tm, d), lambda i, j: (i, 0)),
            pl.BlockSpec((1, d), lambda i, j: (0, 0)),
            pl.BlockSpec((d, th), lambda i, j: (0, j)),
            pl.BlockSpec((d, th), lambda i, j: (0, j + nh)),
            pl.BlockSpec((th, d), lambda i, j: (j, 0)),
            pl.BlockSpec((1, d), lambda i, j: (0, 0)),
        ],
        out_specs=pl.BlockSpec((tm, d), lambda i, j: (i, 0)),
        out_shape=jax.ShapeDtypeStruct((t, d), F32),
        scratch_shapes=[pltpu.VMEM((tm, d), BF16), pltpu.VMEM((tm, d), F32)],
        compiler_params=_params("parallel", "arbitrary"),
        name="ffn",
    )(h, g1.reshape(1, d), w_in, w_in, w_out.astype(BF16), g2.reshape(1, d))


def _mem_attn_kernel(q_ref, mk_ref, mv_ref, o_ref):
    q = q_ref[...]
    mk = mk_ref[0]
    mv = mv_ref[0]
    head_of_lane = lax.broadcasted_iota(jnp.int32, mk.shape, 1) // HEAD_DIM
    acc = jnp.zeros(q.shape, F32)
    for h in range(MEM_HEADS):
        s = _dot_nt(q, jnp.where(head_of_lane == h, mk, 0)) * ATTN_SCALE
        e = jnp.exp(s - jnp.max(s, axis=-1, keepdims=True))
        p = e / jnp.sum(e, axis=-1, keepdims=True)
        acc = acc + _dot(p.astype(BF16), jnp.where(head_of_lane == h, mv, 0))
    o_ref[...] = acc.astype(o_ref.dtype)


def _mem_attn(q_mem, mem_k, mem_v, seq):
    t = q_mem.shape[0]
    m = mem_k.shape[1]
    tm = min(ROW_TILE, seq)
    per_seq = seq // tm
    return pl.pallas_call(
        _mem_attn_kernel,
        grid=(t // tm,),
        in_specs=[
            pl.BlockSpec((tm, MEM_WIDTH), lambda i: (i, 0)),
            pl.BlockSpec((1, m, MEM_WIDTH), lambda i: (i // per_seq, 0, 0)),
            pl.BlockSpec((1, m, MEM_WIDTH), lambda i: (i // per_seq, 0, 0)),
        ],
        out_specs=pl.BlockSpec((tm, MEM_WIDTH), lambda i: (i, 0)),
        out_shape=jax.ShapeDtypeStruct((t, MEM_WIDTH), BF16),
        compiler_params=_params("parallel"),
        name="mem_attn",
    )(q_mem, mem_k, mem_v)


def _compress_kernel(xk_ref, xv_ref, pk_ref, pv_ref, k1a_ref, k1b_ref, k2_ref, v1a_ref, v1b_ref, v2t_ref,
                     c_ref, s_ref, kc_ref, vct_ref):
    def hidden(x_ref, pos_ref, wa_ref, wb_ref):
        x = x_ref[0, 0]
        n = x.shape[0]
        first = _dot((x + pos_ref[0:1, :]).astype(BF16), wa_ref[...])
        second = _dot((x + pos_ref[1:2, :]).astype(BF16), wb_ref[...])
        return jax.nn.gelu(first + pltpu.roll(second, n - 1, 0)).astype(BF16)

    hk = hidden(xk_ref, pk_ref, k1a_ref, k1b_ref)
    both = _dot(hk, k2_ref[...])
    kc_ref[0, 0] = (both[:, :LANES] * c_ref[0] + both[:, LANES:] * s_ref[0]).astype(kc_ref.dtype)
    hv = hidden(xv_ref, pv_ref, v1a_ref, v1b_ref)
    vct_ref[0, 0] = _dot_nt(v2t_ref[...], hv).astype(vct_ref.dtype)


def _rope_partner_cols(w):
    d = np.arange(w.shape[1]) % HEAD_DIM
    src = np.where(d < ROT_HALF, np.arange(w.shape[1]) + ROT_HALF, np.arange(w.shape[1]) - ROT_HALF)
    src = np.clip(src, 0, w.shape[1] - 1)
    sign = np.where(d < ROT_HALF, -1.0, np.where(d < ROT_DIM, 1.0, 0.0)).astype(np.float32)
    return w[:, src] * sign


def _compress(raw_k, raw_v, pos_k, pos_v, k_w1, k_w2, v_w1, v_w2, cos_c, sin_c, batch, seq):
    n_chunk = seq // CMP_STRIDE
    feat = CMP_STRIDE * HEAD_DIM
    hid = k_w1.shape[1]

    def chunks(raw):
        x = raw.reshape(batch, n_chunk, CMP_STRIDE, KV_GROUPS, HEAD_DIM)
        return x.transpose(0, 3, 1, 2, 4).reshape(batch, KV_GROUPS, n_chunk, feat)

    zeros = jnp.zeros((hid, LANES - HEAD_DIM), F32)
    k2 = jnp.concatenate([k_w2, zeros, _rope_partner_cols(k_w2), zeros], axis=1).astype(BF16)
    x_spec = pl.BlockSpec((1, 1, n_chunk, feat), lambda b, g: (b, g, 0, 0))
    pos_spec = pl.BlockSpec((2, feat), lambda b, g: (0, 0))
    w1_spec = pl.BlockSpec((feat, hid), lambda b, g: (0, 0))
    tab_spec = pl.BlockSpec((1, n_chunk, LANES), lambda b, g: (b, 0, 0))
    return pl.pallas_call(
        _compress_kernel,
        grid=(batch, KV_GROUPS),
        in_specs=[x_spec, x_spec, pos_spec, pos_spec, w1_spec, w1_spec,
                  pl.BlockSpec((hid, 2 * LANES), lambda b, g: (0, 0)),
                  w1_spec, w1_spec,
                  pl.BlockSpec((HEAD_DIM, hid), lambda b, g: (0, 0)),
                  tab_spec, tab_spec],
        out_specs=[pl.BlockSpec((1, 1, n_chunk, LANES), lambda b, g: (b, g, 0, 0)),
                   pl.BlockSpec((1, 1, HEAD_DIM, n_chunk), lambda b, g: (b, g, 0, 0))],
        out_shape=[jax.ShapeDtypeStruct((batch, KV_GROUPS, n_chunk, LANES), BF16),
                   jax.ShapeDtypeStruct((batch, KV_GROUPS, HEAD_DIM, n_chunk), BF16)],
        compiler_params=_params("parallel", "parallel"),
        name="nsa_compress",
    )(chunks(raw_k), chunks(raw_v), pos_k.reshape(2, feat), pos_v.reshape(2, feat),
      k_w1[:feat].astype(BF16), k_w1[feat:].astype(BF16), k2,
      v_w1[:feat].astype(BF16), v_w1[feat:].astype(BF16), v_w2.T.astype(BF16), cos_c, sin_c)


def _nsa_attn_kernel(q_ref, kk_ref, vt_ref, kc_ref, vct_ref, tap_ref, gate_ref, o_ref,
                     sel_ref, m_ref, l_ref, acc_ref):
    tq = q_ref.shape[0]
    tk = tq
    qt = pl.program_id(2)
    t0 = qt * tq
    width = GROUP_HEADS * tq

    q2 = q_ref[...]
    qs = jnp.concatenate([q2[:, h * LANES:(h + 1) * LANES] for h in range(GROUP_HEADS)], axis=0)
    lane = lax.broadcasted_iota(jnp.int32, qs.shape, 1)
    q_sel = jnp.where(lane < HEAD_DIM, qs, 0)
    q_win = jnp.where(lane >= HEAD_DIM, qs, 0)

    col_t = t0 + lax.broadcasted_iota(jnp.int32, (1, tq), 1)
    col_t4 = jnp.concatenate([col_t] * GROUP_HEADS, axis=1)

    n_cmp = kc_ref.shape[2]
    s = _dot_nt(kc_ref[0, 0], q_sel)
    blk_end = lax.broadcasted_iota(jnp.int32, (n_cmp, 1), 0) * CMP_STRIDE + (CMP_BLOCK - 1)
    valid = blk_end <= col_t4
    s = jnp.where(valid, s, NEG)
    e = jnp.where(valid, jnp.exp(s - jnp.max(s, axis=0, keepdims=True)), 0.0)
    p = e / jnp.maximum(jnp.sum(e, axis=0, keepdims=True), 1e-30)
    o_cmp = _dot(vct_ref[0, 0], p.astype(BF16))
    p_sum = p[:, 0:tq]
    for h in range(1, GROUP_HEADS):
        p_sum = p_sum + p[:, h * tq:(h + 1) * tq]
    imp = jnp.dot(tap_ref[...], p_sum, precision=lax.Precision.HIGHEST, preferred_element_type=F32)

    n_sel = imp.shape[0]
    blk = lax.broadcasted_iota(jnp.int32, (n_sel, tq), 0)
    blk_f = blk.astype(F32)
    cur = col_t // SEL_BLOCK
    forced = (blk == 0) | (blk == cur) | (blk == cur - 1)
    score = jnp.where(forced, FORCE, imp)
    score = jnp.where(blk <= cur, score, -FORCE)

    def pick_one(_, carry):
        work, chosen = carry
        best = jnp.max(work, axis=0, keepdims=True)
        first = jnp.min(jnp.where(work == best, blk_f, float(n_sel)), axis=0, keepdims=True)
        hit = blk_f == first
        return jnp.where(hit, -jnp.inf, work), jnp.where(hit, 1.0, chosen)

    _, chosen = lax.fori_loop(0, min(SEL_TOPK, n_sel), pick_one, (score, jnp.zeros((n_sel, tq), F32)))
    sel_ref[...] = chosen

    key_in_tile = lax.broadcasted_iota(jnp.int32, (tk, 1), 0)
    blocks_per_tile = tk // SEL_BLOCK

    def key_rows(kt):
        return kk_ref[pl.ds(pl.multiple_of(kt * tk, tk), tk), :]

    def flash_step(scores, v_rows):
        m_old = m_ref[...]
        m_new = jnp.maximum(m_old, jnp.max(scores, axis=0, keepdims=True))
        alpha = jnp.exp(m_old - m_new)
        e = jnp.exp(scores - m_new)
        l_ref[...] = alpha * l_ref[...] + jnp.sum(e, axis=0, keepdims=True)
        acc_ref[...] = alpha * acc_ref[...] + _dot(v_rows, e.astype(BF16))
        m_ref[...] = m_new

    def reset():
        m_ref[...] = jnp.full(m_ref.shape, NEG, F32)
        l_ref[...] = jnp.zeros(l_ref.shape, F32)
        acc_ref[...] = jnp.zeros(acc_ref.shape, F32)

    def sel_bias(kt):
        rows = [jnp.broadcast_to(sel_ref[pl.ds(kt * blocks_per_tile + i, 1), :], (SEL_BLOCK, tq))
                for i in range(blocks_per_tile)]
        bias = (jnp.concatenate(rows, axis=0) - 1.0) * (-NEG)
        return jnp.concatenate([bias] * GROUP_HEADS, axis=1)

    reset()

    def sel_step(kt, carry):
        flash_step(_dot_nt(key_rows(kt), q_sel) + sel_bias(kt), vt_ref[kt, 0:HEAD_DIM, :])
        return carry

    lax.fori_loop(0, qt, sel_step, 0)
    causal = (t0 + key_in_tile) <= col_t4
    scores = _dot_nt(key_rows(qt), q_sel) + sel_bias(qt)
    flash_step(jnp.where(causal, scores, NEG), vt_ref[qt, 0:HEAD_DIM, :])
    o_sel = acc_ref[...] / l_ref[...]

    reset()
    flash_step(jnp.where(causal, _dot_nt(key_rows(qt), q_win), NEG), vt_ref[qt, HEAD_DIM:2 * HEAD_DIM, :])
    for back in range(1, WINDOW // tk + 1):
        @pl.when(qt >= back)
        def _():
            kt = qt - back
            in_window = (col_t4 - (kt * tk + key_in_tile)) < WINDOW
            flash_step(jnp.where(in_window, _dot_nt(key_rows(kt), q_win), NEG),
                       vt_ref[kt, HEAD_DIM:2 * HEAD_DIM, :])
    o_win = acc_ref[...] / l_ref[...]

    gates = gate_ref[...]
    outs = []
    for h in range(GROUP_HEADS):
        cols = slice(h * tq, (h + 1) * tq)
        outs.append(gates[3 * h:3 * h + 1, :] * o_cmp[:, cols]
                    + gates[3 * h + 1:3 * h + 2, :] * o_sel[:, cols]
                    + gates[3 * h + 2:3 * h + 3, :] * o_win[:, cols])
    o_ref[...] = jnp.concatenate(outs, axis=0).T.astype(o_ref.dtype)


def _tap_matrix(n_sel, n_chunk):
    ratio = SEL_BLOCK // CMP_STRIDE
    tap = np.zeros((n_sel, n_chunk), np.float32)
    n_cmp = n_chunk - (CMP_BLOCK // CMP_STRIDE - 1)
    for j in range(n_sel):
        for n in range(n_cmp):
            lo = max(n * CMP_STRIDE, j * SEL_BLOCK)
            hi = min(n * CMP_STRIDE + CMP_BLOCK, (j + 1) * SEL_BLOCK)
            if hi > lo:
                tap[j, n] = (hi - lo) / CMP_STRIDE
    return jnp.asarray(tap)


def _nsa_attention(qk, vt, gates, kc, vct, batch, seq):
    tq = ATTN_TILE
    n_tiles = seq // tq
    n_chunk = kc.shape[2]
    n_sel = seq // SEL_BLOCK
    q_blocks = GROUP_HEADS * LANES
    k_col0 = KV_GROUPS * q_blocks // LANES
    vt5 = vt.reshape(batch, n_tiles, KV_GROUPS, 2 * HEAD_DIM, tq)
    gates5 = gates.reshape(batch, n_tiles, KV_GROUPS, 16, tq)
    width = GROUP_HEADS * tq
    return pl.pallas_call(
        _nsa_attn_kernel,
        grid=(batch, KV_GROUPS, n_tiles),
        in_specs=[
            pl.BlockSpec((tq, q_blocks), lambda b, g, i: (b * n_tiles + i, g)),
            pl.BlockSpec((seq, LANES), lambda b, g, i: (b, k_col0 + g)),
            pl.BlockSpec((None, n_tiles, None, 2 * HEAD_DIM, tq), lambda b, g, i: (b, 0, g, 0, 0)),
            pl.BlockSpec((1, 1, n_chunk, LANES), lambda b, g, i: (b, g, 0, 0)),
            pl.BlockSpec((1, 1, HEAD_DIM, n_chunk), lambda b, g, i: (b, g, 0, 0)),
            pl.BlockSpec((n_sel, n_chunk), lambda b, g, i: (0, 0)),
            pl.BlockSpec((None, None, None, 16, tq), lambda b, g, i: (b, i, g, 0, 0)),
        ],
        out_specs=pl.BlockSpec((tq, GROUP_WIDTH), lambda b, g, i: (b * n_tiles + i, g)),
        out_shape=jax.ShapeDtypeStruct((batch * seq, PRIMARY_WIDTH), BF16),
        scratch_shapes=[
            pltpu.VMEM((n_sel, tq), F32),
            pltpu.VMEM((1, width), F32),
            pltpu.VMEM((1, width), F32),
            pltpu.VMEM((HEAD_DIM, width), F32),
        ],
        compiler_params=_params("parallel", "parallel", "arbitrary"),
        name="nsa_attention",
    )(qk, qk, vt5, kc, vct, _tap_matrix(n_sel, n_chunk), gates5)


def _rope_tables(positions):
    inv = ROPE_THETA ** (-jnp.arange(0, ROT_DIM, 2, dtype=F32) / ROT_DIM)
    ang = positions.astype(F32)[..., None] * inv
    cos, sin = jnp.cos(ang), jnp.sin(ang)
    pad = jnp.zeros(ang.shape[:-1] + (HEAD_DIM - ROT_DIM,), F32)
    zero = jnp.zeros_like(sin)
    cos_h = jnp.concatenate([cos, cos, pad + 1.0], axis=-1)
    sin_hi = jnp.concatenate([zero, sin, pad], axis=-1)
    sin_lo = jnp.concatenate([-sin, zero, pad], axis=-1)
    both = lambda a: jnp.concatenate([a, a], axis=-1)
    return both(cos_h), both(sin_hi), both(sin_lo), both(jnp.concatenate([sin, sin, pad], axis=-1))


def _nsa_layer(h, pre_g, w_in, gate_b, pos_k, pos_v, k_w1, k_w2, v_w1, v_w2, positions, batch, seq):
    t, d = h.shape
    pw, kw = PRIMARY_WIDTH, KV_WIDTH
    offs = np.cumsum([0, pw] + [kw] * 6 + [3 * KV_GROUPS * GROUP_HEADS, MEM_WIDTH])
    col = lambda i: w_in[:, offs[i]:offs[i + 1]]
    w_q, w_kc, w_vc, w_ks, w_vs, w_kw, w_vw, w_gl, w_qm = [col(i) for i in range(9)]

    wq_h = w_q.reshape(d, KV_GROUPS * GROUP_HEADS, 1, HEAD_DIM)
    wq_dup = jnp.broadcast_to(wq_h, (d, KV_GROUPS * GROUP_HEADS, 2, HEAD_DIM)).reshape(d, 2 * pw)
    wk_pair = jnp.stack([w_ks.reshape(d, KV_GROUPS, HEAD_DIM), w_kw.reshape(d, KV_GROUPS, HEAD_DIM)], axis=2)
    w_rope = jnp.concatenate([wq_dup, wk_pair.reshape(d, 2 * kw)], axis=1)
    scale = jnp.concatenate([jnp.full((1, 2 * pw), ATTN_SCALE, F32), jnp.ones((1, 2 * kw), F32)], axis=1)
    cos, sin_hi, sin_lo, _ = _rope_tables(positions.reshape(t))
    qk = _norm_proj(h, pre_g, w_rope, tn=w_rope.shape[1] // 3, out_dtype=BF16,
                    rope=(cos, sin_hi, sin_lo, scale), name="nsa_proj_rope")

    wv_pair = jnp.stack([w_vs.reshape(d, KV_GROUPS, HEAD_DIM), w_vw.reshape(d, KV_GROUPS, HEAD_DIM)], axis=2)
    wv_t = wv_pair.reshape(d, 2 * kw).T
    vt = _norm_proj_t(h, pre_g, wv_t, jnp.zeros((2 * kw, 1), F32), tm=ATTN_TILE, tn=2 * kw,
                      out_dtype=BF16, gate=False, name="nsa_proj_values")

    per_group = 3 * GROUP_HEADS
    wg = jnp.pad(w_gl.reshape(d, KV_GROUPS, per_group), ((0, 0), (0, 0), (0, 16 - per_group)))
    bg = jnp.pad(gate_b.reshape(KV_GROUPS, per_group), ((0, 0), (0, 16 - per_group)))
    n_gate = LANES
    wg_t = jnp.pad(wg.reshape(d, KV_GROUPS * 16).T, ((0, n_gate - KV_GROUPS * 16), (0, 0)))
    bg_t = jnp.pad(bg.reshape(KV_GROUPS * 16, 1), ((0, n_gate - KV_GROUPS * 16), (0, 0)))
    gates = _norm_proj_t(h, pre_g, wg_t, bg_t, tm=ATTN_TILE, tn=n_gate, out_dtype=F32, gate=True,
                         name="nsa_proj_gates")[:, :KV_GROUPS * 16, :]

    w_plain = jnp.concatenate([w_qm, w_kc, w_vc], axis=1)
    plain = _norm_proj(h, pre_g, w_plain, tn=w_plain.shape[1], out_dtype=F32, name="nsa_proj_plain")
    q_mem = plain[:, :MEM_WIDTH].astype(BF16)
    raw_k = plain[:, MEM_WIDTH:MEM_WIDTH + kw]
    raw_v = plain[:, MEM_WIDTH + kw:]

    n_chunk = seq // CMP_STRIDE
    cmp_end = jnp.minimum(jnp.arange(n_chunk) * CMP_STRIDE + CMP_BLOCK - 1, seq - 1)
    cos_c, _, _, sin_c = _rope_tables(positions[:, cmp_end])
    kc, vct = _compress(raw_k, raw_v, pos_k, pos_v, k_w1, k_w2, v_w1, v_w2, cos_c, sin_c, batch, seq)
    prim = _nsa_attention(qk, vt, gates, kc, vct, batch, seq)
    return prim, q_mem


def _split_dot(x, w_bf16):
    hi = x.astype(BF16)
    lo = (x - hi.astype(F32)).astype(BF16)
    return _dot(hi, w_bf16) + _dot(lo, w_bf16)


def _head_ones(width):
    head = np.arange(width) // HEAD_DIM
    return jnp.asarray(head[:, None] == head[None, :], BF16)


def _rwkv_prep_kernel(p_ref, prev_ref, mu_ref, w0_ref, w2_ref, a0_ref, a2_ref, g2_ref, kk_ref, ka_ref, rk_ref,
                      ones_ref, r_ref, k_ref, v_ref, na_ref, b_ref, lw_ref, g_ref, bonus_ref, *, tiles_per_seq):
    pw = PRIMARY_WIDTH
    x = p_ref[...]
    first_tile = (pl.program_id(0) % tiles_per_seq) == 0
    last_prev = jnp.where(first_tile, 0.0, prev_ref[SUBLANES - 1:SUBLANES, :])
    row = lax.broadcasted_iota(jnp.int32, x.shape, 0)
    prev = jnp.where(row == 0, last_prev, pltpu.roll(x, 1, 0))
    xs = x + (prev - x) * mu_ref[...]
    r = xs[:, 0:pw]
    k = xs[:, pw:2 * pw]
    v = xs[:, 2 * pw:3 * pw]
    lora = xs[:, 3 * pw:3 * pw + DECAY_LORA + AAA_LORA]
    gl = xs[:, 3 * pw + DECAY_LORA + AAA_LORA:]
    w_pre = w0_ref[...] + _dot(jnp.tanh(lora).astype(BF16), w2_ref[...])
    z = -w_pre
    w = -(jnp.maximum(z, 0.0) + jnp.log1p(jnp.exp(-jnp.abs(z)))) - 0.5
    lw_ref[...] = -jnp.exp(w)
    a = jax.nn.sigmoid(a0_ref[...] + _dot(lora.astype(BF16), a2_ref[...]))
    g_ref[...] = _dot(jax.nn.sigmoid(gl).astype(BF16), g2_ref[...])
    ones = ones_ref[...]
    kk = k * kk_ref[...]
    kk = kk * lax.rsqrt(jnp.maximum(_split_dot(kk * kk, ones), 1e-24))
    k2 = k * (1.0 + (a - 1.0) * ka_ref[...])
    r_ref[...] = r
    k_ref[...] = k2
    v_ref[...] = v
    na_ref[...] = -kk
    b_ref[...] = kk * a
    bonus_ref[...] = _split_dot(r * k2 * rk_ref[...], ones) * v


def _rwkv_prep(proj, mu, w0, w2, a0, a2, g2, k_k, k_a, r_k, seq):
    t, width = proj.shape
    pw = PRIMARY_WIDTH
    tm = min(PREP_TILE, seq)
    row = lambda a: a.reshape(1, -1)
    lora_w = DECAY_LORA + AAA_LORA
    w2e = jnp.concatenate([w2, jnp.zeros((AAA_LORA, pw), F32)], axis=0).astype(BF16)
    a2e = jnp.concatenate([jnp.zeros((DECAY_LORA, pw), F32), a2], axis=0).astype(BF16)
    const = lambda shape: pl.BlockSpec(shape, lambda i: (0, 0))
    out_spec = pl.BlockSpec((tm, pw), lambda i: (i, 0))
    return pl.pallas_call(
        functools.partial(_rwkv_prep_kernel, tiles_per_seq=seq // tm),
        grid=(t // tm,),
        in_specs=[
            pl.BlockSpec((tm, width), lambda i: (i, 0)),
            pl.BlockSpec((SUBLANES, width), lambda i: (jnp.maximum(i * (tm // SUBLANES) - 1, 0), 0)),
            const((1, width)), const((1, pw)), const((lora_w, pw)), const((1, pw)), const((lora_w, pw)),
            const((GATE_LORA, pw)), const((1, pw)), const((1, pw)), const((1, pw)), const((pw, pw)),
        ],
        out_specs=[out_spec] * 8,
        out_shape=[jax.ShapeDtypeStruct((t, pw), F32)] * 8,
        compiler_params=_params("parallel"),
        name="rwkv_prep",
    )(proj, proj, row(mu), row(w0), w2e, row(a0), a2e, g2.astype(BF16), row(k_k), row(k_a), row(r_k),
      _head_ones(pw))


def _block_diag(x, mask):
    return jnp.where(mask, jnp.concatenate([x.astype(BF16)] * GROUP_HEADS, axis=0), jnp.zeros((), BF16))


def _rwkv_scan_kernel(r_ref, k_ref, v_ref, na_ref, b_ref, lw_ref, g_ref, bonus_ref, lng_ref, lnb_ref, o_ref, state_ref):
    c = r_ref.shape[0]
    gw = GROUP_WIDTH

    @pl.when(pl.program_id(1) == 0)
    def _():
        state_ref[...] = jnp.zeros_like(state_ref)

    rr = lax.broadcasted_iota(jnp.int32, (gw, gw), 0)
    cc = lax.broadcasted_iota(jnp.int32, (gw, gw), 1)
    bd_mask = (rr // HEAD_DIM) == (cc // HEAD_DIM)
    t_idx = lax.broadcasted_iota(jnp.int32, (c, gw), 0)
    s_idx = lax.broadcasted_iota(jnp.int32, (c, gw), 1) % HEAD_DIM
    strict = t_idx > s_idx
    incl = t_idx >= s_idx
    eye = jnp.where(t_idx == s_idx, 1.0, 0.0)
    same_block = {}
    size = INV_BASE
    while size <= c:
        same_block[size] = (t_idx // size) == (s_idx // size)
        size *= 2
    tril =jnp.where(lax.broadcasted_iota(jnp.int32, (c, c), 0) >= lax.broadcasted_iota(jnp.int32, (c, c), 1),
                     1.0, 0.0).astype(BF16)
    ones_bd = jnp.where(bd_mask, 1.0, 0.0).astype(BF16)

    def cumsum_rows(x):
        hi = x.astype(BF16)
        rem = x - hi.astype(F32)
        mid = rem.astype(BF16)
        lo = (rem - mid.astype(F32)).astype(BF16)
        return _dot(tril, hi) + _dot(tril, mid) + _dot(tril, lo)

    for grp in range(KV_GROUPS):
        cols = slice(grp * gw, (grp + 1) * gw)
        r, k, v = r_ref[:, cols], k_ref[:, cols], v_ref[:, cols]
        na, bv, lw = na_ref[:, cols], b_ref[:, cols], lw_ref[:, cols]
        state = state_ref[grp]

        cum = cumsum_rows(lw)
        p_incl = jnp.exp(cum)
        inv_p = jnp.exp(-cum)
        a_t = na * jnp.exp(cum - lw)
        r_t = r * p_incl
        b_t = (bv * inv_p).astype(BF16)
        k_t = (k * inv_p).astype(BF16)
        x2 = jnp.concatenate([a_t, r_t], axis=0).astype(BF16)

        g_b = _dot_nt(x2, _block_diag(b_t, bd_mask))
        g_k = _dot_nt(x2, _block_diag(k_t, bd_mask))
        l_ab = jnp.where(strict, g_b[:c], 0.0)
        m_rb = jnp.where(incl, g_b[c:], 0.0)
        l_ak = jnp.where(strict, g_k[:c], 0.0)
        m_rk = jnp.where(incl, g_k[c:], 0.0)

        power = jnp.where(same_block[INV_BASE], l_ab, 0.0)
        t_inv = eye + power
        for _ in range(int(np.log2(INV_BASE)) - 1):
            power = _dot(power.astype(BF16), _block_diag(power, bd_mask))
            t_inv = t_inv + _dot(t_inv.astype(BF16), _block_diag(power, bd_mask))
        size = 2 * INV_BASE
        while size <= c:
            l_off = jnp.where(same_block[size] & ~same_block[size // 2], l_ab, 0.0)
            half = _dot(t_inv.astype(BF16), _block_diag(l_off, bd_mask))
            t_inv = t_inv + _dot(half.astype(BF16), _block_diag(t_inv, bd_mask))
            size *= 2

        xh = _dot_nt(x2, state.astype(BF16))
        v_bd = _block_diag(v, bd_mask)
        rhs = xh[:c] + _dot(l_ak.astype(BF16), v_bd)
        u = _dot(t_inv.astype(BF16), _block_diag(rhs, bd_mask))
        out = xh[c:] + _dot(m_rb.astype(BF16), _block_diag(u, bd_mask)) + _dot(m_rk.astype(BF16), v_bd)

        uv = jnp.concatenate([u, v], axis=0).astype(BF16)
        y2 = jnp.concatenate([b_t, k_t], axis=0)
        delta = lax.dot_general(uv, y2, _TN, preferred_element_type=F32)
        state_ref[grp] = (state + jnp.where(bd_mask, delta, 0.0)) * p_incl[c - 1:c, :]

        mean = _split_dot(out, ones_bd) * (1.0 / HEAD_DIM)
        dev = out - mean
        var = _split_dot(dev * dev, ones_bd) * (1.0 / HEAD_DIM)
        y = dev * lax.rsqrt(var + RWKV_GN_EPS) * lng_ref[:, cols] + lnb_ref[:, cols]
        o_ref[:, cols] = ((y + bonus_ref[:, cols]) * g_ref[:, cols]).astype(o_ref.dtype)


def _rwkv_scan(r, k, v, na, b, lw, g, bonus, ln_g, ln_b, batch, seq):
    c = SCAN_CHUNK
    n_chunks = seq // c
    pw = PRIMARY_WIDTH
    blk = pl.BlockSpec((c, pw), lambda bi, ci: (bi * n_chunks + ci, 0))
    const = pl.BlockSpec((1, pw), lambda bi, ci: (0, 0))
    return pl.pallas_call(
        _rwkv_scan_kernel,
        grid=(batch, n_chunks),
        in_specs=[blk] * 8 + [const, const],
        out_specs=blk,
        out_shape=jax.ShapeDtypeStruct((batch * seq, pw), BF16),
        scratch_shapes=[pltpu.VMEM((KV_GROUPS, GROUP_WIDTH, GROUP_WIDTH), F32)],
        compiler_params=_params("parallel", "arbitrary"),
        name="rwkv_scan",
    )(r, k, v, na, b, lw, g, bonus, ln_g.reshape(1, pw), ln_b.reshape(1, pw))


def _rwkv_layer(h, pre_g, w_in, mu, w0, w2, a0, a2, g2, k_k, k_a, r_k, ln_g, ln_b, batch, seq):
    proj = _norm_proj(h, pre_g, w_in[:, :RWKV_SHIFT_W], tn=RWKV_SHIFT_W // 2, out_dtype=F32, name="rwkv_proj")
    q_mem = _norm_proj(h, pre_g, w_in[:, RWKV_SHIFT_W:], tn=MEM_WIDTH, out_dtype=BF16, name="rwkv_proj_mem")
    r, k, v, na, b, lw, g, bonus = _rwkv_prep(proj, mu, w0, w2, a0, a2, g2, k_k, k_a, r_k.reshape(-1), seq)
    prim = _rwkv_scan(r, k, v, na, b, lw, g, bonus, ln_g, ln_b, batch, seq)
    return prim, q_mem


def kernel(x, mem, positions, mem_norm_g, w_mem_kv, pre_mix_g, post_mix_g, pre_ffn_g, post_ffn_g, w_out, w_ffn_in, w_ffn_out, nsa_w_in, nsa_gate_b, nsa_cmp_pos_k, nsa_cmp_pos_v, nsa_cmp_k_w1, nsa_cmp_k_w2, nsa_cmp_v_w1, nsa_cmp_v_w2, rwkv_w_in, rwkv_mu, rwkv_w0, rwkv_w2, rwkv_a0, rwkv_a2, rwkv_g2, rwkv_k_k, rwkv_k_a, rwkv_r_k, rwkv_ln_g, rwkv_ln_b):
    batch, seq, d = x.shape
    n_mem = mem.shape[1]
    depth = pre_mix_g.shape[0]
    mkv = _norm_proj(mem.reshape(batch * n_mem, d), mem_norm_g, w_mem_kv, tn=w_mem_kv.shape[1], out_dtype=BF16,
                     name="mem_kv")
    mem_k = mkv[:, :MEM_WIDTH].reshape(batch, n_mem, MEM_WIDTH)
    mem_v = mkv[:, MEM_WIDTH:].reshape(batch, n_mem, MEM_WIDTH)
    h = x.reshape(batch * seq, d)
    for i in range(depth):
        j = i // 2
        if i % 2 == 0:
            prim, q_mem = _nsa_layer(h, pre_mix_g[i], nsa_w_in[j], nsa_gate_b[j], nsa_cmp_pos_k[j], nsa_cmp_pos_v[j],
                                     nsa_cmp_k_w1[j], nsa_cmp_k_w2[j], nsa_cmp_v_w1[j], nsa_cmp_v_w2[j],
                                     positions, batch, seq)
        else:
            prim, q_mem = _rwkv_layer(h, pre_mix_g[i], rwkv_w_in[j], rwkv_mu[j], rwkv_w0[j], rwkv_w2[j], rwkv_a0[j],
                                      rwkv_a2[j], rwkv_g2[j], rwkv_k_k[j], rwkv_k_a[j], rwkv_r_k[j], rwkv_ln_g[j],
                                      rwkv_ln_b[j], batch, seq)
        mo = _mem_attn(q_mem, mem_k, mem_v, seq)
        h = _out_proj(prim, mo, w_out[i], post_mix_g[i], h)
        h = _ffn(h, pre_ffn_g[i], w_ffn_in[i], w_ffn_out[i], post_ffn_g[i])
    return h.reshape(batch, seq, d)
```

```python
import functools

import numpy as np
import jax
import jax.numpy as jnp
from jax import lax
from jax.experimental import pallas as pl
from jax.experimental.pallas import tpu as pltpu

F32 = jnp.float32
BF16 = jnp.bfloat16

HEAD_DIM = 64
ROT_DIM = HEAD_DIM // 4
ROT_HALF = ROT_DIM // 2
ROPE_THETA = 500000.0
MEM_HEADS = 4
MEM_WIDTH = MEM_HEADS * HEAD_DIM
KV_GROUPS = 3
GROUP_HEADS = 4
GROUP_WIDTH = GROUP_HEADS * HEAD_DIM
PRIMARY_WIDTH = KV_GROUPS * GROUP_WIDTH
KV_WIDTH = KV_GROUPS * HEAD_DIM
CMP_BLOCK = 32
CMP_STRIDE = 16
SEL_BLOCK = 64
SEL_TOPK = 16
WINDOW = 512
DECAY_LORA = 64
AAA_LORA = 64
GATE_LORA = 128
RWKV_SHIFT_W = 3 * PRIMARY_WIDTH + DECAY_LORA + AAA_LORA + GATE_LORA
RWKV_GN_EPS = HEAD_DIM * 1e-5
NORM_EPS = 1e-6
NEG = -1e30
FORCE = 1e6
ATTN_SCALE = HEAD_DIM ** -0.5

LANES = 128
SUBLANES = 8
VMEM_LIMIT_BYTES = 48 * 1024 * 1024

ROW_TILE = 512
ATTN_TILE = 256
SCAN_CHUNK = 64
INV_BASE = 8
SEL_SPAN = 2
WIN_TILES = WINDOW // ATTN_TILE + 1
N_FORCED = 3
ACC_ROWS = HEAD_DIM + 16
PREP_TILE = 256

_NT = (((1,), (1,)), ((), ()))
_TN = (((0,), (0,)), ((), ()))


def _params(*sem):
    return pltpu.CompilerParams(dimension_semantics=sem, vmem_limit_bytes=VMEM_LIMIT_BYTES)


def _rms(x, g):
    return x * lax.rsqrt(jnp.mean(x * x, axis=-1, keepdims=True) + NORM_EPS) * g


def _dot(a, b):
    return jnp.dot(a, b, preferred_element_type=F32)


def _dot_nt(a, b):
    return lax.dot_general(a, b, _NT, preferred_element_type=F32)


def _proj_kernel(x_ref, g_ref, w_ref, o_ref, xn_ref):
    @pl.when(pl.program_id(1) == 0)
    def _():
        xn_ref[...] = _rms(x_ref[...], g_ref[...]).astype(BF16)

    o_ref[...] = _dot(xn_ref[...], w_ref[...]).astype(o_ref.dtype)


def _proj_rope_kernel(x_ref, g_ref, w_ref, c_ref, sp_ref, sm_ref, scale_ref, o_ref, xn_ref):
    @pl.when(pl.program_id(1) == 0)
    def _():
        xn_ref[...] = _rms(x_ref[...], g_ref[...]).astype(BF16)

    y = _dot(xn_ref[...], w_ref[...])
    width = y.shape[1]
    reps = width // LANES
    cos = jnp.concatenate([c_ref[...]] * reps, axis=1)
    sin_hi = jnp.concatenate([sp_ref[...]] * reps, axis=1)
    sin_lo = jnp.concatenate([sm_ref[...]] * reps, axis=1)
    y = y * cos + pltpu.roll(y, ROT_HALF, 1) * sin_hi + pltpu.roll(y, width - ROT_HALF, 1) * sin_lo
    o_ref[...] = (y * scale_ref[...]).astype(o_ref.dtype)


def _proj_t_kernel(x_ref, g_ref, wt_ref, b_ref, o_ref, xn_ref, *, gate):
    @pl.when(pl.program_id(1) == 0)
    def _():
        xn_ref[...] = _rms(x_ref[...], g_ref[...]).astype(BF16)

    y = _dot_nt(wt_ref[...], xn_ref[...])
    if gate:
        y = jax.nn.sigmoid(y + b_ref[...])
    o_ref[0] = y.astype(o_ref.dtype)


def _norm_proj(x, g, w, *, tn, out_dtype, rope=None, name):
    t, d = x.shape
    n = w.shape[1]
    tm = min(ROW_TILE, t)
    in_specs = [
        pl.BlockSpec((tm, d), lambda i, j: (i, 0)),
        pl.BlockSpec((1, d), lambda i, j: (0, 0)),
        pl.BlockSpec((d, tn), lambda i, j: (0, j)),
    ]
    args = [x, g.reshape(1, d), w.astype(BF16)]
    if rope is None:
        body = _proj_kernel
    else:
        body = _proj_rope_kernel
        cos, sin_hi, sin_lo, scale = rope
        in_specs += [pl.BlockSpec((tm, LANES), lambda i, j: (i, 0))] * 3
        in_specs += [pl.BlockSpec((1, tn), lambda i, j: (0, j))]
        args += [cos, sin_hi, sin_lo, scale]
    return pl.pallas_call(
        body,
        grid=(t // tm, n // tn),
        in_specs=in_specs,
        out_specs=pl.BlockSpec((tm, tn), lambda i, j: (i, j)),
        out_shape=jax.ShapeDtypeStruct((t, n), out_dtype),
        scratch_shapes=[pltpu.VMEM((tm, d), BF16)],
        compiler_params=_params("parallel", "arbitrary"),
        name=name,
    )(*args)


def _norm_proj_t(x, g, wt, bias, *, tm, tn, out_dtype, gate, name):
    t, d = x.shape
    n = wt.shape[0]
    return pl.pallas_call(
        functools.partial(_proj_t_kernel, gate=gate),
        grid=(t // tm, n // tn),
        in_specs=[
            pl.BlockSpec((tm, d), lambda i, j: (i, 0)),
            pl.BlockSpec((1, d), lambda i, j: (0, 0)),
            pl.BlockSpec((tn, d), lambda i, j: (j, 0)),
            pl.BlockSpec((tn, 1), lambda i, j: (j, 0)),
        ],
        out_specs=pl.BlockSpec((1, tn, tm), lambda i, j: (i, j, 0)),
        out_shape=jax.ShapeDtypeStruct((t // tm, n, tm), out_dtype),
        scratch_shapes=[pltpu.VMEM((tm, d), BF16)],
        compiler_params=_params("parallel", "arbitrary"),
        name=name,
    )(x, g.reshape(1, d), wt.astype(BF16), bias)


def _outproj_kernel(a_ref, b_ref, wa_ref, wb_ref, g_ref, h_ref, o_ref):
    y = _dot(a_ref[...], wa_ref[...]) + _dot(b_ref[...], wb_ref[...])
    o_ref[...] = h_ref[...] + _rms(y, g_ref[...])


def _out_proj(prim, mo, w_out, g, h):
    t, d = h.shape
    tm = min(ROW_TILE, t)
    pw = prim.shape[1]
    return pl.pallas_call(
        _outproj_kernel,
        grid=(t // tm,),
        in_specs=[
            pl.BlockSpec((tm, pw), lambda i: (i, 0)),
            pl.BlockSpec((tm, MEM_WIDTH), lambda i: (i, 0)),
            pl.BlockSpec((pw, d), lambda i: (0, 0)),
            pl.BlockSpec((MEM_WIDTH, d), lambda i: (0, 0)),
            pl.BlockSpec((1, d), lambda i: (0, 0)),
            pl.BlockSpec((tm, d), lambda i: (i, 0)),
        ],
        out_specs=pl.BlockSpec((tm, d), lambda i: (i, 0)),
        out_shape=jax.ShapeDtypeStruct((t, d), F32),
        compiler_params=_params("parallel"),
        name="out_proj",
    )(prim, mo, w_out[:pw].astype(BF16), w_out[pw:].astype(BF16), g.reshape(1, d), h)


def _ffn_kernel(h_ref, g1_ref, wg_ref, wu_ref, wo_ref, g2_ref, o_ref, hn_ref, acc_ref):
    j = pl.program_id(1)

    @pl.when(j == 0)
    def _():
        hn_ref[...] = _rms(h_ref[...], g1_ref[...]).astype(BF16)
        acc_ref[...] = jnp.zeros_like(acc_ref)

    hn = hn_ref[...]
    gate = _dot(hn, wg_ref[...])
    up = _dot(hn, wu_ref[...])
    act = (jax.nn.silu(gate) * up).astype(BF16)
    acc_ref[...] += _dot(act, wo_ref[...])

    @pl.when(j == pl.num_programs(1) - 1)
    def _():
        o_ref[...] = h_ref[...] + _rms(acc_ref[...], g2_ref[...])


def _ffn_chunk(hidden):
    units = hidden // LANES
    for parts in range(2, units + 1):
        if units % parts == 0:
            return (units // parts) * LANES
    return hidden


def _ffn(h, g1, w_in, w_out, g2):
    t, d = h.shape
    hidden = w_out.shape[0]
    th = _ffn_chunk(hidden)
    nh = hidden // th
    tm = min(ROW_TILE, t)
    w_in = w_in.astype(BF16)
    return pl.pallas_call(
        _ffn_kernel,
        grid=(t // tm, nh),
        in_specs=[
            pl.BlockSpec((tm, d), lambda i, j: (i, 0)),
            pl.BlockSpec((1, d), lambda i, j: (0, 0)),
            pl.BlockSpec((d, th), lambda i, j: (0, j)),
            pl.BlockSpec((d, th), lambda i, j: (0, j + nh)),
            pl.BlockSpec((th, d), lambda i, j: (j, 0)),
            pl.BlockSpec((1, d), lambda i, j: (0, 0)),
        ],
        out_specs=pl.BlockSpec((tm, d), lambda i, j: (i, 0)),
        out_shape=jax.ShapeDtypeStruct((t, d), F32),
        scratch_shapes=[pltpu.VMEM((tm, d), BF16), pltpu.VMEM((tm, d), F32)],
        compiler_params=_params("parallel", "arbitrary"),
        name="ffn",
    )(h, g1.reshape(1, d), w_in, w_in, w_out.astype(BF16), g2.reshape(1, d))


def _mem_attn_kernel(q_ref, mk_ref, mv_ref, o_ref):
    q = q_ref[...]
    mk = mk_ref[0]
    mv = mv_ref[0]
    head_of_lane = lax.broadcasted_iota(jnp.int32, mk.shape, 1) // HEAD_DIM
    acc = jnp.zeros(q.shape, F32)
    for h in range(MEM_HEADS):
        s = _dot_nt(q, jnp.where(head_of_lane == h, mk, 0)) * ATTN_SCALE
        e = jnp.exp(s - jnp.max(s, axis=-1, keepdims=True))
        p = e / jnp.sum(e, axis=-1, keepdims=True)
        acc = acc + _dot(p.astype(BF16), jnp.where(head_of_lane == h, mv, 0))
    o_ref[...] = acc.astype(o_ref.dtype)


def _mem_attn(q_mem, mem_k, mem_v, seq):
    t = q_mem.shape[0]
    m = mem_k.shape[1]
    tm = min(ROW_TILE, seq)
    per_seq = seq // tm
    return pl.pallas_call(
        _mem_attn_kernel,
        grid=(t // tm,),
        in_specs=[
            pl.BlockSpec((tm, MEM_WIDTH), lambda i: (i, 0)),
            pl.BlockSpec((1, m, MEM_WIDTH), lambda i: (i // per_seq, 0, 0)),
            pl.BlockSpec((1, m, MEM_WIDTH), lambda i: (i // per_seq, 0, 0)),
        ],
        out_specs=pl.BlockSpec((tm, MEM_WIDTH), lambda i: (i, 0)),
        out_shape=jax.ShapeDtypeStruct((t, MEM_WIDTH), BF16),
        compiler_params=_params("parallel"),
        name="mem_attn",
    )(q_mem, mem_k, mem_v)


def _compress_kernel(xk_ref, xv_ref, pk_ref, pv_ref, k1a_ref, k1b_ref, k2_ref, v1a_ref, v1b_ref, v2t_ref,
                     c_ref, s_ref, kc_ref, vct_ref):
    def hidden(x_ref, pos_ref, wa_ref, wb_ref):
        x = x_ref[0, 0]
        n = x.shape[0]
        first = _dot((x + pos_ref[0:1, :]).astype(BF16), wa_ref[...])
        second = _dot((x + pos_ref[1:2, :]).astype(BF16), wb_ref[...])
        return jax.nn.gelu(first + pltpu.roll(second, n - 1, 0)).astype(BF16)

    hk = hidden(xk_ref, pk_ref, k1a_ref, k1b_ref)
    both = _dot(hk, k2_ref[...])
    kc_ref[0, 0] = (both[:, :LANES] * c_ref[0] + both[:, LANES:] * s_ref[0]).astype(kc_ref.dtype)
    hv = hidden(xv_ref, pv_ref, v1a_ref, v1b_ref)
    vct_ref[0, 0] = _dot_nt(v2t_ref[...], hv).astype(vct_ref.dtype)


def _rope_partner_cols(w):
    d = np.arange(w.shape[1]) % HEAD_DIM
    src = np.where(d < ROT_HALF, np.arange(w.shape[1]) + ROT_HALF, np.arange(w.shape[1]) - ROT_HALF)
    src = np.clip(src, 0, w.shape[1] - 1)
    sign = np.where(d < ROT_HALF, -1.0, np.where(d < ROT_DIM, 1.0, 0.0)).astype(np.float32)
    return w[:, src] * sign


def _compress(raw_k, raw_v, pos_k, pos_v, k_w1, k_w2, v_w1, v_w2, cos_c, sin_c, batch, seq):
    n_chunk = seq // CMP_STRIDE
    feat = CMP_STRIDE * HEAD_DIM
    hid = k_w1.shape[1]

    def chunks(raw):
        x = raw.reshape(batch, n_chunk, CMP_STRIDE, KV_GROUPS, HEAD_DIM)
        return x.transpose(0, 3, 1, 2, 4).reshape(batch, KV_GROUPS, n_chunk, feat)

    zeros = jnp.zeros((hid, LANES - HEAD_DIM), F32)
    k2 = jnp.concatenate([k_w2, zeros, _rope_partner_cols(k_w2), zeros], axis=1).astype(BF16)
    x_spec = pl.BlockSpec((1, 1, n_chunk, feat), lambda b, g: (b, g, 0, 0))
    pos_spec = pl.BlockSpec((2, feat), lambda b, g: (0, 0))
    w1_spec = pl.BlockSpec((feat, hid), lambda b, g: (0, 0))
    tab_spec = pl.BlockSpec((1, n_chunk, LANES), lambda b, g: (b, 0, 0))
    return pl.pallas_call(
        _compress_kernel,
        grid=(batch, KV_GROUPS),
        in_specs=[x_spec, x_spec, pos_spec, pos_spec, w1_spec, w1_spec,
                  pl.BlockSpec((hid, 2 * LANES), lambda b, g: (0, 0)),
                  w1_spec, w1_spec,
                  pl.BlockSpec((HEAD_DIM, hid), lambda b, g: (0, 0)),
                  tab_spec, tab_spec],
        out_specs=[pl.BlockSpec((1, 1, n_chunk, LANES), lambda b, g: (b, g, 0, 0)),
                   pl.BlockSpec((1, 1, HEAD_DIM, n_chunk), lambda b, g: (b, g, 0, 0))],
        out_shape=[jax.ShapeDtypeStruct((batch, KV_GROUPS, n_chunk, LANES), BF16),
                   jax.ShapeDtypeStruct((batch, KV_GROUPS, HEAD_DIM, n_chunk), BF16)],
        compiler_params=_params("parallel", "parallel"),
        name="nsa_compress",
    )(chunks(raw_k), chunks(raw_v), pos_k.reshape(2, feat), pos_v.reshape(2, feat),
      k_w1[:feat].astype(BF16), k_w1[feat:].astype(BF16), k2,
      v_w1[:feat].astype(BF16), v_w1[feat:].astype(BF16), v_w2.T.astype(BF16), cos_c, sin_c)


def _split_dot_left(w_bf16, x):
    hi = x.astype(BF16)
    rem = x - hi.astype(F32)
    mid = rem.astype(BF16)
    lo = (rem - mid.astype(F32)).astype(BF16)
    return _dot(w_bf16, hi) + _dot(w_bf16, mid) + _dot(w_bf16, lo)


def _nsa_attn_kernel(q_ref, kk_ref, hot_ref, vt_ref, kc_ref, vct_ref, tap_ref, gate_ref, o_ref,
                     qx_ref, m_ref, acc_ref):
    tq = q_ref.shape[0]
    tk = tq
    qt = pl.program_id(2)
    t0 = qt * tq
    heads = range(GROUP_HEADS)
    head_rows = lambda h: slice(h * tq, (h + 1) * tq)

    q2 = q_ref[...]
    qs = jnp.concatenate([q2[:, h * LANES:(h + 1) * LANES] for h in heads], axis=0)
    lane = lax.broadcasted_iota(jnp.int32, qs.shape, 1)
    q_sel = jnp.where(lane < HEAD_DIM, qs, 0)
    q_win = jnp.where(lane >= HEAD_DIM, qs, 0)

    col_t = t0 + lax.broadcasted_iota(jnp.int32, (1, tq), 1)

    n_cmp = kc_ref.shape[2]
    blk_end = lax.broadcasted_iota(jnp.int32, (n_cmp, 1), 0) * CMP_STRIDE + (CMP_BLOCK - 1)
    valid = blk_end <= col_t
    cmp_scores = [_dot_nt(kc_ref[0, 0], q_sel[head_rows(h)]) for h in heads]
    o_cmp = []
    p_sum = jnp.zeros((n_cmp, tq), F32)
    for h in heads:
        s = jnp.where(valid, cmp_scores[h], NEG)
        top = jnp.max(s, axis=0, keepdims=True)
        e = jnp.exp(s - top)
        norm = jnp.where(top > 0.5 * NEG, 1.0 / jnp.sum(e, axis=0, keepdims=True), 0.0)
        p = e * norm
        o_cmp.append(_dot(vct_ref[0, 0], p.astype(BF16)))
        p_sum = p_sum + p
    imp = _split_dot_left(tap_ref[...], p_sum)

    n_sel = imp.shape[0]
    blk = lax.broadcasted_iota(jnp.int32, (n_sel, tq), 0)
    blk_f = blk.astype(F32)
    cur = col_t // SEL_BLOCK
    forced = (blk == 0) | (blk == cur) | (blk == cur - 1)
    work = jnp.where(forced, -jnp.inf, jnp.where(blk <= cur, imp, -FORCE))
    chosen = jnp.where(forced, 1.0, 0.0)
    for _ in range(max(min(SEL_TOPK, n_sel) - N_FORCED, 0)):
        best = jnp.max(work, axis=0, keepdims=True)
        hit = blk_f == jnp.min(jnp.where(work == best, blk_f, float(n_sel)), axis=0, keepdims=True)
        work = jnp.where(hit, -jnp.inf, work)
        chosen = jnp.where(hit, 1.0, chosen)
    if n_sel < LANES:
        chosen = jnp.concatenate([chosen, jnp.zeros((LANES - n_sel, tq), F32)], axis=0)
    block_bias = ((chosen - 1.0) * (-NEG)).T.astype(BF16)
    qx_ref[:, 0:LANES] = q_sel
    qx_ref[:, LANES:2 * LANES] = jnp.concatenate([block_bias] * GROUP_HEADS, axis=0)

    def key_pos(kt, n):
        return kt * tk + lax.broadcasted_iota(jnp.int32, (n * tk, 1), 0)

    def key_rows(ref, kt, n):
        return ref[pl.ds(pl.multiple_of(kt * tk, tk), n * tk), :]

    def value_rows(kt, n, first_row):
        vals = jnp.concatenate([vt_ref[kt + i, first_row:first_row + HEAD_DIM, :] for i in range(n)], axis=1)
        return jnp.concatenate([vals, jnp.ones((ACC_ROWS - HEAD_DIM, n * tk), BF16)], axis=0)

    def softmax_pv(scores, v_ext):
        top = jnp.max(scores, axis=0, keepdims=True)
        return _dot(v_ext, jnp.exp(scores - top).astype(BF16)), top

    def normalized(acc):
        return acc[0:HEAD_DIM] / acc[HEAD_DIM:HEAD_DIM + 1]

    win_start = jnp.maximum(qt - (WIN_TILES - 1), 0)
    win_pos = key_pos(win_start, WIN_TILES)
    win_mask = (win_pos <= col_t) & ((col_t - win_pos) < WINDOW)
    win_keys = key_rows(kk_ref, win_start, WIN_TILES)
    win_vals = value_rows(win_start, WIN_TILES, HEAD_DIM)
    win_scores = [_dot_nt(win_keys, q_win[head_rows(h)]) for h in heads]
    o_win = [normalized(softmax_pv(jnp.where(win_mask, win_scores[h], NEG), win_vals)[0]) for h in heads]

    m_ref[...] = jnp.full(m_ref.shape, NEG, F32)
    acc_ref[...] = jnp.zeros(acc_ref.shape, F32)

    def score_span(i):
        kt = i * SEL_SPAN
        keys = jnp.concatenate([key_rows(kk_ref, kt, SEL_SPAN), key_rows(hot_ref, kt, SEL_SPAN)], axis=1)
        return tuple(_dot_nt(keys, qx_ref[head_rows(h), :]) for h in heads)

    def flash_span(i, span_scores, causal):
        kt = i * SEL_SPAN
        v_ext = value_rows(kt, SEL_SPAN, 0)
        for h in heads:
            cols = head_rows(h)
            scores = span_scores[h]
            if causal:
                scores = jnp.where(key_pos(kt, SEL_SPAN) <= col_t, scores, NEG)
            m_old = m_ref[:, cols]
            m_new = jnp.maximum(m_old, jnp.max(scores, axis=0, keepdims=True))
            e = jnp.exp(scores - m_new).astype(BF16)
            acc_ref[:, cols] = jnp.exp(m_old - m_new) * acc_ref[:, cols] + _dot(v_ext, e)
            m_ref[:, cols] = m_new

    def sel_step(i, span_scores):
        following = score_span(i + 1)
        flash_span(i, span_scores, False)
        return following

    full_spans = qt // SEL_SPAN
    last_scores = lax.fori_loop(0, full_spans, sel_step, score_span(0))
    flash_span(full_spans, last_scores, True)

    gates = gate_ref[...]
    outs = []
    for h in heads:
        outs.append(gates[3 * h:3 * h + 1, :] * o_cmp[h]
                    + gates[3 * h + 1:3 * h + 2, :] * normalized(acc_ref[:, head_rows(h)])
                    + gates[3 * h + 2:3 * h + 3, :] * o_win[h])
    o_ref[...] = jnp.concatenate(outs, axis=0).T.astype(o_ref.dtype)


def _tap_matrix(n_sel, n_chunk):
    ratio = SEL_BLOCK // CMP_STRIDE
    tap = np.zeros((n_sel, n_chunk), np.float32)
    n_cmp = n_chunk - (CMP_BLOCK // CMP_STRIDE - 1)
    for j in range(n_sel):
        for n in range(n_cmp):
            lo = max(n * CMP_STRIDE, j * SEL_BLOCK)
            hi = min(n * CMP_STRIDE + CMP_BLOCK, (j + 1) * SEL_BLOCK)
            if hi > lo:
                tap[j, n] = (hi - lo) / CMP_STRIDE
    return jnp.asarray(tap)


def _nsa_attention(qk, vt, gates, kc, vct, batch, seq):
    tq = ATTN_TILE
    n_tiles = seq // tq
    n_chunk = kc.shape[2]
    n_sel = seq // SEL_BLOCK
    q_blocks = GROUP_HEADS * LANES
    k_col0 = KV_GROUPS * q_blocks // LANES
    vt5 = vt.reshape(batch, n_tiles, KV_GROUPS, 2 * HEAD_DIM, tq)
    gates5 = gates.reshape(batch, n_tiles, KV_GROUPS, 16, tq)
    width = GROUP_HEADS * tq
    assert n_sel <= LANES, "block one-hot is one lane tile wide"
    assert n_tiles % SEL_SPAN == 0 and n_tiles >= WIN_TILES
    hot = jnp.asarray((np.arange(seq)[:, None] // SEL_BLOCK) == np.arange(LANES)[None, :], BF16)
    return pl.pallas_call(
        _nsa_attn_kernel,
        grid=(batch, KV_GROUPS, n_tiles),
        in_specs=[
            pl.BlockSpec((tq, q_blocks), lambda b, g, i: (b * n_tiles + i, g)),
            pl.BlockSpec((seq, LANES), lambda b, g, i: (b, k_col0 + g)),
            pl.BlockSpec((seq, LANES), lambda b, g, i: (0, 0)),
            pl.BlockSpec((None, n_tiles, None, 2 * HEAD_DIM, tq), lambda b, g, i: (b, 0, g, 0, 0)),
            pl.BlockSpec((1, 1, n_chunk, LANES), lambda b, g, i: (b, g, 0, 0)),
            pl.BlockSpec((1, 1, HEAD_DIM, n_chunk), lambda b, g, i: (b, g, 0, 0)),
            pl.BlockSpec((n_sel, n_chunk), lambda b, g, i: (0, 0)),
            pl.BlockSpec((None, None, None, 16, tq), lambda b, g, i: (b, i, g, 0, 0)),
        ],
        out_specs=pl.BlockSpec((tq, GROUP_WIDTH), lambda b, g, i: (b * n_tiles + i, g)),
        out_shape=jax.ShapeDtypeStruct((batch * seq, PRIMARY_WIDTH), BF16),
        scratch_shapes=[
            pltpu.VMEM((width, 2 * LANES), BF16),
            pltpu.VMEM((1, width), F32),
            pltpu.VMEM((ACC_ROWS, width), F32),
        ],
        compiler_params=_params("parallel", "parallel", "arbitrary"),
        name="nsa_attention",
    )(qk, qk, hot, vt5, kc, vct, _tap_matrix(n_sel, n_chunk), gates5)


def _rope_tables(positions):
    inv = ROPE_THETA ** (-jnp.arange(0, ROT_DIM, 2, dtype=F32) / ROT_DIM)
    ang = positions.astype(F32)[..., None] * inv
    cos, sin = jnp.cos(ang), jnp.sin(ang)
    pad = jnp.zeros(ang.shape[:-1] + (HEAD_DIM - ROT_DIM,), F32)
    zero = jnp.zeros_like(sin)
    cos_h = jnp.concatenate([cos, cos, pad + 1.0], axis=-1)
    sin_hi = jnp.concatenate([zero, sin, pad], axis=-1)
    sin_lo = jnp.concatenate([-sin, zero, pad], axis=-1)
    both = lambda a: jnp.concatenate([a, a], axis=-1)
    return both(cos_h), both(sin_hi), both(sin_lo), both(jnp.concatenate([sin, sin, pad], axis=-1))


def _nsa_layer(h, pre_g, w_in, gate_b, pos_k, pos_v, k_w1, k_w2, v_w1, v_w2, positions, batch, seq):
    t, d = h.shape
    pw, kw = PRIMARY_WIDTH, KV_WIDTH
    offs = np.cumsum([0, pw] + [kw] * 6 + [3 * KV_GROUPS * GROUP_HEADS, MEM_WIDTH])
    col = lambda i: w_in[:, offs[i]:offs[i + 1]]
    w_q, w_kc, w_vc, w_ks, w_vs, w_kw, w_vw, w_gl, w_qm = [col(i) for i in range(9)]

    wq_h = w_q.reshape(d, KV_GROUPS * GROUP_HEADS, 1, HEAD_DIM)
    wq_dup = jnp.broadcast_to(wq_h, (d, KV_GROUPS * GROUP_HEADS, 2, HEAD_DIM)).reshape(d, 2 * pw)
    wk_pair = jnp.stack([w_ks.reshape(d, KV_GROUPS, HEAD_DIM), w_kw.reshape(d, KV_GROUPS, HEAD_DIM)], axis=2)
    w_rope = jnp.concatenate([wq_dup, wk_pair.reshape(d, 2 * kw)], axis=1)
    scale = jnp.concatenate([jnp.full((1, 2 * pw), ATTN_SCALE, F32), jnp.ones((1, 2 * kw), F32)], axis=1)
    cos, sin_hi, sin_lo, _ = _rope_tables(positions.reshape(t))
    qk = _norm_proj(h, pre_g, w_rope, tn=w_rope.shape[1] // 3, out_dtype=BF16,
                    rope=(cos, sin_hi, sin_lo, scale), name="nsa_proj_rope")

    wv_pair = jnp.stack([w_vs.reshape(d, KV_GROUPS, HEAD_DIM), w_vw.reshape(d, KV_GROUPS, HEAD_DIM)], axis=2)
    wv_t = wv_pair.reshape(d, 2 * kw).T
    vt = _norm_proj_t(h, pre_g, wv_t, jnp.zeros((2 * kw, 1), F32), tm=ATTN_TILE, tn=2 * kw,
                      out_dtype=BF16, gate=False, name="nsa_proj_values")

    per_group = 3 * GROUP_HEADS
    wg = jnp.pad(w_gl.reshape(d, KV_GROUPS, per_group), ((0, 0), (0, 0), (0, 16 - per_group)))
    bg = jnp.pad(gate_b.reshape(KV_GROUPS, per_group), ((0, 0), (0, 16 - per_group)))
    n_gate = LANES
    wg_t = jnp.pad(wg.reshape(d, KV_GROUPS * 16).T, ((0, n_gate - KV_GROUPS * 16), (0, 0)))
    bg_t = jnp.pad(bg.reshape(KV_GROUPS * 16, 1), ((0, n_gate - KV_GROUPS * 16), (0, 0)))
    gates = _norm_proj_t(h, pre_g, wg_t, bg_t, tm=ATTN_TILE, tn=n_gate, out_dtype=F32, gate=True,
                         name="nsa_proj_gates")[:, :KV_GROUPS * 16, :]

    w_plain = jnp.concatenate([w_qm, w_kc, w_vc], axis=1)
    plain = _norm_proj(h, pre_g, w_plain, tn=w_plain.shape[1], out_dtype=F32, name="nsa_proj_plain")
    q_mem = plain[:, :MEM_WIDTH].astype(BF16)
    raw_k = plain[:, MEM_WIDTH:MEM_WIDTH + kw]
    raw_v = plain[:, MEM_WIDTH + kw:]

    n_chunk = seq // CMP_STRIDE
    cmp_end = jnp.minimum(jnp.arange(n_chunk) * CMP_STRIDE + CMP_BLOCK - 1, seq - 1)
    cos_c, _, _, sin_c = _rope_tables(positions[:, cmp_end])
    kc, vct = _compress(raw_k, raw_v, pos_k, pos_v, k_w1, k_w2, v_w1, v_w2, cos_c, sin_c, batch, seq)
    prim = _nsa_attention(qk, vt, gates, kc, vct, batch, seq)
    return prim, q_mem


def _split_dot(x, w_bf16):
    hi = x.astype(BF16)
    lo = (x - hi.astype(F32)).astype(BF16)
    return _dot(hi, w_bf16) + _dot(lo, w_bf16)


def _head_ones(width):
    head = np.arange(width) // HEAD_DIM
    return jnp.asarray(head[:, None] == head[None, :], BF16)


def _rwkv_prep_kernel(p_ref, prev_ref, mu_ref, w0_ref, w2_ref, a0_ref, a2_ref, g2_ref, kk_ref, ka_ref, rk_ref,
                      ones_ref, r_ref, k_ref, v_ref, na_ref, b_ref, lw_ref, g_ref, bonus_ref, *, tiles_per_seq):
    pw = PRIMARY_WIDTH
    x = p_ref[...]
    first_tile = (pl.program_id(0) % tiles_per_seq) == 0
    last_prev = jnp.where(first_tile, 0.0, prev_ref[SUBLANES - 1:SUBLANES, :])
    row = lax.broadcasted_iota(jnp.int32, x.shape, 0)
    prev = jnp.where(row == 0, last_prev, pltpu.roll(x, 1, 0))
    xs = x + (prev - x) * mu_ref[...]
    r = xs[:, 0:pw]
    k = xs[:, pw:2 * pw]
    v = xs[:, 2 * pw:3 * pw]
    lora = xs[:, 3 * pw:3 * pw + DECAY_LORA + AAA_LORA]
    gl = xs[:, 3 * pw + DECAY_LORA + AAA_LORA:]
    w_pre = w0_ref[...] + _dot(jnp.tanh(lora).astype(BF16), w2_ref[...])
    z = -w_pre
    w = -(jnp.maximum(z, 0.0) + jnp.log1p(jnp.exp(-jnp.abs(z)))) - 0.5
    lw_ref[...] = -jnp.exp(w)
    a = jax.nn.sigmoid(a0_ref[...] + _dot(lora.astype(BF16), a2_ref[...]))
    g_ref[...] = _dot(jax.nn.sigmoid(gl).astype(BF16), g2_ref[...])
    ones = ones_ref[...]
    kk = k * kk_ref[...]
    kk = kk * lax.rsqrt(jnp.maximum(_split_dot(kk * kk, ones), 1e-24))
    k2 = k * (1.0 + (a - 1.0) * ka_ref[...])
    r_ref[...] = r
    k_ref[...] = k2
    v_ref[...] = v
    na_ref[...] = -kk
    b_ref[...] = kk * a
    bonus_ref[...] = _split_dot(r * k2 * rk_ref[...], ones) * v


def _rwkv_prep(proj, mu, w0, w2, a0, a2, g2, k_k, k_a, r_k, seq):
    t, width = proj.shape
    pw = PRIMARY_WIDTH
    tm = min(PREP_TILE, seq)
    row = lambda a: a.reshape(1, -1)
    lora_w = DECAY_LORA + AAA_LORA
    w2e = jnp.concatenate([w2, jnp.zeros((AAA_LORA, pw), F32)], axis=0).astype(BF16)
    a2e = jnp.concatenate([jnp.zeros((DECAY_LORA, pw), F32), a2], axis=0).astype(BF16)
    const = lambda shape: pl.BlockSpec(shape, lambda i: (0, 0))
    out_spec = pl.BlockSpec((tm, pw), lambda i: (i, 0))
    return pl.pallas_call(
        functools.partial(_rwkv_prep_kernel, tiles_per_seq=seq // tm),
        grid=(t // tm,),
        in_specs=[
            pl.BlockSpec((tm, width), lambda i: (i, 0)),
            pl.BlockSpec((SUBLANES, width), lambda i: (jnp.maximum(i * (tm // SUBLANES) - 1, 0), 0)),
            const((1, width)), const((1, pw)), const((lora_w, pw)), const((1, pw)), const((lora_w, pw)),
            const((GATE_LORA, pw)), const((1, pw)), const((1, pw)), const((1, pw)), const((pw, pw)),
        ],
        out_specs=[out_spec] * 8,
        out_shape=[jax.ShapeDtypeStruct((t, pw), F32)] * 8,
        compiler_params=_params("parallel"),
        name="rwkv_prep",
    )(proj, proj, row(mu), row(w0), w2e, row(a0), a2e, g2.astype(BF16), row(k_k), row(k_a), row(r_k),
      _head_ones(pw))


def _block_diag(x, mask):
    return jnp.where(mask, jnp.concatenate([x.astype(BF16)] * GROUP_HEADS, axis=0), jnp.zeros((), BF16))


def _rwkv_scan_kernel(r_ref, k_ref, v_ref, na_ref, b_ref, lw_ref, g_ref, bonus_ref, lng_ref, lnb_ref, o_ref, state_ref):
    c = r_ref.shape[0]
    gw = GROUP_WIDTH

    @pl.when(pl.program_id(1) == 0)
    def _():
        state_ref[...] = jnp.zeros_like(state_ref)

    rr = lax.broadcasted_iota(jnp.int32, (gw, gw), 0)
    cc = lax.broadcasted_iota(jnp.int32, (gw, gw), 1)
    bd_mask = (rr // HEAD_DIM) == (cc // HEAD_DIM)
    t_idx = lax.broadcasted_iota(jnp.int32, (c, gw), 0)
    s_idx = lax.broadcasted_iota(jnp.int32, (c, gw), 1) % HEAD_DIM
    strict = t_idx > s_idx
    incl = t_idx >= s_idx
    eye = jnp.where(t_idx == s_idx, 1.0, 0.0)
    same_block = {}
    size = INV_BASE
    while size <= c:
        same_block[size] = (t_idx // size) == (s_idx // size)
        size *= 2
    tril =jnp.where(lax.broadcasted_iota(jnp.int32, (c, c), 0) >= lax.broadcasted_iota(jnp.int32, (c, c), 1),
                     1.0, 0.0).astype(BF16)
    ones_bd = jnp.where(bd_mask, 1.0, 0.0).astype(BF16)

    def cumsum_rows(x):
        hi = x.astype(BF16)
        rem = x - hi.astype(F32)
        mid = rem.astype(BF16)
        lo = (rem - mid.astype(F32)).astype(BF16)
        return _dot(tril, hi) + _dot(tril, mid) + _dot(tril, lo)

    for grp in range(KV_GROUPS):
        cols = slice(grp * gw, (grp + 1) * gw)
        r, k, v = r_ref[:, cols], k_ref[:, cols], v_ref[:, cols]
        na, bv, lw = na_ref[:, cols], b_ref[:, cols], lw_ref[:, cols]
        state = state_ref[grp]

        cum = cumsum_rows(lw)
        p_incl = jnp.exp(cum)
        inv_p = jnp.exp(-cum)
        a_t = na * jnp.exp(cum - lw)
        r_t = r * p_incl
        b_t = (bv * inv_p).astype(BF16)
        k_t = (k * inv_p).astype(BF16)
        x2 = jnp.concatenate([a_t, r_t], axis=0).astype(BF16)

        g_b = _dot_nt(x2, _block_diag(b_t, bd_mask))
        g_k = _dot_nt(x2, _block_diag(k_t, bd_mask))
        l_ab = jnp.where(strict, g_b[:c], 0.0)
        m_rb = jnp.where(incl, g_b[c:], 0.0)
        l_ak = jnp.where(strict, g_k[:c], 0.0)
        m_rk = jnp.where(incl, g_k[c:], 0.0)

        power = jnp.where(same_block[INV_BASE], l_ab, 0.0)
        t_inv = eye + power
        for _ in range(int(np.log2(INV_BASE)) - 1):
            power = _dot(power.astype(BF16), _block_diag(power, bd_mask))
            t_inv = t_inv + _dot(t_inv.astype(BF16), _block_diag(power, bd_mask))
        size = 2 * INV_BASE
        while size <= c:
            l_off = jnp.where(same_block[size] & ~same_block[size // 2], l_ab, 0.0)
            half = _dot(t_inv.astype(BF16), _block_diag(l_off, bd_mask))
            t_inv = t_inv + _dot(half.astype(BF16), _block_diag(t_inv, bd_mask))
            size *= 2

        xh = _dot_nt(x2, state.astype(BF16))
        v_bd = _block_diag(v, bd_mask)
        rhs = xh[:c] + _dot(l_ak.astype(BF16), v_bd)
        u = _dot(t_inv.astype(BF16), _block_diag(rhs, bd_mask))
        out = xh[c:] + _dot(m_rb.astype(BF16), _block_diag(u, bd_mask)) + _dot(m_rk.astype(BF16), v_bd)

        uv = jnp.concatenate([u, v], axis=0).astype(BF16)
        y2 = jnp.concatenate([b_t, k_t], axis=0)
        delta = lax.dot_general(uv, y2, _TN, preferred_element_type=F32)
        state_ref[grp] = (state + jnp.where(bd_mask, delta, 0.0)) * p_incl[c - 1:c, :]

        mean = _split_dot(out, ones_bd) * (1.0 / HEAD_DIM)
        dev = out - mean
        var = _split_dot(dev * dev, ones_bd) * (1.0 / HEAD_DIM)
        y = dev * lax.rsqrt(var + RWKV_GN_EPS) * lng_ref[:, cols] + lnb_ref[:, cols]
        o_ref[:, cols] = ((y + bonus_ref[:, cols]) * g_ref[:, cols]).astype(o_ref.dtype)


def _rwkv_scan(r, k, v, na, b, lw, g, bonus, ln_g, ln_b, batch, seq):
    c = SCAN_CHUNK
    n_chunks = seq // c
    pw = PRIMARY_WIDTH
    blk = pl.BlockSpec((c, pw), lambda bi, ci: (bi * n_chunks + ci, 0))
    const = pl.BlockSpec((1, pw), lambda bi, ci: (0, 0))
    return pl.pallas_call(
        _rwkv_scan_kernel,
        grid=(batch, n_chunks),
        in_specs=[blk] * 8 + [const, const],
        out_specs=blk,
        out_shape=jax.ShapeDtypeStruct((batch * seq, pw), BF16),
        scratch_shapes=[pltpu.VMEM((KV_GROUPS, GROUP_WIDTH, GROUP_WIDTH), F32)],
        compiler_params=_params("parallel", "arbitrary"),
        name="rwkv_scan",
    )(r, k, v, na, b, lw, g, bonus, ln_g.reshape(1, pw), ln_b.reshape(1, pw))


def _rwkv_layer(h, pre_g, w_in, mu, w0, w2, a0, a2, g2, k_k, k_a, r_k, ln_g, ln_b, batch, seq):
    proj = _norm_proj(h, pre_g, w_in[:, :RWKV_SHIFT_W], tn=RWKV_SHIFT_W // 2, out_dtype=F32, name="rwkv_proj")
    q_mem = _norm_proj(h, pre_g, w_in[:, RWKV_SHIFT_W:], tn=MEM_WIDTH, out_dtype=BF16, name="rwkv_proj_mem")
    r, k, v, na, b, lw, g, bonus = _rwkv_prep(proj, mu, w0, w2, a0, a2, g2, k_k, k_a, r_k.reshape(-1), seq)
    prim = _rwkv_scan(r, k, v, na, b, lw, g, bonus, ln_g, ln_b, batch, seq)
    return prim, q_mem


def kernel(x, mem, positions, mem_norm_g, w_mem_kv, pre_mix_g, post_mix_g, pre_ffn_g, post_ffn_g, w_out, w_ffn_in, w_ffn_out, nsa_w_in, nsa_gate_b, nsa_cmp_pos_k, nsa_cmp_pos_v, nsa_cmp_k_w1, nsa_cmp_k_w2, nsa_cmp_v_w1, nsa_cmp_v_w2, rwkv_w_in, rwkv_mu, rwkv_w0, rwkv_w2, rwkv_a0, rwkv_a2, rwkv_g2, rwkv_k_k, rwkv_k_a, rwkv_r_k, rwkv_ln_g, rwkv_ln_b):
    batch, seq, d = x.shape
    n_mem = mem.shape[1]
    depth = pre_mix_g.shape[0]
    mkv = _norm_proj(mem.reshape(batch * n_mem, d), mem_norm_g, w_mem_kv, tn=w_mem_kv.shape[1], out_dtype=BF16,
                     name="mem_kv")
    mem_k = mkv[:, :MEM_WIDTH].reshape(batch, n_mem, MEM_WIDTH)
    mem_v = mkv[:, MEM_WIDTH:].reshape(batch, n_mem, MEM_WIDTH)
    h = x.reshape(batch * seq, d)
    for i in range(depth):
        j = i // 2
        if i % 2 == 0:
            prim, q_mem = _nsa_layer(h, pre_mix_g[i], nsa_w_in[j], nsa_gate_b[j], nsa_cmp_pos_k[j], nsa_cmp_pos_v[j],
                                     nsa_cmp_k_w1[j], nsa_cmp_k_w2[j], nsa_cmp_v_w1[j], nsa_cmp_v_w2[j],
                                     positions, batch, seq)
        else:
            prim, q_mem = _rwkv_layer(h, pre_mix_g[i], rwkv_w_in[j], rwkv_mu[j], rwkv_w0[j], rwkv_w2[j], rwkv_a0[j],
                                      rwkv_a2[j], rwkv_g2[j], rwkv_k_k[j], rwkv_k_a[j], rwkv_r_k[j], rwkv_ln_g[j],
                                      rwkv_ln_b[j], batch, seq)
        mo = _mem_attn(q_mem, mem_k, mem_v, seq)
        h = _out_proj(prim, mo, w_out[i], post_mix_g[i], h)
        h = _ffn(h, pre_ffn_g[i], w_ffn_in[i], w_ffn_out[i], post_ffn_g[i])
    return h.reshape(batch, seq, d)
```

```python
import functools

import numpy as np
import jax
import jax.numpy as jnp
from jax import lax
from jax.experimental import pallas as pl
from jax.experimental.pallas import tpu as pltpu

F32 = jnp.float32
BF16 = jnp.bfloat16

HEAD_DIM = 64
ROT_DIM = HEAD_DIM // 4
ROT_HALF = ROT_DIM // 2
ROPE_THETA = 500000.0
MEM_HEADS = 4
MEM_WIDTH = MEM_HEADS * HEAD_DIM
KV_GROUPS = 3
GROUP_HEADS = 4
GROUP_WIDTH = GROUP_HEADS * HEAD_DIM
PRIMARY_WIDTH = KV_GROUPS * GROUP_WIDTH
KV_WIDTH = KV_GROUPS * HEAD_DIM
CMP_BLOCK = 32
CMP_STRIDE = 16
SEL_BLOCK = 64
SEL_TOPK = 16
WINDOW = 512
DECAY_LORA = 64
AAA_LORA = 64
GATE_LORA = 128
RWKV_SHIFT_W = 3 * PRIMARY_WIDTH + DECAY_LORA + AAA_LORA + GATE_LORA
RWKV_GN_EPS = HEAD_DIM * 1e-5
NORM_EPS = 1e-6
NEG = -1e30
FORCE = 1e6
ATTN_SCALE = HEAD_DIM ** -0.5
LOG2E = float(np.log2(np.e))
SAFE_LOG2 = 80.0

LANES = 128
SUBLANES = 8
VMEM_LIMIT_BYTES = 48 * 1024 * 1024

ROW_TILE = 512
ATTN_TILE = 256
SCAN_CHUNK = 64
INV_BASE = 8
SEL_SPAN = 2
WIN_TILES = WINDOW // ATTN_TILE + 1
N_FORCED = 3
ACC_ROWS = HEAD_DIM + 16
PREP_TILE = 256

_NT = (((1,), (1,)), ((), ()))
_TN = (((0,), (0,)), ((), ()))


def _params(*sem):
    return pltpu.CompilerParams(dimension_semantics=sem, vmem_limit_bytes=VMEM_LIMIT_BYTES)


def _rms(x, g):
    return x * lax.rsqrt(jnp.mean(x * x, axis=-1, keepdims=True) + NORM_EPS) * g


def _dot(a, b):
    return jnp.dot(a, b, preferred_element_type=F32)


def _dot_nt(a, b):
    return lax.dot_general(a, b, _NT, preferred_element_type=F32)


def _proj_kernel(x_ref, g_ref, w_ref, o_ref, xn_ref):
    @pl.when(pl.program_id(1) == 0)
    def _():
        xn_ref[...] = _rms(x_ref[...], g_ref[...]).astype(BF16)

    o_ref[...] = _dot(xn_ref[...], w_ref[...]).astype(o_ref.dtype)


def _proj_rope_kernel(x_ref, g_ref, w_ref, c_ref, sp_ref, sm_ref, scale_ref, o_ref, xn_ref):
    @pl.when(pl.program_id(1) == 0)
    def _():
        xn_ref[...] = _rms(x_ref[...], g_ref[...]).astype(BF16)

    y = _dot(xn_ref[...], w_ref[...])
    width = y.shape[1]
    reps = width // LANES
    cos = jnp.concatenate([c_ref[...]] * reps, axis=1)
    sin_hi = jnp.concatenate([sp_ref[...]] * reps, axis=1)
    sin_lo = jnp.concatenate([sm_ref[...]] * reps, axis=1)
    y = y * cos + pltpu.roll(y, ROT_HALF, 1) * sin_hi + pltpu.roll(y, width - ROT_HALF, 1) * sin_lo
    o_ref[...] = (y * scale_ref[...]).astype(o_ref.dtype)


def _proj_t_kernel(x_ref, g_ref, wt_ref, b_ref, o_ref, xn_ref, *, gate):
    @pl.when(pl.program_id(1) == 0)
    def _():
        xn_ref[...] = _rms(x_ref[...], g_ref[...]).astype(BF16)

    y = _dot_nt(wt_ref[...], xn_ref[...])
    if gate:
        y = jax.nn.sigmoid(y + b_ref[...])
    o_ref[0] = y.astype(o_ref.dtype)


def _norm_proj(x, g, w, *, tn, out_dtype, rope=None, name):
    t, d = x.shape
    n = w.shape[1]
    tm = min(ROW_TILE, t)
    in_specs = [
        pl.BlockSpec((tm, d), lambda i, j: (i, 0)),
        pl.BlockSpec((1, d), lambda i, j: (0, 0)),
        pl.BlockSpec((d, tn), lambda i, j: (0, j)),
    ]
    args = [x, g.reshape(1, d), w.astype(BF16)]
    if rope is None:
        body = _proj_kernel
    else:
        body = _proj_rope_kernel
        cos, sin_hi, sin_lo, scale = rope
        in_specs += [pl.BlockSpec((tm, LANES), lambda i, j: (i, 0))] * 3
        in_specs += [pl.BlockSpec((1, tn), lambda i, j: (0, j))]
        args += [cos, sin_hi, sin_lo, scale]
    return pl.pallas_call(
        body,
        grid=(t // tm, n // tn),
        in_specs=in_specs,
        out_specs=pl.BlockSpec((tm, tn), lambda i, j: (i, j)),
        out_shape=jax.ShapeDtypeStruct((t, n), out_dtype),
        scratch_shapes=[pltpu.VMEM((tm, d), BF16)],
        compiler_params=_params("parallel", "arbitrary"),
        name=name,
    )(*args)


def _norm_proj_t(x, g, wt, bias, *, tm, tn, out_dtype, gate, name):
    t, d = x.shape
    n = wt.shape[0]
    return pl.pallas_call(
        functools.partial(_proj_t_kernel, gate=gate),
        grid=(t // tm, n // tn),
        in_specs=[
            pl.BlockSpec((tm, d), lambda i, j: (i, 0)),
            pl.BlockSpec((1, d), lambda i, j: (0, 0)),
            pl.BlockSpec((tn, d), lambda i, j: (j, 0)),
            pl.BlockSpec((tn, 1), lambda i, j: (j, 0)),
        ],
        out_specs=pl.BlockSpec((1, tn, tm), lambda i, j: (i, j, 0)),
        out_shape=jax.ShapeDtypeStruct((t // tm, n, tm), out_dtype),
        scratch_shapes=[pltpu.VMEM((tm, d), BF16)],
        compiler_params=_params("parallel", "arbitrary"),
        name=name,
    )(x, g.reshape(1, d), wt.astype(BF16), bias)


def _outproj_kernel(a_ref, b_ref, wa_ref, wb_ref, g_ref, h_ref, o_ref):
    y = _dot(a_ref[...], wa_ref[...]) + _dot(b_ref[...], wb_ref[...])
    o_ref[...] = h_ref[...] + _rms(y, g_ref[...])


def _out_proj(prim, mo, w_out, g, h):
    t, d = h.shape
    tm = min(ROW_TILE, t)
    pw = prim.shape[1]
    return pl.pallas_call(
        _outproj_kernel,
        grid=(t // tm,),
        in_specs=[
            pl.BlockSpec((tm, pw), lambda i: (i, 0)),
            pl.BlockSpec((tm, MEM_WIDTH), lambda i: (i, 0)),
            pl.BlockSpec((pw, d), lambda i: (0, 0)),
            pl.BlockSpec((MEM_WIDTH, d), lambda i: (0, 0)),
            pl.BlockSpec((1, d), lambda i: (0, 0)),
            pl.BlockSpec((tm, d), lambda i: (i, 0)),
        ],
        out_specs=pl.BlockSpec((tm, d), lambda i: (i, 0)),
        out_shape=jax.ShapeDtypeStruct((t, d), F32),
        compiler_params=_params("parallel"),
        name="out_proj",
    )(prim, mo, w_out[:pw].astype(BF16), w_out[pw:].astype(BF16), g.reshape(1, d), h)


def _ffn_kernel(h_ref, g1_ref, wg_ref, wu_ref, wo_ref, g2_ref, o_ref, hn_ref, acc_ref):
    j = pl.program_id(1)

    @pl.when(j == 0)
    def _():
        hn_ref[...] = _rms(h_ref[...], g1_ref[...]).astype(BF16)
        acc_ref[...] = jnp.zeros_like(acc_ref)

    hn = hn_ref[...]
    gate = _dot(hn, wg_ref[...])
    up = _dot(hn, wu_ref[...])
    act = (jax.nn.silu(gate) * up).astype(BF16)
    acc_ref[...] += _dot(act, wo_ref[...])

    @pl.when(j == pl.num_programs(1) - 1)
    def _():
        o_ref[...] = h_ref[...] + _rms(acc_ref[...], g2_ref[...])


def _ffn_chunk(hidden):
    units = hidden // LANES
    for parts in range(2, units + 1):
        if units % parts == 0:
            return (units // parts) * LANES
    return hidden


def _ffn(h, g1, w_in, w_out, g2):
    t, d = h.shape
    hidden = w_out.shape[0]
    th = _ffn_chunk(hidden)
    nh = hidden // th
    tm = min(ROW_TILE, t)
    w_in = w_in.astype(BF16)
    return pl.pallas_call(
        _ffn_kernel,
        grid=(t // tm, nh),
        in_specs=[
            pl.BlockSpec((tm, d), lambda i, j: (i, 0)),
            pl.BlockSpec((1, d), lambda i, j: (0, 0)),
            pl.BlockSpec((d, th), lambda i, j: (0, j)),
            pl.BlockSpec((d, th), lambda i, j: (0, j + nh)),
            pl.BlockSpec((th, d), lambda i, j: (j, 0)),
            pl.BlockSpec((1, d), lambda i, j: (0, 0)),
        ],
        out_specs=pl.BlockSpec((tm, d), lambda i, j: (i, 0)),
        out_shape=jax.ShapeDtypeStruct((t, d), F32),
        scratch_shapes=[pltpu.VMEM((tm, d), BF16), pltpu.VMEM((tm, d), F32)],
        compiler_params=_params("parallel", "arbitrary"),
        name="ffn",
    )(h, g1.reshape(1, d), w_in, w_in, w_out.astype(BF16), g2.reshape(1, d))


def _mem_attn_kernel(q_ref, mk_ref, mv_ref, o_ref):
    q = q_ref[...]
    mk = mk_ref[0]
    mv = mv_ref[0]
    head_of_lane = lax.broadcasted_iota(jnp.int32, mk.shape, 1) // HEAD_DIM
    acc = jnp.zeros(q.shape, F32)
    for h in range(MEM_HEADS):
        s = _dot_nt(q, jnp.where(head_of_lane == h, mk, 0)) * ATTN_SCALE
        e = jnp.exp(s - jnp.max(s, axis=-1, keepdims=True))
        p = e / jnp.sum(e, axis=-1, keepdims=True)
        acc = acc + _dot(p.astype(BF16), jnp.where(head_of_lane == h, mv, 0))
    o_ref[...] = acc.astype(o_ref.dtype)


def _mem_attn(q_mem, mem_k, mem_v, seq):
    t = q_mem.shape[0]
    m = mem_k.shape[1]
    tm = min(ROW_TILE, seq)
    per_seq = seq // tm
    return pl.pallas_call(
        _mem_attn_kernel,
        grid=(t // tm,),
        in_specs=[
            pl.BlockSpec((tm, MEM_WIDTH), lambda i: (i, 0)),
            pl.BlockSpec((1, m, MEM_WIDTH), lambda i: (i // per_seq, 0, 0)),
            pl.BlockSpec((1, m, MEM_WIDTH), lambda i: (i // per_seq, 0, 0)),
        ],
        out_specs=pl.BlockSpec((tm, MEM_WIDTH), lambda i: (i, 0)),
        out_shape=jax.ShapeDtypeStruct((t, MEM_WIDTH), BF16),
        compiler_params=_params("parallel"),
        name="mem_attn",
    )(q_mem, mem_k, mem_v)


def _compress_kernel(xk_ref, xv_ref, pk_ref, pv_ref, k1a_ref, k1b_ref, k2_ref, v1a_ref, v1b_ref, v2t_ref,
                     c_ref, s_ref, kc_ref, vct_ref):
    def hidden(x_ref, pos_ref, wa_ref, wb_ref):
        x = x_ref[0, 0]
        n = x.shape[0]
        first = _dot((x + pos_ref[0:1, :]).astype(BF16), wa_ref[...])
        second = _dot((x + pos_ref[1:2, :]).astype(BF16), wb_ref[...])
        return jax.nn.gelu(first + pltpu.roll(second, n - 1, 0)).astype(BF16)

    hk = hidden(xk_ref, pk_ref, k1a_ref, k1b_ref)
    both = _dot(hk, k2_ref[...])
    kc_ref[0, 0] = (both[:, :LANES] * c_ref[0] + both[:, LANES:] * s_ref[0]).astype(kc_ref.dtype)
    hv = hidden(xv_ref, pv_ref, v1a_ref, v1b_ref)
    vct_ref[0, 0] = _dot_nt(v2t_ref[...], hv).astype(vct_ref.dtype)


def _rope_partner_cols(w):
    d = np.arange(w.shape[1]) % HEAD_DIM
    src = np.where(d < ROT_HALF, np.arange(w.shape[1]) + ROT_HALF, np.arange(w.shape[1]) - ROT_HALF)
    src = np.clip(src, 0, w.shape[1] - 1)
    sign = np.where(d < ROT_HALF, -1.0, np.where(d < ROT_DIM, 1.0, 0.0)).astype(np.float32)
    return w[:, src] * sign


def _compress(raw_k, raw_v, pos_k, pos_v, k_w1, k_w2, v_w1, v_w2, cos_c, sin_c, batch, seq):
    n_chunk = seq // CMP_STRIDE
    feat = CMP_STRIDE * HEAD_DIM
    hid = k_w1.shape[1]

    def chunks(raw):
        x = raw.reshape(batch, n_chunk, CMP_STRIDE, KV_GROUPS, HEAD_DIM)
        return x.transpose(0, 3, 1, 2, 4).reshape(batch, KV_GROUPS, n_chunk, feat)

    zeros = jnp.zeros((hid, LANES - HEAD_DIM), F32)
    k2 = jnp.concatenate([k_w2, zeros, _rope_partner_cols(k_w2), zeros], axis=1).astype(BF16)
    x_spec = pl.BlockSpec((1, 1, n_chunk, feat), lambda b, g: (b, g, 0, 0))
    pos_spec = pl.BlockSpec((2, feat), lambda b, g: (0, 0))
    w1_spec = pl.BlockSpec((feat, hid), lambda b, g: (0, 0))
    tab_spec = pl.BlockSpec((1, n_chunk, LANES), lambda b, g: (b, 0, 0))
    return pl.pallas_call(
        _compress_kernel,
        grid=(batch, KV_GROUPS),
        in_specs=[x_spec, x_spec, pos_spec, pos_spec, w1_spec, w1_spec,
                  pl.BlockSpec((hid, 2 * LANES), lambda b, g: (0, 0)),
                  w1_spec, w1_spec,
                  pl.BlockSpec((HEAD_DIM, hid), lambda b, g: (0, 0)),
                  tab_spec, tab_spec],
        out_specs=[pl.BlockSpec((1, 1, n_chunk, LANES), lambda b, g: (b, g, 0, 0)),
                   pl.BlockSpec((1, 1, HEAD_DIM, n_chunk), lambda b, g: (b, g, 0, 0))],
        out_shape=[jax.ShapeDtypeStruct((batch, KV_GROUPS, n_chunk, LANES), BF16),
                   jax.ShapeDtypeStruct((batch, KV_GROUPS, HEAD_DIM, n_chunk), BF16)],
        compiler_params=_params("parallel", "parallel"),
        name="nsa_compress",
    )(chunks(raw_k), chunks(raw_v), pos_k.reshape(2, feat), pos_v.reshape(2, feat),
      k_w1[:feat].astype(BF16), k_w1[feat:].astype(BF16), k2,
      v_w1[:feat].astype(BF16), v_w1[feat:].astype(BF16), v_w2.T.astype(BF16), cos_c, sin_c)


def _split_dot_left(w_bf16, x):
    hi = x.astype(BF16)
    rem = x - hi.astype(F32)
    mid = rem.astype(BF16)
    lo = (rem - mid.astype(F32)).astype(BF16)
    return _dot(w_bf16, hi) + _dot(w_bf16, mid) + _dot(w_bf16, lo)


def _nsa_attn_kernel(q_ref, kk_ref, hot_ref, vt_ref, kc_ref, vct_ref, tap_ref, gate_ref, o_ref,
                     qx_ref, m_ref, acc_ref, kmax_ref):
    tq = q_ref.shape[0]
    tk = tq
    qt = pl.program_id(2)
    t0 = qt * tq
    heads = range(GROUP_HEADS)
    head_rows = lambda h: slice(h * tq, (h + 1) * tq)

    q2 = q_ref[...]
    qs = jnp.concatenate([q2[:, h * LANES:(h + 1) * LANES] for h in heads], axis=0)
    lane = lax.broadcasted_iota(jnp.int32, qs.shape, 1)
    q_sel = jnp.where(lane < HEAD_DIM, qs, 0)
    q_win = jnp.where(lane >= HEAD_DIM, qs, 0)

    col_t = t0 + lax.broadcasted_iota(jnp.int32, (1, tq), 1)

    n_cmp = kc_ref.shape[2]
    blk_end = lax.broadcasted_iota(jnp.int32, (n_cmp, 1), 0) * CMP_STRIDE + (CMP_BLOCK - 1)
    valid = blk_end <= col_t
    cmp_scores = [_dot_nt(kc_ref[0, 0], q_sel[head_rows(h)]) for h in heads]
    o_cmp = []
    p_sum = jnp.zeros((n_cmp, tq), F32)
    for h in heads:
        s = jnp.where(valid, cmp_scores[h], NEG)
        top = jnp.max(s, axis=0, keepdims=True)
        e = jnp.exp2(s - top)
        norm = jnp.where(top > 0.5 * NEG, 1.0 / jnp.sum(e, axis=0, keepdims=True), 0.0)
        p = e * norm
        o_cmp.append(_dot(vct_ref[0, 0], p.astype(BF16)))
        p_sum = p_sum + p
    imp = _split_dot_left(tap_ref[...], p_sum)

    n_sel = imp.shape[0]
    blk = lax.broadcasted_iota(jnp.int32, (n_sel, tq), 0)
    blk_f = blk.astype(F32)
    cur = col_t // SEL_BLOCK
    forced = (blk == 0) | (blk == cur) | (blk == cur - 1)
    work = jnp.where(forced, -jnp.inf, jnp.where(blk <= cur, imp, -FORCE))
    chosen = jnp.where(forced, 1.0, 0.0)
    for _ in range(max(min(SEL_TOPK, n_sel) - N_FORCED, 0)):
        best = jnp.max(work, axis=0, keepdims=True)
        hit = blk_f == jnp.min(jnp.where(work == best, blk_f, float(n_sel)), axis=0, keepdims=True)
        work = jnp.where(hit, -jnp.inf, work)
        chosen = jnp.where(hit, 1.0, chosen)
    if n_sel < LANES:
        chosen = jnp.concatenate([chosen, jnp.zeros((LANES - n_sel, tq), F32)], axis=0)
    block_bias = ((chosen - 1.0) * (-NEG)).T.astype(BF16)
    qx_ref[:, 0:LANES] = q_sel
    qx_ref[:, LANES:2 * LANES] = jnp.concatenate([block_bias] * GROUP_HEADS, axis=0)

    def key_pos(kt, n):
        return kt * tk + lax.broadcasted_iota(jnp.int32, (n * tk, 1), 0)

    def key_rows(ref, kt, n):
        return ref[pl.ds(pl.multiple_of(kt * tk, tk), n * tk), :]

    def value_rows(kt, n, first_row):
        vals = jnp.concatenate([vt_ref[kt + i, first_row:first_row + HEAD_DIM, :] for i in range(n)], axis=1)
        return jnp.concatenate([vals, jnp.ones((ACC_ROWS - HEAD_DIM, n * tk), BF16)], axis=0)

    def softmax_pv(scores, v_ext):
        top = jnp.max(scores, axis=0, keepdims=True)
        return _dot(v_ext, jnp.exp2(scores - top).astype(BF16)), top

    def normalized(acc):
        return acc[0:HEAD_DIM] / acc[HEAD_DIM:HEAD_DIM + 1]

    win_start = jnp.maximum(qt - (WIN_TILES - 1), 0)
    win_pos = key_pos(win_start, WIN_TILES)
    win_mask = (win_pos <= col_t) & ((col_t - win_pos) < WINDOW)
    win_keys = key_rows(kk_ref, win_start, WIN_TILES)
    win_vals = value_rows(win_start, WIN_TILES, HEAD_DIM)
    win_scores = [_dot_nt(win_keys, q_win[head_rows(h)]) for h in heads]
    o_win = [normalized(softmax_pv(jnp.where(win_mask, win_scores[h], NEG), win_vals)[0]) for h in heads]

    acc_ref[...] = jnp.zeros(acc_ref.shape, F32)
    full_spans = qt // SEL_SPAN

    def score_span(i):
        kt = i * SEL_SPAN
        keys = jnp.concatenate([key_rows(kk_ref, kt, SEL_SPAN), key_rows(hot_ref, kt, SEL_SPAN)], axis=1)
        return tuple(_dot_nt(keys, qx_ref[head_rows(h), :]) for h in heads)

    def causal_scores(i, scores):
        return jnp.where(key_pos(i * SEL_SPAN, SEL_SPAN) <= col_t, scores, NEG)

    @pl.when(qt == 0)
    def _():
        k_all = kk_ref[...].astype(F32)
        kmax_ref[0] = jnp.max(jnp.sum(k_all * k_all, axis=1, keepdims=True))
    q_f32 = qs.astype(F32)
    q_sq = 0.5 * jnp.max(jnp.sum(q_f32 * q_f32, axis=1, keepdims=True))
    bounded = q_sq * kmax_ref[0] <= SAFE_LOG2 * SAFE_LOG2

    @pl.when(bounded)
    def _():
        def plain_span(i, causal):
            v_ext = value_rows(i * SEL_SPAN, SEL_SPAN, 0)
            scores = score_span(i)
            for h in heads:
                e = jnp.exp2(causal_scores(i, scores[h]) if causal else scores[h]).astype(BF16)
                acc_ref[:, head_rows(h)] += _dot(v_ext, e)

        def plain_step(i, carry):
            plain_span(i, False)
            return carry

        lax.fori_loop(0, full_spans, plain_step, 0)
        plain_span(full_spans, True)

    @pl.when(jnp.logical_not(bounded))
    def _():
        m_ref[...] = jnp.full(m_ref.shape, NEG, F32)

        def flash_span(i, span_scores, causal):
            v_ext = value_rows(i * SEL_SPAN, SEL_SPAN, 0)
            for h in heads:
                cols = head_rows(h)
                scores = causal_scores(i, span_scores[h]) if causal else span_scores[h]
                m_old = m_ref[:, cols]
                m_new = jnp.maximum(m_old, jnp.max(scores, axis=0, keepdims=True))
                e = jnp.exp2(scores - m_new).astype(BF16)
                acc_ref[:, cols] = jnp.exp2(m_old - m_new) * acc_ref[:, cols] + _dot(v_ext, e)
                m_ref[:, cols] = m_new

        def sel_step(i, span_scores):
            following = score_span(i + 1)
            flash_span(i, span_scores, False)
            return following

        last_scores = lax.fori_loop(0, full_spans, sel_step, score_span(0))
        flash_span(full_spans, last_scores, True)

    gates = gate_ref[...]
    outs = []
    for h in heads:
        outs.append(gates[3 * h:3 * h + 1, :] * o_cmp[h]
                    + gates[3 * h + 1:3 * h + 2, :] * normalized(acc_ref[:, head_rows(h)])
                    + gates[3 * h + 2:3 * h + 3, :] * o_win[h])
    o_ref[...] = jnp.concatenate(outs, axis=0).T.astype(o_ref.dtype)


def _tap_matrix(n_sel, n_chunk):
    ratio = SEL_BLOCK // CMP_STRIDE
    tap = np.zeros((n_sel, n_chunk), np.float32)
    n_cmp = n_chunk - (CMP_BLOCK // CMP_STRIDE - 1)
    for j in range(n_sel):
        for n in range(n_cmp):
            lo = max(n * CMP_STRIDE, j * SEL_BLOCK)
            hi = min(n * CMP_STRIDE + CMP_BLOCK, (j + 1) * SEL_BLOCK)
            if hi > lo:
                tap[j, n] = (hi - lo) / CMP_STRIDE
    return jnp.asarray(tap)


def _nsa_attention(qk, vt, gates, kc, vct, batch, seq):
    tq = ATTN_TILE
    n_tiles = seq // tq
    n_chunk = kc.shape[2]
    n_sel = seq // SEL_BLOCK
    q_blocks = GROUP_HEADS * LANES
    k_col0 = KV_GROUPS * q_blocks // LANES
    vt5 = vt.reshape(batch, n_tiles, KV_GROUPS, 2 * HEAD_DIM, tq)
    gates5 = gates.reshape(batch, n_tiles, KV_GROUPS, 16, tq)
    width = GROUP_HEADS * tq
    assert n_sel <= LANES, "block one-hot is one lane tile wide"
    assert n_tiles % SEL_SPAN == 0 and n_tiles >= WIN_TILES
    hot = jnp.asarray((np.arange(seq)[:, None] // SEL_BLOCK) == np.arange(LANES)[None, :], BF16)
    return pl.pallas_call(
        _nsa_attn_kernel,
        grid=(batch, KV_GROUPS, n_tiles),
        in_specs=[
            pl.BlockSpec((tq, q_blocks), lambda b, g, i: (b * n_tiles + i, g)),
            pl.BlockSpec((seq, LANES), lambda b, g, i: (b, k_col0 + g)),
            pl.BlockSpec((seq, LANES), lambda b, g, i: (0, 0)),
            pl.BlockSpec((None, n_tiles, None, 2 * HEAD_DIM, tq), lambda b, g, i: (b, 0, g, 0, 0)),
            pl.BlockSpec((1, 1, n_chunk, LANES), lambda b, g, i: (b, g, 0, 0)),
            pl.BlockSpec((1, 1, HEAD_DIM, n_chunk), lambda b, g, i: (b, g, 0, 0)),
            pl.BlockSpec((n_sel, n_chunk), lambda b, g, i: (0, 0)),
            pl.BlockSpec((None, None, None, 16, tq), lambda b, g, i: (b, i, g, 0, 0)),
        ],
        out_specs=pl.BlockSpec((tq, GROUP_WIDTH), lambda b, g, i: (b * n_tiles + i, g)),
        out_shape=jax.ShapeDtypeStruct((batch * seq, PRIMARY_WIDTH), BF16),
        scratch_shapes=[
            pltpu.VMEM((width, 2 * LANES), BF16),
            pltpu.VMEM((1, width), F32),
            pltpu.VMEM((ACC_ROWS, width), F32),
            pltpu.SMEM((1,), F32),
        ],
        compiler_params=_params("parallel", "parallel", "arbitrary"),
        name="nsa_attention",
    )(qk, qk, hot, vt5, kc, vct, _tap_matrix(n_sel, n_chunk), gates5)


def _rope_tables(positions):
    inv = ROPE_THETA ** (-jnp.arange(0, ROT_DIM, 2, dtype=F32) / ROT_DIM)
    ang = positions.astype(F32)[..., None] * inv
    cos, sin = jnp.cos(ang), jnp.sin(ang)
    pad = jnp.zeros(ang.shape[:-1] + (HEAD_DIM - ROT_DIM,), F32)
    zero = jnp.zeros_like(sin)
    cos_h = jnp.concatenate([cos, cos, pad + 1.0], axis=-1)
    sin_hi = jnp.concatenate([zero, sin, pad], axis=-1)
    sin_lo = jnp.concatenate([-sin, zero, pad], axis=-1)
    both = lambda a: jnp.concatenate([a, a], axis=-1)
    return both(cos_h), both(sin_hi), both(sin_lo), both(jnp.concatenate([sin, sin, pad], axis=-1))


def _nsa_layer(h, pre_g, w_in, gate_b, pos_k, pos_v, k_w1, k_w2, v_w1, v_w2, positions, batch, seq):
    t, d = h.shape
    pw, kw = PRIMARY_WIDTH, KV_WIDTH
    offs = np.cumsum([0, pw] + [kw] * 6 + [3 * KV_GROUPS * GROUP_HEADS, MEM_WIDTH])
    col = lambda i: w_in[:, offs[i]:offs[i + 1]]
    w_q, w_kc, w_vc, w_ks, w_vs, w_kw, w_vw, w_gl, w_qm = [col(i) for i in range(9)]

    wq_h = w_q.reshape(d, KV_GROUPS * GROUP_HEADS, 1, HEAD_DIM)
    wq_dup = jnp.broadcast_to(wq_h, (d, KV_GROUPS * GROUP_HEADS, 2, HEAD_DIM)).reshape(d, 2 * pw)
    wk_pair = jnp.stack([w_ks.reshape(d, KV_GROUPS, HEAD_DIM), w_kw.reshape(d, KV_GROUPS, HEAD_DIM)], axis=2)
    w_rope = jnp.concatenate([wq_dup, wk_pair.reshape(d, 2 * kw)], axis=1)
    scale = jnp.concatenate([jnp.full((1, 2 * pw), ATTN_SCALE * LOG2E, F32), jnp.ones((1, 2 * kw), F32)], axis=1)
    cos, sin_hi, sin_lo, _ = _rope_tables(positions.reshape(t))
    qk = _norm_proj(h, pre_g, w_rope, tn=w_rope.shape[1] // 3, out_dtype=BF16,
                    rope=(cos, sin_hi, sin_lo, scale), name="nsa_proj_rope")

    wv_pair = jnp.stack([w_vs.reshape(d, KV_GROUPS, HEAD_DIM), w_vw.reshape(d, KV_GROUPS, HEAD_DIM)], axis=2)
    wv_t = wv_pair.reshape(d, 2 * kw).T
    vt = _norm_proj_t(h, pre_g, wv_t, jnp.zeros((2 * kw, 1), F32), tm=ATTN_TILE, tn=2 * kw,
                      out_dtype=BF16, gate=False, name="nsa_proj_values")

    per_group = 3 * GROUP_HEADS
    wg = jnp.pad(w_gl.reshape(d, KV_GROUPS, per_group), ((0, 0), (0, 0), (0, 16 - per_group)))
    bg = jnp.pad(gate_b.reshape(KV_GROUPS, per_group), ((0, 0), (0, 16 - per_group)))
    n_gate = LANES
    wg_t = jnp.pad(wg.reshape(d, KV_GROUPS * 16).T, ((0, n_gate - KV_GROUPS * 16), (0, 0)))
    bg_t = jnp.pad(bg.reshape(KV_GROUPS * 16, 1), ((0, n_gate - KV_GROUPS * 16), (0, 0)))
    gates = _norm_proj_t(h, pre_g, wg_t, bg_t, tm=ATTN_TILE, tn=n_gate, out_dtype=F32, gate=True,
                         name="nsa_proj_gates")[:, :KV_GROUPS * 16, :]

    w_plain = jnp.concatenate([w_qm, w_kc, w_vc], axis=1)
    plain = _norm_proj(h, pre_g, w_plain, tn=w_plain.shape[1], out_dtype=F32, name="nsa_proj_plain")
    q_mem = plain[:, :MEM_WIDTH].astype(BF16)
    raw_k = plain[:, MEM_WIDTH:MEM_WIDTH + kw]
    raw_v = plain[:, MEM_WIDTH + kw:]

    n_chunk = seq // CMP_STRIDE
    cmp_end = jnp.minimum(jnp.arange(n_chunk) * CMP_STRIDE + CMP_BLOCK - 1, seq - 1)
    cos_c, _, _, sin_c = _rope_tables(positions[:, cmp_end])
    kc, vct = _compress(raw_k, raw_v, pos_k, pos_v, k_w1, k_w2, v_w1, v_w2, cos_c, sin_c, batch, seq)
    prim = _nsa_attention(qk, vt, gates, kc, vct, batch, seq)
    return prim, q_mem


def _split_dot(x, w_bf16):
    hi = x.astype(BF16)
    lo = (x - hi.astype(F32)).astype(BF16)
    return _dot(hi, w_bf16) + _dot(lo, w_bf16)


def _head_ones(width):
    head = np.arange(width) // HEAD_DIM
    return jnp.asarray(head[:, None] == head[None, :], BF16)


def _rwkv_prep_kernel(p_ref, prev_ref, mu_ref, w0_ref, w2_ref, a0_ref, a2_ref, g2_ref, kk_ref, ka_ref, rk_ref,
                      ones_ref, r_ref, k_ref, v_ref, na_ref, b_ref, lw_ref, g_ref, bonus_ref, *, tiles_per_seq):
    pw = PRIMARY_WIDTH
    x = p_ref[...]
    first_tile = (pl.program_id(0) % tiles_per_seq) == 0
    last_prev = jnp.where(first_tile, 0.0, prev_ref[SUBLANES - 1:SUBLANES, :])
    row = lax.broadcasted_iota(jnp.int32, x.shape, 0)
    prev = jnp.where(row == 0, last_prev, pltpu.roll(x, 1, 0))
    xs = x + (prev - x) * mu_ref[...]
    r = xs[:, 0:pw]
    k = xs[:, pw:2 * pw]
    v = xs[:, 2 * pw:3 * pw]
    lora = xs[:, 3 * pw:3 * pw + DECAY_LORA + AAA_LORA]
    gl = xs[:, 3 * pw + DECAY_LORA + AAA_LORA:]
    w_pre = w0_ref[...] + _dot(jnp.tanh(lora).astype(BF16), w2_ref[...])
    z = -w_pre
    w = -(jnp.maximum(z, 0.0) + jnp.log1p(jnp.exp(-jnp.abs(z)))) - 0.5
    lw_ref[...] = -jnp.exp(w)
    a = jax.nn.sigmoid(a0_ref[...] + _dot(lora.astype(BF16), a2_ref[...]))
    g_ref[...] = _dot(jax.nn.sigmoid(gl).astype(BF16), g2_ref[...])
    ones = ones_ref[...]
    kk = k * kk_ref[...]
    kk = kk * lax.rsqrt(jnp.maximum(_split_dot(kk * kk, ones), 1e-24))
    k2 = k * (1.0 + (a - 1.0) * ka_ref[...])
    r_ref[...] = r
    k_ref[...] = k2
    v_ref[...] = v
    na_ref[...] = -kk
    b_ref[...] = kk * a
    bonus_ref[...] = _split_dot(r * k2 * rk_ref[...], ones) * v


def _rwkv_prep(proj, mu, w0, w2, a0, a2, g2, k_k, k_a, r_k, seq):
    t, width = proj.shape
    pw = PRIMARY_WIDTH
    tm = min(PREP_TILE, seq)
    row = lambda a: a.reshape(1, -1)
    lora_w = DECAY_LORA + AAA_LORA
    w2e = jnp.concatenate([w2, jnp.zeros((AAA_LORA, pw), F32)], axis=0).astype(BF16)
    a2e = jnp.concatenate([jnp.zeros((DECAY_LORA, pw), F32), a2], axis=0).astype(BF16)
    const = lambda shape: pl.BlockSpec(shape, lambda i: (0, 0))
    out_spec = pl.BlockSpec((tm, pw), lambda i: (i, 0))
    return pl.pallas_call(
        functools.partial(_rwkv_prep_kernel, tiles_per_seq=seq // tm),
        grid=(t // tm,),
        in_specs=[
            pl.BlockSpec((tm, width), lambda i: (i, 0)),
            pl.BlockSpec((SUBLANES, width), lambda i: (jnp.maximum(i * (tm // SUBLANES) - 1, 0), 0)),
            const((1, width)), const((1, pw)), const((lora_w, pw)), const((1, pw)), const((lora_w, pw)),
            const((GATE_LORA, pw)), const((1, pw)), const((1, pw)), const((1, pw)), const((pw, pw)),
        ],
        out_specs=[out_spec] * 8,
        out_shape=[jax.ShapeDtypeStruct((t, pw), F32)] * 8,
        compiler_params=_params("parallel"),
        name="rwkv_prep",
    )(proj, proj, row(mu), row(w0), w2e, row(a0), a2e, g2.astype(BF16), row(k_k), row(k_a), row(r_k),
      _head_ones(pw))


def _block_diag(x, mask):
    return jnp.where(mask, jnp.concatenate([x.astype(BF16)] * GROUP_HEADS, axis=0), jnp.zeros((), BF16))


def _rwkv_scan_kernel(r_ref, k_ref, v_ref, na_ref, b_ref, lw_ref, g_ref, bonus_ref, lng_ref, lnb_ref, o_ref, state_ref):
    n_batch, c, _ = r_ref.shape
    gw = GROUP_WIDTH
    units = [(bi, grp) for bi in range(n_batch) for grp in range(KV_GROUPS)]
    every = lambda fn, *lists: [fn(*args) for args in zip(*lists)] if lists else [fn(u) for u in units]

    @pl.when(pl.program_id(0) == 0)
    def _():
        state_ref[...] = jnp.zeros_like(state_ref)

    rr = lax.broadcasted_iota(jnp.int32, (gw, gw), 0)
    cc = lax.broadcasted_iota(jnp.int32, (gw, gw), 1)
    bd_mask = (rr // HEAD_DIM) == (cc // HEAD_DIM)
    t_idx = lax.broadcasted_iota(jnp.int32, (c, gw), 0)
    s_idx = lax.broadcasted_iota(jnp.int32, (c, gw), 1) % HEAD_DIM
    strict = t_idx > s_idx
    incl = t_idx >= s_idx
    eye = jnp.where(t_idx == s_idx, 1.0, 0.0)
    same_block = {}
    size = INV_BASE
    while size <= c:
        same_block[size] = (t_idx // size) == (s_idx // size)
        size *= 2
    tril =jnp.where(lax.broadcasted_iota(jnp.int32, (c, c), 0) >= lax.broadcasted_iota(jnp.int32, (c, c), 1),
                     1.0, 0.0).astype(BF16)
    ones_bd = jnp.where(bd_mask, 1.0, 0.0).astype(BF16)

    def cumsum_rows(x):
        hi = x.astype(BF16)
        rem = x - hi.astype(F32)
        mid = rem.astype(BF16)
        lo = (rem - mid.astype(F32)).astype(BF16)
        return _dot(tril, hi) + _dot(tril, mid) + _dot(tril, lo)

    bd = lambda x: _block_diag(x, bd_mask)
    cols = lambda grp: slice(grp * gw, (grp + 1) * gw)
    load = lambda ref: [ref[bi, :, cols(grp)] for bi, grp in units]
    r, k, v, na, bv, lw = (load(ref) for ref in (r_ref, k_ref, v_ref, na_ref, b_ref, lw_ref))
    state = [state_ref[bi * KV_GROUPS + grp] for bi, grp in units]

    cum = every(cumsum_rows, lw)
    p_incl = every(jnp.exp, cum)
    inv_p = every(lambda x: jnp.exp(-x), cum)
    b_t = every(lambda x, s: (x * s).astype(BF16), bv, inv_p)
    k_t = every(lambda x, s: (x * s).astype(BF16), k, inv_p)
    x2 = every(lambda a, cu, l, rr_, p: jnp.concatenate([a * jnp.exp(cu - l), rr_ * p], axis=0).astype(BF16),
               na, cum, lw, r, p_incl)

    g_b = every(lambda x, y: _dot_nt(x, bd(y)), x2, b_t)
    g_k = every(lambda x, y: _dot_nt(x, bd(y)), x2, k_t)
    xh = every(lambda x, s: _dot_nt(x, s.astype(BF16)), x2, state)
    l_ab = every(lambda g: jnp.where(strict, g[:c], 0.0), g_b)
    m_rb = every(lambda g: jnp.where(incl, g[c:], 0.0).astype(BF16), g_b)
    l_ak = every(lambda g: jnp.where(strict, g[:c], 0.0).astype(BF16), g_k)
    m_rk = every(lambda g: jnp.where(incl, g[c:], 0.0).astype(BF16), g_k)

    power = every(lambda l: jnp.where(same_block[INV_BASE], l, 0.0), l_ab)
    t_inv = every(lambda p: eye + p, power)
    for _ in range(int(np.log2(INV_BASE)) - 1):
        power = every(lambda p: _dot(p.astype(BF16), bd(p)), power)
        t_inv = every(lambda t, p: t + _dot(t.astype(BF16), bd(p)), t_inv, power)
    size = 2 * INV_BASE
    while size <= c:
        off_mask = same_block[size] & ~same_block[size // 2]
        half = every(lambda t, l: _dot(t.astype(BF16), bd(jnp.where(off_mask, l, 0.0))), t_inv, l_ab)
        t_inv = every(lambda t, hf: t + _dot(hf.astype(BF16), bd(t)), t_inv, half)
        size *= 2

    v_bd = every(bd, v)
    rhs = every(lambda x, l, vb: x[:c] + _dot(l, vb), xh, l_ak, v_bd)
    u = every(lambda t, rh: _dot(t.astype(BF16), bd(rh)), t_inv, rhs)
    out = every(lambda x, mb, uu, mk, vb: x[c:] + _dot(mb, bd(uu)) + _dot(mk, vb), xh, m_rb, u, m_rk, v_bd)
    delta = every(lambda uu, vv, b, kk_: lax.dot_general(jnp.concatenate([uu, vv], axis=0).astype(BF16),
                                                         jnp.concatenate([b, kk_], axis=0), _TN,
                                                         preferred_element_type=F32), u, v, b_t, k_t)
    mean = every(lambda o: _split_dot(o, ones_bd) * (1.0 / HEAD_DIM), out)
    dev = every(lambda o, m: o - m, out, mean)
    var = every(lambda dv: _split_dot(dv * dv, ones_bd) * (1.0 / HEAD_DIM), dev)
    for i, (bi, grp) in enumerate(units):
        state_ref[bi * KV_GROUPS + grp] = (state[i] + jnp.where(bd_mask, delta[i], 0.0)) * p_incl[i][c - 1:c, :]
        y = dev[i] * lax.rsqrt(var[i] + RWKV_GN_EPS) * lng_ref[:, cols(grp)] + lnb_ref[:, cols(grp)]
        o_ref[bi, :, cols(grp)] = ((y + bonus_ref[bi, :, cols(grp)]) * g_ref[bi, :, cols(grp)]).astype(o_ref.dtype)


def _rwkv_scan(r, k, v, na, b, lw, g, bonus, ln_g, ln_b, batch, seq):
    c = SCAN_CHUNK
    n_chunks = seq // c
    pw = PRIMARY_WIDTH
    assert GROUP_HEADS * c == GROUP_WIDTH, "block-diagonal packing needs a 64-token chunk"
    blk = pl.BlockSpec((batch, c, pw), lambda ci: (0, ci, 0))
    const = pl.BlockSpec((1, pw), lambda ci: (0, 0))
    per_batch = lambda a: a.reshape(batch, seq, pw)
    out = pl.pallas_call(
        _rwkv_scan_kernel,
        grid=(n_chunks,),
        in_specs=[blk] * 8 + [const, const],
        out_specs=blk,
        out_shape=jax.ShapeDtypeStruct((batch, seq, pw), BF16),
        scratch_shapes=[pltpu.VMEM((batch * KV_GROUPS, GROUP_WIDTH, GROUP_WIDTH), F32)],
        compiler_params=_params("arbitrary"),
        name="rwkv_scan",
    )(*(per_batch(a) for a in (r, k, v, na, b, lw, g, bonus)), ln_g.reshape(1, pw), ln_b.reshape(1, pw))
    return out.reshape(batch * seq, pw)


def _rwkv_layer(h, pre_g, w_in, mu, w0, w2, a0, a2, g2, k_k, k_a, r_k, ln_g, ln_b, batch, seq):
    proj = _norm_proj(h, pre_g, w_in[:, :RWKV_SHIFT_W], tn=RWKV_SHIFT_W // 2, out_dtype=F32, name="rwkv_proj")
    q_mem = _norm_proj(h, pre_g, w_in[:, RWKV_SHIFT_W:], tn=MEM_WIDTH, out_dtype=BF16, name="rwkv_proj_mem")
    r, k, v, na, b, lw, g, bonus = _rwkv_prep(proj, mu, w0, w2, a0, a2, g2, k_k, k_a, r_k.reshape(-1), seq)
    prim = _rwkv_scan(r, k, v, na, b, lw, g, bonus, ln_g, ln_b, batch, seq)
    return prim, q_mem


def kernel(x, mem, positions, mem_norm_g, w_mem_kv, pre_mix_g, post_mix_g, pre_ffn_g, post_ffn_g, w_out, w_ffn_in, w_ffn_out, nsa_w_in, nsa_gate_b, nsa_cmp_pos_k, nsa_cmp_pos_v, nsa_cmp_k_w1, nsa_cmp_k_w2, nsa_cmp_v_w1, nsa_cmp_v_w2, rwkv_w_in, rwkv_mu, rwkv_w0, rwkv_w2, rwkv_a0, rwkv_a2, rwkv_g2, rwkv_k_k, rwkv_k_a, rwkv_r_k, rwkv_ln_g, rwkv_ln_b):
    batch, seq, d = x.shape
    n_mem = mem.shape[1]
    depth = pre_mix_g.shape[0]
    mkv = _norm_proj(mem.reshape(batch * n_mem, d), mem_norm_g, w_mem_kv, tn=w_mem_kv.shape[1], out_dtype=BF16,
                     name="mem_kv")
    mem_k = mkv[:, :MEM_WIDTH].reshape(batch, n_mem, MEM_WIDTH)
    mem_v = mkv[:, MEM_WIDTH:].reshape(batch, n_mem, MEM_WIDTH)
    h = x.reshape(batch * seq, d)
    for i in range(depth):
        j = i // 2
        if i % 2 == 0:
            prim, q_mem = _nsa_layer(h, pre_mix_g[i], nsa_w_in[j], nsa_gate_b[j], nsa_cmp_pos_k[j], nsa_cmp_pos_v[j],
                                     nsa_cmp_k_w1[j], nsa_cmp_k_w2[j], nsa_cmp_v_w1[j], nsa_cmp_v_w2[j],
                                     positions, batch, seq)
        else:
            prim, q_mem = _rwkv_layer(h, pre_mix_g[i], rwkv_w_in[j], rwkv_mu[j], rwkv_w0[j], rwkv_w2[j], rwkv_a0[j],
                                      rwkv_a2[j], rwkv_g2[j], rwkv_k_k[j], rwkv_k_a[j], rwkv_r_k[j], rwkv_ln_g[j],
                                      rwkv_ln_b[j], batch, seq)
        mo = _mem_attn(q_mem, mem_k, mem_v, seq)
        h = _out_proj(prim, mo, w_out[i], post_mix_g[i], h)
        h = _ffn(h, pre_ffn_g[i], w_ffn_in[i], w_ffn_out[i], post_ffn_g[i])
    return h.reshape(batch, seq, d)
```

```python
import functools

import numpy as np
import jax
import jax.numpy as jnp
from jax import lax
from jax.experimental import pallas as pl
from jax.experimental.pallas import tpu as pltpu

F32 = jnp.float32
BF16 = jnp.bfloat16

HEAD_DIM = 64
ROT_DIM = HEAD_DIM // 4
ROT_HALF = ROT_DIM // 2
ROPE_THETA = 500000.0
MEM_HEADS = 4
MEM_WIDTH = MEM_HEADS * HEAD_DIM
KV_GROUPS = 3
GROUP_HEADS = 4
GROUP_WIDTH = GROUP_HEADS * HEAD_DIM
PRIMARY_WIDTH = KV_GROUPS * GROUP_WIDTH
KV_WIDTH = KV_GROUPS * HEAD_DIM
CMP_BLOCK = 32
CMP_STRIDE = 16
SEL_BLOCK = 64
SEL_TOPK = 16
WINDOW = 512
DECAY_LORA = 64
AAA_LORA = 64
GATE_LORA = 128
RWKV_SHIFT_W = 3 * PRIMARY_WIDTH + DECAY_LORA + AAA_LORA + GATE_LORA
RWKV_GN_EPS = HEAD_DIM * 1e-5
NORM_EPS = 1e-6
NEG = -1e30
FORCE = 1e6
ATTN_SCALE = HEAD_DIM ** -0.5
LOG2E = float(np.log2(np.e))
SAFE_LOG2 = 80.0

LANES = 128
SUBLANES = 8
VMEM_LIMIT_BYTES = 48 * 1024 * 1024

ROW_TILE = 512
ATTN_TILE = 256
SCAN_CHUNK = 64
INV_BASE = 8
SEL_SPAN = 2
WIN_TILES = WINDOW // ATTN_TILE + 1
N_FORCED = 3
ACC_ROWS = HEAD_DIM + 16
PREP_TILE = 256

_NT = (((1,), (1,)), ((), ()))
_TN = (((0,), (0,)), ((), ()))


def _params(*sem):
    return pltpu.CompilerParams(dimension_semantics=sem, vmem_limit_bytes=VMEM_LIMIT_BYTES)


def _rms(x, g):
    return x * lax.rsqrt(jnp.mean(x * x, axis=-1, keepdims=True) + NORM_EPS) * g


def _dot(a, b):
    return jnp.dot(a, b, preferred_element_type=F32)


def _dot_nt(a, b):
    return lax.dot_general(a, b, _NT, preferred_element_type=F32)


def _proj_kernel(x_ref, g_ref, w_ref, o_ref, xn_ref):
    @pl.when(pl.program_id(1) == 0)
    def _():
        xn_ref[...] = _rms(x_ref[...], g_ref[...]).astype(BF16)

    o_ref[...] = _dot(xn_ref[...], w_ref[...]).astype(o_ref.dtype)


def _proj_rope_kernel(x_ref, g_ref, w_ref, c_ref, sp_ref, sm_ref, scale_ref, o_ref, xn_ref):
    @pl.when(pl.program_id(1) == 0)
    def _():
        xn_ref[...] = _rms(x_ref[...], g_ref[...]).astype(BF16)

    y = _dot(xn_ref[...], w_ref[...])
    width = y.shape[1]
    reps = width // LANES
    cos = jnp.concatenate([c_ref[...]] * reps, axis=1)
    sin_hi = jnp.concatenate([sp_ref[...]] * reps, axis=1)
    sin_lo = jnp.concatenate([sm_ref[...]] * reps, axis=1)
    y = y * cos + pltpu.roll(y, ROT_HALF, 1) * sin_hi + pltpu.roll(y, width - ROT_HALF, 1) * sin_lo
    o_ref[...] = (y * scale_ref[...]).astype(o_ref.dtype)


def _proj_t_kernel(x_ref, g_ref, wt_ref, b_ref, o_ref, xn_ref, *, gate):
    @pl.when(pl.program_id(1) == 0)
    def _():
        xn_ref[...] = _rms(x_ref[...], g_ref[...]).astype(BF16)

    y = _dot_nt(wt_ref[...], xn_ref[...])
    if gate:
        y = jax.nn.sigmoid(y + b_ref[...])
    o_ref[0] = y.astype(o_ref.dtype)


def _norm_proj(x, g, w, *, tn, out_dtype, rope=None, name):
    t, d = x.shape
    n = w.shape[1]
    tm = min(ROW_TILE, t)
    in_specs = [
        pl.BlockSpec((tm, d), lambda i, j: (i, 0)),
        pl.BlockSpec((1, d), lambda i, j: (0, 0)),
        pl.BlockSpec((d, tn), lambda i, j: (0, j)),
    ]
    args = [x, g.reshape(1, d), w.astype(BF16)]
    if rope is None:
        body = _proj_kernel
    else:
        body = _proj_rope_kernel
        cos, sin_hi, sin_lo, scale = rope
        in_specs += [pl.BlockSpec((tm, LANES), lambda i, j: (i, 0))] * 3
        in_specs += [pl.BlockSpec((1, tn), lambda i, j: (0, j))]
        args += [cos, sin_hi, sin_lo, scale]
    return pl.pallas_call(
        body,
        grid=(t // tm, n // tn),
        in_specs=in_specs,
        out_specs=pl.BlockSpec((tm, tn), lambda i, j: (i, j)),
        out_shape=jax.ShapeDtypeStruct((t, n), out_dtype),
        scratch_shapes=[pltpu.VMEM((tm, d), BF16)],
        compiler_params=_params("parallel", "arbitrary"),
        name=name,
    )(*args)


def _norm_proj_t(x, g, wt, bias, *, tm, tn, out_dtype, gate, name):
    t, d = x.shape
    n = wt.shape[0]
    return pl.pallas_call(
        functools.partial(_proj_t_kernel, gate=gate),
        grid=(t // tm, n // tn),
        in_specs=[
            pl.BlockSpec((tm, d), lambda i, j: (i, 0)),
            pl.BlockSpec((1, d), lambda i, j: (0, 0)),
            pl.BlockSpec((tn, d), lambda i, j: (j, 0)),
            pl.BlockSpec((tn, 1), lambda i, j: (j, 0)),
        ],
        out_specs=pl.BlockSpec((1, tn, tm), lambda i, j: (i, j, 0)),
        out_shape=jax.ShapeDtypeStruct((t // tm, n, tm), out_dtype),
        scratch_shapes=[pltpu.VMEM((tm, d), BF16)],
        compiler_params=_params("parallel", "arbitrary"),
        name=name,
    )(x, g.reshape(1, d), wt.astype(BF16), bias)


def _outproj_kernel(a_ref, b_ref, wa_ref, wb_ref, g_ref, h_ref, o_ref):
    y = _dot(a_ref[...], wa_ref[...]) + _dot(b_ref[...], wb_ref[...])
    o_ref[...] = h_ref[...] + _rms(y, g_ref[...])


def _out_proj(prim, mo, w_out, g, h):
    t, d = h.shape
    tm = min(ROW_TILE, t)
    pw = prim.shape[1]
    return pl.pallas_call(
        _outproj_kernel,
        grid=(t // tm,),
        in_specs=[
            pl.BlockSpec((tm, pw), lambda i: (i, 0)),
            pl.BlockSpec((tm, MEM_WIDTH), lambda i: (i, 0)),
            pl.BlockSpec((pw, d), lambda i: (0, 0)),
            pl.BlockSpec((MEM_WIDTH, d), lambda i: (0, 0)),
            pl.BlockSpec((1, d), lambda i: (0, 0)),
            pl.BlockSpec((tm, d), lambda i: (i, 0)),
        ],
        out_specs=pl.BlockSpec((tm, d), lambda i: (i, 0)),
        out_shape=jax.ShapeDtypeStruct((t, d), F32),
        compiler_params=_params("parallel"),
        name="out_proj",
    )(prim, mo, w_out[:pw].astype(BF16), w_out[pw:].astype(BF16), g.reshape(1, d), h)


def _ffn_kernel(h_ref, g1_ref, wg_ref, wu_ref, wo_ref, g2_ref, o_ref, hn_ref, acc_ref):
    j = pl.program_id(1)

    @pl.when(j == 0)
    def _():
        hn_ref[...] = _rms(h_ref[...], g1_ref[...]).astype(BF16)
        acc_ref[...] = jnp.zeros_like(acc_ref)

    hn = hn_ref[...]
    gate = _dot(hn, wg_ref[...])
    up = _dot(hn, wu_ref[...])
    act = (jax.nn.silu(gate) * up).astype(BF16)
    acc_ref[...] += _dot(act, wo_ref[...])

    @pl.when(j == pl.num_programs(1) - 1)
    def _():
        o_ref[...] = h_ref[...] + _rms(acc_ref[...], g2_ref[...])


def _ffn_chunk(hidden):
    units = hidden // LANES
    for parts in range(2, units + 1):
        if units % parts == 0:
            return (units // parts) * LANES
    return hidden


def _ffn(h, g1, w_in, w_out, g2):
    t, d = h.shape
    hidden = w_out.shape[0]
    th = _ffn_chunk(hidden)
    nh = hidden // th
    tm = min(ROW_TILE, t)
    w_in = w_in.astype(BF16)
    return pl.pallas_call(
        _ffn_kernel,
        grid=(t // tm, nh),
        in_specs=[
            pl.BlockSpec((tm, d), lambda i, j: (i, 0)),
            pl.BlockSpec((1, d), lambda i, j: (0, 0)),
            pl.BlockSpec((d, th), lambda i, j: (0, j)),
            pl.BlockSpec((d, th), lambda i, j: (0, j + nh)),
            pl.BlockSpec((th, d), lambda i, j: (j, 0)),
            pl.BlockSpec((1, d), lambda i, j: (0, 0)),
        ],
        out_specs=pl.BlockSpec((tm, d), lambda i, j: (i, 0)),
        out_shape=jax.ShapeDtypeStruct((t, d), F32),
        scratch_shapes=[pltpu.VMEM((tm, d), BF16), pltpu.VMEM((tm, d), F32)],
        compiler_params=_params("parallel", "arbitrary"),
        name="ffn",
    )(h, g1.reshape(1, d), w_in, w_in, w_out.astype(BF16), g2.reshape(1, d))


def _mem_attn_kernel(q_ref, mk_ref, mv_ref, o_ref):
    q = q_ref[...]
    mk = mk_ref[0]
    mv = mv_ref[0]
    head_of_lane = lax.broadcasted_iota(jnp.int32, mk.shape, 1) // HEAD_DIM
    acc = jnp.zeros(q.shape, F32)
    for h in range(MEM_HEADS):
        s = _dot_nt(q, jnp.where(head_of_lane == h, mk, 0)) * ATTN_SCALE
        e = jnp.exp(s - jnp.max(s, axis=-1, keepdims=True))
        p = e / jnp.sum(e, axis=-1, keepdims=True)
        acc = acc + _dot(p.astype(BF16), jnp.where(head_of_lane == h, mv, 0))
    o_ref[...] = acc.astype(o_ref.dtype)


def _mem_attn(q_mem, mem_k, mem_v, seq):
    t = q_mem.shape[0]
    m = mem_k.shape[1]
    tm = min(ROW_TILE, seq)
    per_seq = seq // tm
    return pl.pallas_call(
        _mem_attn_kernel,
        grid=(t // tm,),
        in_specs=[
            pl.BlockSpec((tm, MEM_WIDTH), lambda i: (i, 0)),
            pl.BlockSpec((1, m, MEM_WIDTH), lambda i: (i // per_seq, 0, 0)),
            pl.BlockSpec((1, m, MEM_WIDTH), lambda i: (i // per_seq, 0, 0)),
        ],
        out_specs=pl.BlockSpec((tm, MEM_WIDTH), lambda i: (i, 0)),
        out_shape=jax.ShapeDtypeStruct((t, MEM_WIDTH), BF16),
        compiler_params=_params("parallel"),
        name="mem_attn",
    )(q_mem, mem_k, mem_v)


def _compress_kernel(xk_ref, xv_ref, pk_ref, pv_ref, k1a_ref, k1b_ref, k2_ref, v1a_ref, v1b_ref, v2t_ref,
                     c_ref, s_ref, kc_ref, vct_ref):
    def hidden(x_ref, pos_ref, wa_ref, wb_ref):
        x = x_ref[0, 0]
        n = x.shape[0]
        first = _dot((x + pos_ref[0:1, :]).astype(BF16), wa_ref[...])
        second = _dot((x + pos_ref[1:2, :]).astype(BF16), wb_ref[...])
        return jax.nn.gelu(first + pltpu.roll(second, n - 1, 0)).astype(BF16)

    hk = hidden(xk_ref, pk_ref, k1a_ref, k1b_ref)
    both = _dot(hk, k2_ref[...])
    kc_ref[0, 0] = (both[:, :LANES] * c_ref[0] + both[:, LANES:] * s_ref[0]).astype(kc_ref.dtype)
    hv = hidden(xv_ref, pv_ref, v1a_ref, v1b_ref)
    vct_ref[0, 0] = _dot_nt(v2t_ref[...], hv).astype(vct_ref.dtype)


def _rope_partner_cols(w):
    d = np.arange(w.shape[1]) % HEAD_DIM
    src = np.where(d < ROT_HALF, np.arange(w.shape[1]) + ROT_HALF, np.arange(w.shape[1]) - ROT_HALF)
    src = np.clip(src, 0, w.shape[1] - 1)
    sign = np.where(d < ROT_HALF, -1.0, np.where(d < ROT_DIM, 1.0, 0.0)).astype(np.float32)
    return w[:, src] * sign


def _compress(raw_k, raw_v, pos_k, pos_v, k_w1, k_w2, v_w1, v_w2, cos_c, sin_c, batch, seq):
    n_chunk = seq // CMP_STRIDE
    feat = CMP_STRIDE * HEAD_DIM
    hid = k_w1.shape[1]

    def chunks(raw):
        x = raw.reshape(batch, n_chunk, CMP_STRIDE, KV_GROUPS, HEAD_DIM)
        return x.transpose(0, 3, 1, 2, 4).reshape(batch, KV_GROUPS, n_chunk, feat)

    zeros = jnp.zeros((hid, LANES - HEAD_DIM), F32)
    k2 = jnp.concatenate([k_w2, zeros, _rope_partner_cols(k_w2), zeros], axis=1).astype(BF16)
    x_spec = pl.BlockSpec((1, 1, n_chunk, feat), lambda b, g: (b, g, 0, 0))
    pos_spec = pl.BlockSpec((2, feat), lambda b, g: (0, 0))
    w1_spec = pl.BlockSpec((feat, hid), lambda b, g: (0, 0))
    tab_spec = pl.BlockSpec((1, n_chunk, LANES), lambda b, g: (b, 0, 0))
    return pl.pallas_call(
        _compress_kernel,
        grid=(batch, KV_GROUPS),
        in_specs=[x_spec, x_spec, pos_spec, pos_spec, w1_spec, w1_spec,
                  pl.BlockSpec((hid, 2 * LANES), lambda b, g: (0, 0)),
                  w1_spec, w1_spec,
                  pl.BlockSpec((HEAD_DIM, hid), lambda b, g: (0, 0)),
                  tab_spec, tab_spec],
        out_specs=[pl.BlockSpec((1, 1, n_chunk, LANES), lambda b, g: (b, g, 0, 0)),
                   pl.BlockSpec((1, 1, HEAD_DIM, n_chunk), lambda b, g: (b, g, 0, 0))],
        out_shape=[jax.ShapeDtypeStruct((batch, KV_GROUPS, n_chunk, LANES), BF16),
                   jax.ShapeDtypeStruct((batch, KV_GROUPS, HEAD_DIM, n_chunk), BF16)],
        compiler_params=_params("parallel", "parallel"),
        name="nsa_compress",
    )(chunks(raw_k), chunks(raw_v), pos_k.reshape(2, feat), pos_v.reshape(2, feat),
      k_w1[:feat].astype(BF16), k_w1[feat:].astype(BF16), k2,
      v_w1[:feat].astype(BF16), v_w1[feat:].astype(BF16), v_w2.T.astype(BF16), cos_c, sin_c)


def _split_dot_left(w_bf16, x):
    hi = x.astype(BF16)
    rem = x - hi.astype(F32)
    mid = rem.astype(BF16)
    lo = (rem - mid.astype(F32)).astype(BF16)
    return _dot(w_bf16, hi) + _dot(w_bf16, mid) + _dot(w_bf16, lo)


def _nsa_attn_kernel(q_ref, kk_ref, hot_ref, vt_ref, kc_ref, vct_ref, tap_ref, winb_ref, gate_ref, o_ref,
                     qx_ref, m_ref, acc_ref, kmax_ref):
    tq = q_ref.shape[0]
    qt = pl.program_id(2)
    t0 = qt * tq
    heads = range(GROUP_HEADS)

    q2 = q_ref[...]
    qs = jnp.concatenate([q2[:, h * LANES:(h + 1) * LANES] for h in heads], axis=0)
    lane = lax.broadcasted_iota(jnp.int32, qs.shape, 1)
    q_sel = jnp.where(lane < HEAD_DIM, qs, 0)
    q_win = jnp.where(lane >= HEAD_DIM, qs, 0)

    col_t = t0 + lax.broadcasted_iota(jnp.int32, (1, tq), 1)

    @pl.when(qt == 0)
    def _():
        half = lax.broadcasted_iota(jnp.int32, (1, LANES), 1) < HEAD_DIM
        k_abs = jnp.abs(kk_ref[...].astype(F32))
        kc_abs = jnp.abs(kc_ref[0, 0].astype(F32))
        row_sums = [jnp.sum(jnp.where(half, k_abs, 0.0), axis=1, keepdims=True),
                    jnp.sum(jnp.where(half, 0.0, k_abs), axis=1, keepdims=True),
                    jnp.sum(kc_abs, axis=1, keepdims=True)]
        kmax_ref[0] = functools.reduce(jnp.maximum, [jnp.max(s) for s in row_sums])
    bounded = jnp.max(jnp.abs(qs.astype(F32))) * kmax_ref[0] <= SAFE_LOG2

    @pl.when(bounded)
    def _():
        _nsa_attend(False, q_sel, q_win, col_t, qt, kk_ref, hot_ref, vt_ref, kc_ref, vct_ref, tap_ref, winb_ref,
                    gate_ref, o_ref, qx_ref, m_ref, acc_ref)

    @pl.when(jnp.logical_not(bounded))
    def _():
        _nsa_attend(True, q_sel, q_win, col_t, qt, kk_ref, hot_ref, vt_ref, kc_ref, vct_ref, tap_ref, winb_ref,
                    gate_ref, o_ref, qx_ref, m_ref, acc_ref)


def _nsa_attend(stabilized, q_sel, q_win, col_t, qt, kk_ref, hot_ref, vt_ref, kc_ref, vct_ref, tap_ref, winb_ref,
                gate_ref, o_ref, qx_ref, m_ref, acc_ref):
    tq = col_t.shape[1]
    tk = tq
    heads = range(GROUP_HEADS)
    head_rows = lambda h: slice(h * tq, (h + 1) * tq)

    def weights(scores):
        if not stabilized:
            return jnp.exp2(scores), None
        top = jnp.max(scores, axis=0, keepdims=True)
        return jnp.exp2(scores - top), top > 0.5 * NEG

    n_cmp = kc_ref.shape[2]
    blk_end = lax.broadcasted_iota(jnp.int32, (n_cmp, 1), 0) * CMP_STRIDE + (CMP_BLOCK - 1)
    valid = blk_end <= col_t
    cmp_scores = [_dot_nt(kc_ref[0, 0], q_sel[head_rows(h)]) for h in heads]
    o_cmp = []
    p_sum = jnp.zeros((n_cmp, tq), F32)
    for h in heads:
        e, live = weights(jnp.where(valid, cmp_scores[h], NEG))
        total = jnp.sum(e, axis=0, keepdims=True)
        live = total > 0.0 if live is None else live
        p = e * jnp.where(live, 1.0 / total, 0.0)
        o_cmp.append(_dot(vct_ref[0, 0], p.astype(BF16)))
        p_sum = p_sum + p
    imp = _split_dot_left(tap_ref[...], p_sum)

    n_sel = imp.shape[0]
    blk = lax.broadcasted_iota(jnp.int32, (n_sel, tq), 0)
    blk_f = blk.astype(F32)
    cur = col_t // SEL_BLOCK
    forced = (blk == 0) | (blk == cur) | (blk == cur - 1)
    work = jnp.where(forced, -jnp.inf, jnp.where(blk <= cur, imp, -FORCE))
    chosen = jnp.where(forced, 1.0, 0.0)
    for _ in range(max(min(SEL_TOPK, n_sel) - N_FORCED, 0)):
        best = jnp.max(work, axis=0, keepdims=True)
        hit = blk_f == jnp.min(jnp.where(work == best, blk_f, float(n_sel)), axis=0, keepdims=True)
        work = jnp.where(hit, -jnp.inf, work)
        chosen = jnp.where(hit, 1.0, chosen)
    if n_sel < LANES:
        chosen = jnp.concatenate([chosen, jnp.zeros((LANES - n_sel, tq), F32)], axis=0)
    block_bias = ((chosen - 1.0) * (-NEG)).T.astype(BF16)
    qx_ref[:, 0:LANES] = q_sel
    qx_ref[:, LANES:2 * LANES] = jnp.concatenate([block_bias] * GROUP_HEADS, axis=0)

    def key_pos(kt, n):
        return kt * tk + lax.broadcasted_iota(jnp.int32, (n * tk, 1), 0)

    def key_rows(ref, kt, n):
        return ref[pl.ds(pl.multiple_of(kt * tk, tk), n * tk), :]

    def value_rows(kt, n, first_row):
        vals = jnp.concatenate([vt_ref[kt + i, first_row:first_row + HEAD_DIM, :] for i in range(n)], axis=1)
        return jnp.concatenate([vals, jnp.ones((ACC_ROWS - HEAD_DIM, n * tk), BF16)], axis=0)

    def normalized(acc):
        return acc[0:HEAD_DIM] / acc[HEAD_DIM:HEAD_DIM + 1]

    win_start = jnp.maximum(qt - (WIN_TILES - 1), 0)
    win_bias = winb_ref[qt - win_start]
    win_keys = key_rows(kk_ref, win_start, WIN_TILES)
    win_vals = value_rows(win_start, WIN_TILES, HEAD_DIM)
    win_scores = [_dot_nt(win_keys, q_win[head_rows(h)]) for h in heads]
    o_win = [normalized(_dot(win_vals, weights(win_scores[h] + win_bias)[0].astype(BF16))) for h in heads]

    acc_ref[...] = jnp.zeros(acc_ref.shape, F32)
    full_spans = qt // SEL_SPAN

    def score_span(i):
        kt = i * SEL_SPAN
        keys = jnp.concatenate([key_rows(kk_ref, kt, SEL_SPAN), key_rows(hot_ref, kt, SEL_SPAN)], axis=1)
        return tuple(_dot_nt(keys, qx_ref[head_rows(h), :]) for h in heads)

    def causal_scores(i, scores):
        return jnp.where(key_pos(i * SEL_SPAN, SEL_SPAN) <= col_t, scores, NEG)

    if not stabilized:
        def plain_span(i, causal):
            v_ext = value_rows(i * SEL_SPAN, SEL_SPAN, 0)
            scores = score_span(i)
            for h in heads:
                e = jnp.exp2(causal_scores(i, scores[h]) if causal else scores[h]).astype(BF16)
                acc_ref[:, head_rows(h)] += _dot(v_ext, e)

        def plain_step(i, carry):
            plain_span(i, False)
            return carry

        lax.fori_loop(0, full_spans, plain_step, 0)
        plain_span(full_spans, True)
    else:
        m_ref[...] = jnp.full(m_ref.shape, NEG, F32)

        def flash_span(i, span_scores, causal):
            v_ext = value_rows(i * SEL_SPAN, SEL_SPAN, 0)
            for h in heads:
                cols = head_rows(h)
                scores = causal_scores(i, span_scores[h]) if causal else span_scores[h]
                m_old = m_ref[:, cols]
                m_new = jnp.maximum(m_old, jnp.max(scores, axis=0, keepdims=True))
                e = jnp.exp2(scores - m_new).astype(BF16)
                acc_ref[:, cols] = jnp.exp2(m_old - m_new) * acc_ref[:, cols] + _dot(v_ext, e)
                m_ref[:, cols] = m_new

        def sel_step(i, span_scores):
            following = score_span(i + 1)
            flash_span(i, span_scores, False)
            return following

        last_scores = lax.fori_loop(0, full_spans, sel_step, score_span(0))
        flash_span(full_spans, last_scores, True)

    gates = gate_ref[...]
    outs = []
    for h in heads:
        outs.append(gates[3 * h:3 * h + 1, :] * o_cmp[h]
                    + gates[3 * h + 1:3 * h + 2, :] * normalized(acc_ref[:, head_rows(h)])
                    + gates[3 * h + 2:3 * h + 3, :] * o_win[h])
    o_ref[...] = jnp.concatenate(outs, axis=0).T.astype(o_ref.dtype)


def _tap_matrix(n_sel, n_chunk):
    ratio = SEL_BLOCK // CMP_STRIDE
    tap = np.zeros((n_sel, n_chunk), np.float32)
    n_cmp = n_chunk - (CMP_BLOCK // CMP_STRIDE - 1)
    for j in range(n_sel):
        for n in range(n_cmp):
            lo = max(n * CMP_STRIDE, j * SEL_BLOCK)
            hi = min(n * CMP_STRIDE + CMP_BLOCK, (j + 1) * SEL_BLOCK)
            if hi > lo:
                tap[j, n] = (hi - lo) / CMP_STRIDE
    return jnp.asarray(tap, BF16)


def _nsa_attention(qk, vt, gates, kc, vct, batch, seq):
    tq = ATTN_TILE
    n_tiles = seq // tq
    n_chunk = kc.shape[2]
    n_sel = seq // SEL_BLOCK
    q_blocks = GROUP_HEADS * LANES
    k_col0 = KV_GROUPS * q_blocks // LANES
    vt5 = vt.reshape(batch, n_tiles, KV_GROUPS, 2 * HEAD_DIM, tq)
    gates5 = gates.reshape(batch, n_tiles, KV_GROUPS, 16, tq)
    width = GROUP_HEADS * tq
    assert n_sel <= LANES, "block one-hot is one lane tile wide"
    assert n_tiles % SEL_SPAN == 0 and n_tiles >= WIN_TILES
    hot = jnp.asarray((np.arange(seq)[:, None] // SEL_BLOCK) == np.arange(LANES)[None, :], BF16)
    dist = (np.arange(WIN_TILES)[:, None, None] * tq + np.arange(tq)[None, None, :]
            - np.arange(WIN_TILES * tq)[None, :, None])
    win_bias = jnp.asarray(np.where((dist >= 0) & (dist < WINDOW), 0.0, NEG), F32)
    return pl.pallas_call(
        _nsa_attn_kernel,
        grid=(batch, KV_GROUPS, n_tiles),
        in_specs=[
            pl.BlockSpec((tq, q_blocks), lambda b, g, i: (b * n_tiles + i, g)),
            pl.BlockSpec((seq, LANES), lambda b, g, i: (b, k_col0 + g)),
            pl.BlockSpec((seq, LANES), lambda b, g, i: (0, 0)),
            pl.BlockSpec((None, n_tiles, None, 2 * HEAD_DIM, tq), lambda b, g, i: (b, 0, g, 0, 0)),
            pl.BlockSpec((1, 1, n_chunk, LANES), lambda b, g, i: (b, g, 0, 0)),
            pl.BlockSpec((1, 1, HEAD_DIM, n_chunk), lambda b, g, i: (b, g, 0, 0)),
            pl.BlockSpec((n_sel, n_chunk), lambda b, g, i: (0, 0)),
            pl.BlockSpec((WIN_TILES, WIN_TILES * tq, tq), lambda b, g, i: (0, 0, 0)),
            pl.BlockSpec((None, None, None, 16, tq), lambda b, g, i: (b, i, g, 0, 0)),
        ],
        out_specs=pl.BlockSpec((tq, GROUP_WIDTH), lambda b, g, i: (b * n_tiles + i, g)),
        out_shape=jax.ShapeDtypeStruct((batch * seq, PRIMARY_WIDTH), BF16),
        scratch_shapes=[
            pltpu.VMEM((width, 2 * LANES), BF16),
            pltpu.VMEM((1, width), F32),
            pltpu.VMEM((ACC_ROWS, width), F32),
            pltpu.SMEM((1,), F32),
        ],
        compiler_params=_params("parallel", "parallel", "arbitrary"),
        name="nsa_attention",
    )(qk, qk, hot, vt5, kc, vct, _tap_matrix(n_sel, n_chunk), win_bias, gates5)


def _rope_tables(positions):
    inv = ROPE_THETA ** (-jnp.arange(0, ROT_DIM, 2, dtype=F32) / ROT_DIM)
    ang = positions.astype(F32)[..., None] * inv
    cos, sin = lax.optimization_barrier((jnp.cos(ang), jnp.sin(ang)))
    pad = jnp.zeros(ang.shape[:-1] + (HEAD_DIM - ROT_DIM,), F32)
    zero = jnp.zeros_like(sin)
    cos_h = jnp.concatenate([cos, cos, pad + 1.0], axis=-1)
    sin_hi = jnp.concatenate([zero, sin, pad], axis=-1)
    sin_lo = jnp.concatenate([-sin, zero, pad], axis=-1)
    both = lambda a: jnp.concatenate([a, a], axis=-1)
    return both(cos_h), both(sin_hi), both(sin_lo), both(jnp.concatenate([sin, sin, pad], axis=-1))


def _nsa_layer(h, pre_g, w_in, gate_b, pos_k, pos_v, k_w1, k_w2, v_w1, v_w2, positions, batch, seq):
    t, d = h.shape
    pw, kw = PRIMARY_WIDTH, KV_WIDTH
    offs = np.cumsum([0, pw] + [kw] * 6 + [3 * KV_GROUPS * GROUP_HEADS, MEM_WIDTH])
    col = lambda i: w_in[:, offs[i]:offs[i + 1]]
    w_q, w_kc, w_vc, w_ks, w_vs, w_kw, w_vw, w_gl, w_qm = [col(i) for i in range(9)]

    wq_h = w_q.reshape(d, KV_GROUPS * GROUP_HEADS, 1, HEAD_DIM)
    wq_dup = jnp.broadcast_to(wq_h, (d, KV_GROUPS * GROUP_HEADS, 2, HEAD_DIM)).reshape(d, 2 * pw)
    wk_pair = jnp.stack([w_ks.reshape(d, KV_GROUPS, HEAD_DIM), w_kw.reshape(d, KV_GROUPS, HEAD_DIM)], axis=2)
    w_rope = jnp.concatenate([wq_dup, wk_pair.reshape(d, 2 * kw)], axis=1)
    scale = jnp.concatenate([jnp.full((1, 2 * pw), ATTN_SCALE * LOG2E, F32), jnp.ones((1, 2 * kw), F32)], axis=1)
    cos, sin_hi, sin_lo, _ = _rope_tables(positions.reshape(t))
    qk = _norm_proj(h, pre_g, w_rope, tn=w_rope.shape[1] // 3, out_dtype=BF16,
                    rope=(cos, sin_hi, sin_lo, scale), name="nsa_proj_rope")

    wv_pair = jnp.stack([w_vs.reshape(d, KV_GROUPS, HEAD_DIM), w_vw.reshape(d, KV_GROUPS, HEAD_DIM)], axis=2)
    wv_t = wv_pair.reshape(d, 2 * kw).T
    vt = _norm_proj_t(h, pre_g, wv_t, jnp.zeros((2 * kw, 1), F32), tm=ATTN_TILE, tn=2 * kw,
                      out_dtype=BF16, gate=False, name="nsa_proj_values")

    per_group = 3 * GROUP_HEADS
    wg = jnp.pad(w_gl.reshape(d, KV_GROUPS, per_group), ((0, 0), (0, 0), (0, 16 - per_group)))
    bg = jnp.pad(gate_b.reshape(KV_GROUPS, per_group), ((0, 0), (0, 16 - per_group)))
    n_gate = LANES
    wg_t = jnp.pad(wg.reshape(d, KV_GROUPS * 16).T, ((0, n_gate - KV_GROUPS * 16), (0, 0)))
    bg_t = jnp.pad(bg.reshape(KV_GROUPS * 16, 1), ((0, n_gate - KV_GROUPS * 16), (0, 0)))
    gates = _norm_proj_t(h, pre_g, wg_t, bg_t, tm=ATTN_TILE, tn=n_gate, out_dtype=F32, gate=True,
                         name="nsa_proj_gates")[:, :KV_GROUPS * 16, :]

    w_plain = jnp.concatenate([w_qm, w_kc, w_vc], axis=1)
    plain = _norm_proj(h, pre_g, w_plain, tn=w_plain.shape[1], out_dtype=F32, name="nsa_proj_plain")
    q_mem = plain[:, :MEM_WIDTH].astype(BF16)
    raw_k = plain[:, MEM_WIDTH:MEM_WIDTH + kw]
    raw_v = plain[:, MEM_WIDTH + kw:]

    n_chunk = seq // CMP_STRIDE
    cmp_end = jnp.minimum(jnp.arange(n_chunk) * CMP_STRIDE + CMP_BLOCK - 1, seq - 1)
    cos_c, _, _, sin_c = _rope_tables(positions[:, cmp_end])
    kc, vct = _compress(raw_k, raw_v, pos_k, pos_v, k_w1, k_w2, v_w1, v_w2, cos_c, sin_c, batch, seq)
    prim = _nsa_attention(qk, vt, gates, kc, vct, batch, seq)
    return prim, q_mem


def _split_dot(x, w_bf16):
    hi = x.astype(BF16)
    lo = (x - hi.astype(F32)).astype(BF16)
    return _dot(hi, w_bf16) + _dot(lo, w_bf16)


def _head_ones(width):
    head = np.arange(width) // HEAD_DIM
    return jnp.asarray(head[:, None] == head[None, :], BF16)


def _rwkv_prep_kernel(p_ref, prev_ref, mu_ref, w0_ref, w2_ref, a0_ref, a2_ref, g2_ref, kk_ref, ka_ref, rk_ref,
                      ones_ref, r_ref, k_ref, v_ref, na_ref, b_ref, lw_ref, g_ref, bonus_ref, *, tiles_per_seq):
    pw = PRIMARY_WIDTH
    x = p_ref[...]
    first_tile = (pl.program_id(0) % tiles_per_seq) == 0
    last_prev = jnp.where(first_tile, 0.0, prev_ref[SUBLANES - 1:SUBLANES, :])
    row = lax.broadcasted_iota(jnp.int32, x.shape, 0)
    prev = jnp.where(row == 0, last_prev, pltpu.roll(x, 1, 0))
    xs = x + (prev - x) * mu_ref[...]
    r = xs[:, 0:pw]
    k = xs[:, pw:2 * pw]
    v = xs[:, 2 * pw:3 * pw]
    lora = xs[:, 3 * pw:3 * pw + DECAY_LORA + AAA_LORA]
    gl = xs[:, 3 * pw + DECAY_LORA + AAA_LORA:]
    w_pre = w0_ref[...] + _dot(jnp.tanh(lora).astype(BF16), w2_ref[...])
    z = -w_pre
    w = -(jnp.maximum(z, 0.0) + jnp.log1p(jnp.exp(-jnp.abs(z)))) - 0.5
    lw_ref[...] = -jnp.exp(w)
    a = jax.nn.sigmoid(a0_ref[...] + _dot(lora.astype(BF16), a2_ref[...]))
    g_ref[...] = _dot(jax.nn.sigmoid(gl).astype(BF16), g2_ref[...]).astype(g_ref.dtype)
    ones = ones_ref[...]
    kk = k * kk_ref[...]
    kk = kk * lax.rsqrt(jnp.maximum(_split_dot(kk * kk, ones), 1e-24))
    k2 = k * (1.0 + (a - 1.0) * ka_ref[...])
    r_ref[...] = r
    k_ref[...] = k2
    v_ref[...] = v.astype(v_ref.dtype)
    na_ref[...] = -kk
    b_ref[...] = kk * a
    bonus_ref[...] = (_split_dot(r * k2 * rk_ref[...], ones) * v).astype(bonus_ref.dtype)


def _rwkv_prep(proj, mu, w0, w2, a0, a2, g2, k_k, k_a, r_k, seq):
    t, width = proj.shape
    pw = PRIMARY_WIDTH
    tm = min(PREP_TILE, seq)
    row = lambda a: a.reshape(1, -1)
    lora_w = DECAY_LORA + AAA_LORA
    w2e = jnp.concatenate([w2, jnp.zeros((AAA_LORA, pw), F32)], axis=0).astype(BF16)
    a2e = jnp.concatenate([jnp.zeros((DECAY_LORA, pw), F32), a2], axis=0).astype(BF16)
    const = lambda shape: pl.BlockSpec(shape, lambda i: (0, 0))
    out_spec = pl.BlockSpec((tm, pw), lambda i: (i, 0))
    return pl.pallas_call(
        functools.partial(_rwkv_prep_kernel, tiles_per_seq=seq // tm),
        grid=(t // tm,),
        in_specs=[
            pl.BlockSpec((tm, width), lambda i: (i, 0)),
            pl.BlockSpec((SUBLANES, width), lambda i: (jnp.maximum(i * (tm // SUBLANES) - 1, 0), 0)),
            const((1, width)), const((1, pw)), const((lora_w, pw)), const((1, pw)), const((lora_w, pw)),
            const((GATE_LORA, pw)), const((1, pw)), const((1, pw)), const((1, pw)), const((pw, pw)),
        ],
        out_specs=[out_spec] * 8,
        out_shape=[jax.ShapeDtypeStruct((t, pw), dt) for dt in (F32, F32, BF16, F32, F32, F32, BF16, BF16)],
        compiler_params=_params("parallel"),
        name="rwkv_prep",
    )(proj, proj, row(mu), row(w0), w2e, row(a0), a2e, g2.astype(BF16), row(k_k), row(k_a), row(r_k),
      _head_ones(pw))


def _block_diag(x, mask):
    return jnp.where(mask, jnp.concatenate([x.astype(BF16)] * GROUP_HEADS, axis=0), jnp.zeros((), BF16))


def _rwkv_scan_kernel(r_ref, k_ref, v_ref, na_ref, b_ref, lw_ref, g_ref, bonus_ref, lng_ref, lnb_ref, o_ref, state_ref):
    n_batch, c, _ = r_ref.shape
    gw = GROUP_WIDTH
    units = [(bi, grp) for bi in range(n_batch) for grp in range(KV_GROUPS)]
    every = lambda fn, *lists: [fn(*args) for args in zip(*lists)] if lists else [fn(u) for u in units]

    @pl.when(pl.program_id(0) == 0)
    def _():
        state_ref[...] = jnp.zeros_like(state_ref)

    rr = lax.broadcasted_iota(jnp.int32, (gw, gw), 0)
    cc = lax.broadcasted_iota(jnp.int32, (gw, gw), 1)
    bd_mask = (rr // HEAD_DIM) == (cc // HEAD_DIM)
    t_idx = lax.broadcasted_iota(jnp.int32, (c, gw), 0)
    s_idx = lax.broadcasted_iota(jnp.int32, (c, gw), 1) % HEAD_DIM
    strict = t_idx > s_idx
    incl = t_idx >= s_idx
    eye = jnp.where(t_idx == s_idx, 1.0, 0.0)
    same_block = {}
    size = INV_BASE
    while size <= c:
        same_block[size] = (t_idx // size) == (s_idx // size)
        size *= 2
    tril =jnp.where(lax.broadcasted_iota(jnp.int32, (c, c), 0) >= lax.broadcasted_iota(jnp.int32, (c, c), 1),
                     1.0, 0.0).astype(BF16)
    ones_bd = jnp.where(bd_mask, 1.0, 0.0).astype(BF16)

    def cumsum_rows(x):
        hi = x.astype(BF16)
        rem = x - hi.astype(F32)
        mid = rem.astype(BF16)
        lo = (rem - mid.astype(F32)).astype(BF16)
        return _dot(tril, hi) + _dot(tril, mid) + _dot(tril, lo)

    bd = lambda x: _block_diag(x, bd_mask)
    cols = lambda grp: slice(grp * gw, (grp + 1) * gw)
    load = lambda ref: [ref[bi, :, cols(grp)] for bi, grp in units]
    r, k, v, na, bv, lw = (load(ref) for ref in (r_ref, k_ref, v_ref, na_ref, b_ref, lw_ref))
    state = [state_ref[bi * KV_GROUPS + grp] for bi, grp in units]

    cum = every(cumsum_rows, lw)
    p_incl = every(jnp.exp, cum)
    inv_p = every(lambda x: jnp.exp(-x), cum)
    b_t = every(lambda x, s: (x * s).astype(BF16), bv, inv_p)
    k_t = every(lambda x, s: (x * s).astype(BF16), k, inv_p)
    x2 = every(lambda a, cu, l, rr_, p: jnp.concatenate([a * jnp.exp(cu - l), rr_ * p], axis=0).astype(BF16),
               na, cum, lw, r, p_incl)

    g_b = every(lambda x, y: _dot_nt(x, bd(y)), x2, b_t)
    g_k = every(lambda x, y: _dot_nt(x, bd(y)), x2, k_t)
    xh = every(lambda x, s: _dot_nt(x, s.astype(BF16)), x2, state)
    l_ab = every(lambda g: jnp.where(strict, g[:c], 0.0), g_b)
    m_rb = every(lambda g: jnp.where(incl, g[c:], 0.0).astype(BF16), g_b)
    l_ak = every(lambda g: jnp.where(strict, g[:c], 0.0).astype(BF16), g_k)
    m_rk = every(lambda g: jnp.where(incl, g[c:], 0.0).astype(BF16), g_k)

    power = every(lambda l: jnp.where(same_block[INV_BASE], l, 0.0), l_ab)
    t_inv = every(lambda p: eye + p, power)
    for _ in range(int(np.log2(INV_BASE)) - 1):
        power = every(lambda p: _dot(p.astype(BF16), bd(p)), power)
        t_inv = every(lambda t, p: t + _dot(t.astype(BF16), bd(p)), t_inv, power)
    size = 2 * INV_BASE
    while size <= c:
        off_mask = same_block[size] & ~same_block[size // 2]
        half = every(lambda t, l: _dot(t.astype(BF16), bd(jnp.where(off_mask, l, 0.0))), t_inv, l_ab)
        t_inv = every(lambda t, hf: t + _dot(hf.astype(BF16), bd(t)), t_inv, half)
        size *= 2

    v_bd = every(bd, v)
    rhs = every(lambda x, l, vb: x[:c] + _dot(l, vb), xh, l_ak, v_bd)
    u = every(lambda t, rh: _dot(t.astype(BF16), bd(rh)), t_inv, rhs)
    out = every(lambda x, mb, uu, mk, vb: x[c:] + _dot(mb, bd(uu)) + _dot(mk, vb), xh, m_rb, u, m_rk, v_bd)
    delta = every(lambda uu, vv, b, kk_: lax.dot_general(jnp.concatenate([uu.astype(BF16), vv.astype(BF16)], axis=0),
                                                         jnp.concatenate([b, kk_], axis=0), _TN,
                                                         preferred_element_type=F32), u, v, b_t, k_t)
    mean = every(lambda o: _split_dot(o, ones_bd) * (1.0 / HEAD_DIM), out)
    dev = every(lambda o, m: o - m, out, mean)
    var = every(lambda dv: _split_dot(dv * dv, ones_bd) * (1.0 / HEAD_DIM), dev)
    for i, (bi, grp) in enumerate(units):
        state_ref[bi * KV_GROUPS + grp] = (state[i] + jnp.where(bd_mask, delta[i], 0.0)) * p_incl[i][c - 1:c, :]
        y = dev[i] * lax.rsqrt(var[i] + RWKV_GN_EPS) * lng_ref[:, cols(grp)] + lnb_ref[:, cols(grp)]
        o_ref[bi, :, cols(grp)] = ((y + bonus_ref[bi, :, cols(grp)]) * g_ref[bi, :, cols(grp)]).astype(o_ref.dtype)


def _rwkv_scan(r, k, v, na, b, lw, g, bonus, ln_g, ln_b, batch, seq):
    c = SCAN_CHUNK
    n_chunks = seq // c
    pw = PRIMARY_WIDTH
    assert GROUP_HEADS * c == GROUP_WIDTH, "block-diagonal packing needs a 64-token chunk"
    blk = pl.BlockSpec((batch, c, pw), lambda ci: (0, ci, 0))
    const = pl.BlockSpec((1, pw), lambda ci: (0, 0))
    per_batch = lambda a: a.reshape(batch, seq, pw)
    out = pl.pallas_call(
        _rwkv_scan_kernel,
        grid=(n_chunks,),
        in_specs=[blk] * 8 + [const, const],
        out_specs=blk,
        out_shape=jax.ShapeDtypeStruct((batch, seq, pw), BF16),
        scratch_shapes=[pltpu.VMEM((batch * KV_GROUPS, GROUP_WIDTH, GROUP_WIDTH), F32)],
        compiler_params=_params("arbitrary"),
        name="rwkv_scan",
    )(*(per_batch(a) for a in (r, k, v, na, b, lw, g, bonus)), ln_g.reshape(1, pw), ln_b.reshape(1, pw))
    return out.reshape(batch * seq, pw)


def _rwkv_layer(h, pre_g, w_in, mu, w0, w2, a0, a2, g2, k_k, k_a, r_k, ln_g, ln_b, batch, seq):
    proj = _norm_proj(h, pre_g, w_in[:, :RWKV_SHIFT_W], tn=RWKV_SHIFT_W // 2, out_dtype=F32, name="rwkv_proj")
    q_mem = _norm_proj(h, pre_g, w_in[:, RWKV_SHIFT_W:], tn=MEM_WIDTH, out_dtype=BF16, name="rwkv_proj_mem")
    r, k, v, na, b, lw, g, bonus = _rwkv_prep(proj, mu, w0, w2, a0, a2, g2, k_k, k_a, r_k.reshape(-1), seq)
    prim = _rwkv_scan(r, k, v, na, b, lw, g, bonus, ln_g, ln_b, batch, seq)
    return prim, q_mem


def kernel(x, mem, positions, mem_norm_g, w_mem_kv, pre_mix_g, post_mix_g, pre_ffn_g, post_ffn_g, w_out, w_ffn_in, w_ffn_out, nsa_w_in, nsa_gate_b, nsa_cmp_pos_k, nsa_cmp_pos_v, nsa_cmp_k_w1, nsa_cmp_k_w2, nsa_cmp_v_w1, nsa_cmp_v_w2, rwkv_w_in, rwkv_mu, rwkv_w0, rwkv_w2, rwkv_a0, rwkv_a2, rwkv_g2, rwkv_k_k, rwkv_k_a, rwkv_r_k, rwkv_ln_g, rwkv_ln_b):
    batch, seq, d = x.shape
    n_mem = mem.shape[1]
    depth = pre_mix_g.shape[0]
    mkv = _norm_proj(mem.reshape(batch * n_mem, d), mem_norm_g, w_mem_kv, tn=w_mem_kv.shape[1], out_dtype=BF16,
                     name="mem_kv")
    mem_k = mkv[:, :MEM_WIDTH].reshape(batch, n_mem, MEM_WIDTH)
    mem_v = mkv[:, MEM_WIDTH:].reshape(batch, n_mem, MEM_WIDTH)
    h = x.reshape(batch * seq, d)
    for i in range(depth):
        j = i // 2
        if i % 2 == 0:
            prim, q_mem = _nsa_layer(h, pre_mix_g[i], nsa_w_in[j], nsa_gate_b[j], nsa_cmp_pos_k[j], nsa_cmp_pos_v[j],
                                     nsa_cmp_k_w1[j], nsa_cmp_k_w2[j], nsa_cmp_v_w1[j], nsa_cmp_v_w2[j],
                                     positions, batch, seq)
        else:
            prim, q_mem = _rwkv_layer(h, pre_mix_g[i], rwkv_w_in[j], rwkv_mu[j], rwkv_w0[j], rwkv_w2[j], rwkv_a0[j],
                                      rwkv_a2[j], rwkv_g2[j], rwkv_k_k[j], rwkv_k_a[j], rwkv_r_k[j], rwkv_ln_g[j],
                                      rwkv_ln_b[j], batch, seq)
        mo = _mem_attn(q_mem, mem_k, mem_v, seq)
        h = _out_proj(prim, mo, w_out[i], post_mix_g[i], h)
        h = _ffn(h, pre_ffn_g[i], w_ffn_in[i], w_ffn_out[i], post_ffn_g[i])
    return h.reshape(batch, seq, d)
```

```python
import functools

import numpy as np
import jax
import jax.numpy as jnp
from jax import lax
from jax.experimental import pallas as pl
from jax.experimental.pallas import tpu as pltpu

F32 = jnp.float32
BF16 = jnp.bfloat16

HEAD_DIM = 64
ROT_DIM = HEAD_DIM // 4
ROT_HALF = ROT_DIM // 2
ROPE_THETA = 500000.0
MEM_HEADS = 4
MEM_WIDTH = MEM_HEADS * HEAD_DIM
KV_GROUPS = 3
GROUP_HEADS = 4
GROUP_WIDTH = GROUP_HEADS * HEAD_DIM
PRIMARY_WIDTH = KV_GROUPS * GROUP_WIDTH
KV_WIDTH = KV_GROUPS * HEAD_DIM
CMP_BLOCK = 32
CMP_STRIDE = 16
SEL_BLOCK = 64
SEL_TOPK = 16
WINDOW = 512
DECAY_LORA = 64
AAA_LORA = 64
GATE_LORA = 128
RWKV_SHIFT_W = 3 * PRIMARY_WIDTH + DECAY_LORA + AAA_LORA + GATE_LORA
RWKV_GN_EPS = HEAD_DIM * 1e-5
NORM_EPS = 1e-6
NEG = -1e30
FORCE = 1e6
ATTN_SCALE = HEAD_DIM ** -0.5
LOG2E = float(np.log2(np.e))
SAFE_LOG2 = 80.0

LANES = 128
SUBLANES = 8
VMEM_LIMIT_BYTES = 48 * 1024 * 1024

ROW_TILE = 512
ATTN_TILE = 256
SCAN_CHUNK = 64
INV_BASE = 8
SEL_SPAN = 2
WIN_TILES = WINDOW // ATTN_TILE + 1
N_FORCED = 3
ACC_ROWS = HEAD_DIM + 16
PREP_TILE = 256

_NT = (((1,), (1,)), ((), ()))
_TN = (((0,), (0,)), ((), ()))


def _params(*sem):
    return pltpu.CompilerParams(dimension_semantics=sem, vmem_limit_bytes=VMEM_LIMIT_BYTES)


def _rms(x, g):
    return x * lax.rsqrt(jnp.mean(x * x, axis=-1, keepdims=True) + NORM_EPS) * g


def _dot(a, b):
    return jnp.dot(a, b, preferred_element_type=F32)


def _dot_nt(a, b):
    return lax.dot_general(a, b, _NT, preferred_element_type=F32)


def _norm_cast_kernel(x_ref, g_ref, o_ref):
    o_ref[...] = _rms(x_ref[...], g_ref[...]).astype(o_ref.dtype)


def _norm_cast(x, g):
    t, d = x.shape
    tm = min(ROW_TILE, t)
    return pl.pallas_call(
        _norm_cast_kernel,
        grid=(t // tm,),
        in_specs=[pl.BlockSpec((tm, d), lambda i: (i, 0)), pl.BlockSpec((1, d), lambda i: (0, 0))],
        out_specs=pl.BlockSpec((tm, d), lambda i: (i, 0)),
        out_shape=jax.ShapeDtypeStruct((t, d), BF16),
        compiler_params=_params("parallel"),
        name="norm_cast",
    )(x, g.reshape(1, d))


def _proj_kernel(x_ref, w_ref, o_ref):
    o_ref[...] = _dot(x_ref[...], w_ref[...]).astype(o_ref.dtype)


def _proj_rope_kernel(x_ref, w_ref, c_ref, sp_ref, sm_ref, scale_ref, o_ref):
    y = _dot(x_ref[...], w_ref[...])
    width = y.shape[1]
    reps = width // LANES
    cos = jnp.concatenate([c_ref[...]] * reps, axis=1)
    sin_hi = jnp.concatenate([sp_ref[...]] * reps, axis=1)
    sin_lo = jnp.concatenate([sm_ref[...]] * reps, axis=1)
    y = y * cos + pltpu.roll(y, ROT_HALF, 1) * sin_hi + pltpu.roll(y, width - ROT_HALF, 1) * sin_lo
    o_ref[...] = (y * scale_ref[...]).astype(o_ref.dtype)


def _proj_t_kernel(x_ref, wt_ref, b_ref, o_ref, *, gate):
    y = _dot_nt(wt_ref[...], x_ref[...])
    if gate:
        y = jax.nn.sigmoid(y + b_ref[...])
    o_ref[0] = y.astype(o_ref.dtype)


def _proj(x, w, *, tn, out_dtype, rope=None, name):
    t, d = x.shape
    n = w.shape[1]
    tm = min(ROW_TILE, t)
    in_specs = [
        pl.BlockSpec((tm, d), lambda i, j: (i, 0)),
        pl.BlockSpec((d, tn), lambda i, j: (0, j)),
    ]
    args = [x, w.astype(BF16)]
    if rope is None:
        body = _proj_kernel
    else:
        body = _proj_rope_kernel
        cos, sin_hi, sin_lo, scale = rope
        in_specs += [pl.BlockSpec((tm, LANES), lambda i, j: (i, 0))] * 3
        in_specs += [pl.BlockSpec((1, tn), lambda i, j: (0, j))]
        args += [cos, sin_hi, sin_lo, scale]
    return pl.pallas_call(
        body,
        grid=(t // tm, n // tn),
        in_specs=in_specs,
        out_specs=pl.BlockSpec((tm, tn), lambda i, j: (i, j)),
        out_shape=jax.ShapeDtypeStruct((t, n), out_dtype),
        compiler_params=_params("parallel", "parallel"),
        name=name,
    )(*args)


def _proj_t(x, wt, bias, *, tm, tn, out_dtype, gate, name):
    t, d = x.shape
    n = wt.shape[0]
    return pl.pallas_call(
        functools.partial(_proj_t_kernel, gate=gate),
        grid=(t // tm, n // tn),
        in_specs=[
            pl.BlockSpec((tm, d), lambda i, j: (i, 0)),
            pl.BlockSpec((tn, d), lambda i, j: (j, 0)),
            pl.BlockSpec((tn, 1), lambda i, j: (j, 0)),
        ],
        out_specs=pl.BlockSpec((1, tn, tm), lambda i, j: (i, j, 0)),
        out_shape=jax.ShapeDtypeStruct((t // tm, n, tm), out_dtype),
        compiler_params=_params("parallel", "parallel"),
        name=name,
    )(x, wt.astype(BF16), bias)


def _outproj_kernel(a_ref, b_ref, wa_ref, wb_ref, g_ref, h_ref, o_ref):
    y = _dot(a_ref[...], wa_ref[...]) + _dot(b_ref[...], wb_ref[...])
    o_ref[...] = h_ref[...] + _rms(y, g_ref[...])


def _out_proj(prim, mo, w_out, g, h):
    t, d = h.shape
    tm = min(ROW_TILE, t)
    pw = prim.shape[1]
    return pl.pallas_call(
        _outproj_kernel,
        grid=(t // tm,),
        in_specs=[
            pl.BlockSpec((tm, pw), lambda i: (i, 0)),
            pl.BlockSpec((tm, MEM_WIDTH), lambda i: (i, 0)),
            pl.BlockSpec((pw, d), lambda i: (0, 0)),
            pl.BlockSpec((MEM_WIDTH, d), lambda i: (0, 0)),
            pl.BlockSpec((1, d), lambda i: (0, 0)),
            pl.BlockSpec((tm, d), lambda i: (i, 0)),
        ],
        out_specs=pl.BlockSpec((tm, d), lambda i: (i, 0)),
        out_shape=jax.ShapeDtypeStruct((t, d), F32),
        compiler_params=_params("parallel"),
        name="out_proj",
    )(prim, mo, w_out[:pw].astype(BF16), w_out[pw:].astype(BF16), g.reshape(1, d), h)


def _ffn_kernel(h_ref, g1_ref, wg_ref, wu_ref, wo_ref, g2_ref, *rest, feeds_next):
    if feeds_next:
        g3_ref, o_ref, next_ref, hn_ref, acc_ref = rest
    else:
        o_ref, hn_ref, acc_ref = rest
    j = pl.program_id(1)

    @pl.when(j == 0)
    def _():
        hn_ref[...] = _rms(h_ref[...], g1_ref[...]).astype(BF16)
        acc_ref[...] = jnp.zeros_like(acc_ref)

    hn = hn_ref[...]
    gate = _dot(hn, wg_ref[...])
    up = _dot(hn, wu_ref[...])
    act = (jax.nn.silu(gate) * up).astype(BF16)
    acc_ref[...] += _dot(act, wo_ref[...])

    @pl.when(j == pl.num_programs(1) - 1)
    def _():
        out = h_ref[...] + _rms(acc_ref[...], g2_ref[...])
        o_ref[...] = out
        if feeds_next:
            next_ref[...] = _rms(out, g3_ref[...]).astype(next_ref.dtype)


def _ffn_chunk(hidden):
    units = hidden // LANES
    for parts in range(2, units + 1):
        if units % parts == 0:
            return (units // parts) * LANES
    return hidden


def _ffn(h, g1, w_in, w_out, g2, next_g=None):
    t, d = h.shape
    hidden = w_out.shape[0]
    th = _ffn_chunk(hidden)
    nh = hidden // th
    tm = min(ROW_TILE, t)
    w_in = w_in.astype(BF16)
    row = pl.BlockSpec((tm, d), lambda i, j: (i, 0))
    gain = pl.BlockSpec((1, d), lambda i, j: (0, 0))
    feeds_next = next_g is not None
    in_specs = [row, gain,
                pl.BlockSpec((d, th), lambda i, j: (0, j)),
                pl.BlockSpec((d, th), lambda i, j: (0, j + nh)),
                pl.BlockSpec((th, d), lambda i, j: (j, 0)),
                gain]
    args = [h, g1.reshape(1, d), w_in, w_in, w_out.astype(BF16), g2.reshape(1, d)]
    out_specs, out_shape = row, jax.ShapeDtypeStruct((t, d), F32)
    if feeds_next:
        in_specs.append(gain)
        args.append(next_g.reshape(1, d))
        out_specs, out_shape = [row, row], [out_shape, jax.ShapeDtypeStruct((t, d), BF16)]
    return pl.pallas_call(
        functools.partial(_ffn_kernel, feeds_next=feeds_next),
        grid=(t // tm, nh),
        in_specs=in_specs,
        out_specs=out_specs,
        out_shape=out_shape,
        scratch_shapes=[pltpu.VMEM((tm, d), BF16), pltpu.VMEM((tm, d), F32)],
        compiler_params=_params("parallel", "arbitrary"),
        name="ffn",
    )(*args)


def _mem_attn_kernel(q_ref, mk_ref, mv_ref, o_ref):
    q = q_ref[...]
    mk = mk_ref[0]
    mv = mv_ref[0]
    head_of_lane = lax.broadcasted_iota(jnp.int32, mk.shape, 1) // HEAD_DIM
    acc = jnp.zeros(q.shape, F32)
    for h in range(MEM_HEADS):
        s = _dot_nt(q, jnp.where(head_of_lane == h, mk, 0)) * ATTN_SCALE
        e = jnp.exp(s - jnp.max(s, axis=-1, keepdims=True))
        p = e / jnp.sum(e, axis=-1, keepdims=True)
        acc = acc + _dot(p.astype(BF16), jnp.where(head_of_lane == h, mv, 0))
    o_ref[...] = acc.astype(o_ref.dtype)


def _mem_attn(q_mem, mem_k, mem_v, seq):
    t = q_mem.shape[0]
    m = mem_k.shape[1]
    tm = min(ROW_TILE, seq)
    per_seq = seq // tm
    return pl.pallas_call(
        _mem_attn_kernel,
        grid=(t // tm,),
        in_specs=[
            pl.BlockSpec((tm, MEM_WIDTH), lambda i: (i, 0)),
            pl.BlockSpec((1, m, MEM_WIDTH), lambda i: (i // per_seq, 0, 0)),
            pl.BlockSpec((1, m, MEM_WIDTH), lambda i: (i // per_seq, 0, 0)),
        ],
        out_specs=pl.BlockSpec((tm, MEM_WIDTH), lambda i: (i, 0)),
        out_shape=jax.ShapeDtypeStruct((t, MEM_WIDTH), BF16),
        compiler_params=_params("parallel"),
        name="mem_attn",
    )(q_mem, mem_k, mem_v)


def _compress_kernel(xk_ref, xv_ref, pk_ref, pv_ref, k1a_ref, k1b_ref, k2_ref, v1a_ref, v1b_ref, v2t_ref,
                     c_ref, s_ref, kc_ref, vct_ref):
    def hidden(x_ref, pos_ref, wa_ref, wb_ref):
        x = x_ref[0, 0]
        n = x.shape[0]
        first = _dot((x + pos_ref[0:1, :]).astype(BF16), wa_ref[...])
        second = _dot((x + pos_ref[1:2, :]).astype(BF16), wb_ref[...])
        return jax.nn.gelu(first + pltpu.roll(second, n - 1, 0)).astype(BF16)

    hk = hidden(xk_ref, pk_ref, k1a_ref, k1b_ref)
    both = _dot(hk, k2_ref[...])
    kc_ref[0, 0] = (both[:, :LANES] * c_ref[0] + both[:, LANES:] * s_ref[0]).astype(kc_ref.dtype)
    hv = hidden(xv_ref, pv_ref, v1a_ref, v1b_ref)
    vct_ref[0, 0] = _dot_nt(v2t_ref[...], hv).astype(vct_ref.dtype)


def _rope_partner_cols(w):
    d = np.arange(w.shape[1]) % HEAD_DIM
    src = np.where(d < ROT_HALF, np.arange(w.shape[1]) + ROT_HALF, np.arange(w.shape[1]) - ROT_HALF)
    src = np.clip(src, 0, w.shape[1] - 1)
    sign = np.where(d < ROT_HALF, -1.0, np.where(d < ROT_DIM, 1.0, 0.0)).astype(np.float32)
    return w[:, src] * sign


def _compress(raw_k, raw_v, pos_k, pos_v, k_w1, k_w2, v_w1, v_w2, cos_c, sin_c, batch, seq):
    n_chunk = seq // CMP_STRIDE
    feat = CMP_STRIDE * HEAD_DIM
    hid = k_w1.shape[1]

    def chunks(raw):
        x = raw.reshape(batch, n_chunk, CMP_STRIDE, KV_GROUPS, HEAD_DIM)
        return x.transpose(0, 3, 1, 2, 4).reshape(batch, KV_GROUPS, n_chunk, feat)

    zeros = jnp.zeros((hid, LANES - HEAD_DIM), F32)
    k2 = jnp.concatenate([k_w2, zeros, _rope_partner_cols(k_w2), zeros], axis=1).astype(BF16)
    x_spec = pl.BlockSpec((1, 1, n_chunk, feat), lambda b, g: (b, g, 0, 0))
    pos_spec = pl.BlockSpec((2, feat), lambda b, g: (0, 0))
    w1_spec = pl.BlockSpec((feat, hid), lambda b, g: (0, 0))
    tab_spec = pl.BlockSpec((1, n_chunk, LANES), lambda b, g: (b, 0, 0))
    return pl.pallas_call(
        _compress_kernel,
        grid=(batch, KV_GROUPS),
        in_specs=[x_spec, x_spec, pos_spec, pos_spec, w1_spec, w1_spec,
                  pl.BlockSpec((hid, 2 * LANES), lambda b, g: (0, 0)),
                  w1_spec, w1_spec,
                  pl.BlockSpec((HEAD_DIM, hid), lambda b, g: (0, 0)),
                  tab_spec, tab_spec],
        out_specs=[pl.BlockSpec((1, 1, n_chunk, LANES), lambda b, g: (b, g, 0, 0)),
                   pl.BlockSpec((1, 1, HEAD_DIM, n_chunk), lambda b, g: (b, g, 0, 0))],
        out_shape=[jax.ShapeDtypeStruct((batch, KV_GROUPS, n_chunk, LANES), BF16),
                   jax.ShapeDtypeStruct((batch, KV_GROUPS, HEAD_DIM, n_chunk), BF16)],
        compiler_params=_params("parallel", "parallel"),
        name="nsa_compress",
    )(chunks(raw_k), chunks(raw_v), pos_k.reshape(2, feat), pos_v.reshape(2, feat),
      k_w1[:feat].astype(BF16), k_w1[feat:].astype(BF16), k2,
      v_w1[:feat].astype(BF16), v_w1[feat:].astype(BF16), v_w2.T.astype(BF16), cos_c, sin_c)


def _split_dot_left(w_bf16, x):
    hi = x.astype(BF16)
    rem = x - hi.astype(F32)
    mid = rem.astype(BF16)
    lo = (rem - mid.astype(F32)).astype(BF16)
    return _dot(w_bf16, hi) + _dot(w_bf16, mid) + _dot(w_bf16, lo)


def _nsa_attn_kernel(q_ref, kk_ref, hot_ref, vt_ref, kc_ref, vct_ref, tap_ref, winb_ref, gate_ref, o_ref,
                     qx_ref, m_ref, acc_ref, kmax_ref):
    tq = q_ref.shape[0]
    qt = pl.program_id(2)
    t0 = qt * tq
    heads = range(GROUP_HEADS)

    q2 = q_ref[...]
    qs = jnp.concatenate([q2[:, h * LANES:(h + 1) * LANES] for h in heads], axis=0)
    lane = lax.broadcasted_iota(jnp.int32, qs.shape, 1)
    q_sel = jnp.where(lane < HEAD_DIM, qs, 0)
    q_win = jnp.where(lane >= HEAD_DIM, qs, 0)

    col_t = t0 + lax.broadcasted_iota(jnp.int32, (1, tq), 1)

    @pl.when(qt == 0)
    def _():
        half = lax.broadcasted_iota(jnp.int32, (1, LANES), 1) < HEAD_DIM
        k_abs = jnp.abs(kk_ref[...].astype(F32))
        kc_abs = jnp.abs(kc_ref[0, 0].astype(F32))
        row_sums = [jnp.sum(jnp.where(half, k_abs, 0.0), axis=1, keepdims=True),
                    jnp.sum(jnp.where(half, 0.0, k_abs), axis=1, keepdims=True),
                    jnp.sum(kc_abs, axis=1, keepdims=True)]
        kmax_ref[0] = functools.reduce(jnp.maximum, [jnp.max(s) for s in row_sums])
    bounded = jnp.max(jnp.abs(qs.astype(F32))) * kmax_ref[0] <= SAFE_LOG2

    @pl.when(bounded)
    def _():
        _nsa_attend(False, q_sel, q_win, col_t, qt, kk_ref, hot_ref, vt_ref, kc_ref, vct_ref, tap_ref, winb_ref,
                    gate_ref, o_ref, qx_ref, m_ref, acc_ref)

    @pl.when(jnp.logical_not(bounded))
    def _():
        _nsa_attend(True, q_sel, q_win, col_t, qt, kk_ref, hot_ref, vt_ref, kc_ref, vct_ref, tap_ref, winb_ref,
                    gate_ref, o_ref, qx_ref, m_ref, acc_ref)


def _nsa_attend(stabilized, q_sel, q_win, col_t, qt, kk_ref, hot_ref, vt_ref, kc_ref, vct_ref, tap_ref, winb_ref,
                gate_ref, o_ref, qx_ref, m_ref, acc_ref):
    tq = col_t.shape[1]
    tk = tq
    heads = range(GROUP_HEADS)
    head_rows = lambda h: slice(h * tq, (h + 1) * tq)

    def weights(scores):
        if not stabilized:
            return jnp.exp2(scores), None
        top = jnp.max(scores, axis=0, keepdims=True)
        return jnp.exp2(scores - top), top > 0.5 * NEG

    n_cmp = kc_ref.shape[2]
    blk_end = lax.broadcasted_iota(jnp.int32, (n_cmp, 1), 0) * CMP_STRIDE + (CMP_BLOCK - 1)
    valid = blk_end <= col_t
    cmp_scores = [_dot_nt(kc_ref[0, 0], q_sel[head_rows(h)]) for h in heads]
    o_cmp = []
    p_sum = jnp.zeros((n_cmp, tq), F32)
    for h in heads:
        e, live = weights(jnp.where(valid, cmp_scores[h], NEG))
        total = jnp.sum(e, axis=0, keepdims=True)
        live = total > 0.0 if live is None else live
        p = e * jnp.where(live, 1.0 / total, 0.0)
        o_cmp.append(_dot(vct_ref[0, 0], p.astype(BF16)))
        p_sum = p_sum + p
    imp = _split_dot_left(tap_ref[...], p_sum)

    n_sel = imp.shape[0]
    blk = lax.broadcasted_iota(jnp.int32, (n_sel, tq), 0)
    blk_f = blk.astype(F32)
    cur = col_t // SEL_BLOCK
    forced = (blk == 0) | (blk == cur) | (blk == cur - 1)
    work = jnp.where(forced, -jnp.inf, jnp.where(blk <= cur, imp, -FORCE))
    chosen = jnp.where(forced, 1.0, 0.0)
    for _ in range(max(min(SEL_TOPK, n_sel) - N_FORCED, 0)):
        best = jnp.max(work, axis=0, keepdims=True)
        hit = blk_f == jnp.min(jnp.where(work == best, blk_f, float(n_sel)), axis=0, keepdims=True)
        work = jnp.where(hit, -jnp.inf, work)
        chosen = jnp.where(hit, 1.0, chosen)
    if n_sel < LANES:
        chosen = jnp.concatenate([chosen, jnp.zeros((LANES - n_sel, tq), F32)], axis=0)
    block_bias = ((chosen - 1.0) * (-NEG)).T.astype(BF16)
    qx_ref[:, 0:LANES] = q_sel
    qx_ref[:, LANES:2 * LANES] = jnp.concatenate([block_bias] * GROUP_HEADS, axis=0)

    def key_pos(kt, n):
        return kt * tk + lax.broadcasted_iota(jnp.int32, (n * tk, 1), 0)

    def key_rows(ref, kt, n):
        return ref[pl.ds(pl.multiple_of(kt * tk, tk), n * tk), :]

    def value_rows(kt, n, first_row):
        vals = jnp.concatenate([vt_ref[kt + i, first_row:first_row + HEAD_DIM, :] for i in range(n)], axis=1)
        return jnp.concatenate([vals, jnp.ones((ACC_ROWS - HEAD_DIM, n * tk), BF16)], axis=0)

    def normalized(acc):
        return acc[0:HEAD_DIM] / acc[HEAD_DIM:HEAD_DIM + 1]

    win_start = jnp.maximum(qt - (WIN_TILES - 1), 0)
    win_bias = winb_ref[qt - win_start]
    win_keys = key_rows(kk_ref, win_start, WIN_TILES)
    win_vals = value_rows(win_start, WIN_TILES, HEAD_DIM)
    win_scores = [_dot_nt(win_keys, q_win[head_rows(h)]) for h in heads]
    o_win = [normalized(_dot(win_vals, weights(win_scores[h] + win_bias)[0].astype(BF16))) for h in heads]

    acc_ref[...] = jnp.zeros(acc_ref.shape, F32)
    full_spans = qt // SEL_SPAN

    def score_span(i):
        kt = i * SEL_SPAN
        keys = jnp.concatenate([key_rows(kk_ref, kt, SEL_SPAN), key_rows(hot_ref, kt, SEL_SPAN)], axis=1)
        return tuple(_dot_nt(keys, qx_ref[head_rows(h), :]) for h in heads)

    def causal_scores(i, scores):
        return jnp.where(key_pos(i * SEL_SPAN, SEL_SPAN) <= col_t, scores, NEG)

    if not stabilized:
        def plain_span(i, causal):
            v_ext = value_rows(i * SEL_SPAN, SEL_SPAN, 0)
            scores = score_span(i)
            for h in heads:
                e = jnp.exp2(causal_scores(i, scores[h]) if causal else scores[h]).astype(BF16)
                acc_ref[:, head_rows(h)] += _dot(v_ext, e)

        def plain_step(i, carry):
            plain_span(i, False)
            return carry

        lax.fori_loop(0, full_spans, plain_step, 0)
        plain_span(full_spans, True)
    else:
        m_ref[...] = jnp.full(m_ref.shape, NEG, F32)

        def flash_span(i, span_scores, causal):
            v_ext = value_rows(i * SEL_SPAN, SEL_SPAN, 0)
            for h in heads:
                cols = head_rows(h)
                scores = causal_scores(i, span_scores[h]) if causal else span_scores[h]
                m_old = m_ref[:, cols]
                m_new = jnp.maximum(m_old, jnp.max(scores, axis=0, keepdims=True))
                e = jnp.exp2(scores - m_new).astype(BF16)
                acc_ref[:, cols] = jnp.exp2(m_old - m_new) * acc_ref[:, cols] + _dot(v_ext, e)
                m_ref[:, cols] = m_new

        def sel_step(i, span_scores):
            following = score_span(i + 1)
            flash_span(i, span_scores, False)
            return following

        last_scores = lax.fori_loop(0, full_spans, sel_step, score_span(0))
        flash_span(full_spans, last_scores, True)

    gates = gate_ref[...]
    outs = []
    for h in heads:
        outs.append(gates[3 * h:3 * h + 1, :] * o_cmp[h]
                    + gates[3 * h + 1:3 * h + 2, :] * normalized(acc_ref[:, head_rows(h)])
                    + gates[3 * h + 2:3 * h + 3, :] * o_win[h])
    o_ref[...] = jnp.concatenate(outs, axis=0).T.astype(o_ref.dtype)


def _tap_matrix(n_sel, n_chunk):
    ratio = SEL_BLOCK // CMP_STRIDE
    tap = np.zeros((n_sel, n_chunk), np.float32)
    n_cmp = n_chunk - (CMP_BLOCK // CMP_STRIDE - 1)
    for j in range(n_sel):
        for n in range(n_cmp):
            lo = max(n * CMP_STRIDE, j * SEL_BLOCK)
            hi = min(n * CMP_STRIDE + CMP_BLOCK, (j + 1) * SEL_BLOCK)
            if hi > lo:
                tap[j, n] = (hi - lo) / CMP_STRIDE
    return jnp.asarray(tap, BF16)


def _nsa_attention(qk, vt, gates, kc, vct, batch, seq):
    tq = ATTN_TILE
    n_tiles = seq // tq
    n_chunk = kc.shape[2]
    n_sel = seq // SEL_BLOCK
    q_blocks = GROUP_HEADS * LANES
    k_col0 = KV_GROUPS * q_blocks // LANES
    vt5 = vt.reshape(batch, n_tiles, KV_GROUPS, 2 * HEAD_DIM, tq)
    gates5 = gates.reshape(batch, n_tiles, KV_GROUPS, 16, tq)
    width = GROUP_HEADS * tq
    assert n_sel <= LANES, "block one-hot is one lane tile wide"
    assert n_tiles % SEL_SPAN == 0 and n_tiles >= WIN_TILES
    hot = jnp.asarray((np.arange(seq)[:, None] // SEL_BLOCK) == np.arange(LANES)[None, :], BF16)
    dist = (np.arange(WIN_TILES)[:, None, None] * tq + np.arange(tq)[None, None, :]
            - np.arange(WIN_TILES * tq)[None, :, None])
    win_bias = jnp.asarray(np.where((dist >= 0) & (dist < WINDOW), 0.0, NEG), F32)
    return pl.pallas_call(
        _nsa_attn_kernel,
        grid=(batch, KV_GROUPS, n_tiles),
        in_specs=[
            pl.BlockSpec((tq, q_blocks), lambda b, g, i: (b * n_tiles + i, g)),
            pl.BlockSpec((seq, LANES), lambda b, g, i: (b, k_col0 + g)),
            pl.BlockSpec((seq, LANES), lambda b, g, i: (0, 0)),
            pl.BlockSpec((None, n_tiles, None, 2 * HEAD_DIM, tq), lambda b, g, i: (b, 0, g, 0, 0)),
            pl.BlockSpec((1, 1, n_chunk, LANES), lambda b, g, i: (b, g, 0, 0)),
            pl.BlockSpec((1, 1, HEAD_DIM, n_chunk), lambda b, g, i: (b, g, 0, 0)),
            pl.BlockSpec((n_sel, n_chunk), lambda b, g, i: (0, 0)),
            pl.BlockSpec((WIN_TILES, WIN_TILES * tq, tq), lambda b, g, i: (0, 0, 0)),
            pl.BlockSpec((None, None, None, 16, tq), lambda b, g, i: (b, i, g, 0, 0)),
        ],
        out_specs=pl.BlockSpec((tq, GROUP_WIDTH), lambda b, g, i: (b * n_tiles + i, g)),
        out_shape=jax.ShapeDtypeStruct((batch * seq, PRIMARY_WIDTH), BF16),
        scratch_shapes=[
            pltpu.VMEM((width, 2 * LANES), BF16),
            pltpu.VMEM((1, width), F32),
            pltpu.VMEM((ACC_ROWS, width), F32),
            pltpu.SMEM((1,), F32),
        ],
        compiler_params=_params("parallel", "parallel", "arbitrary"),
        name="nsa_attention",
    )(qk, qk, hot, vt5, kc, vct, _tap_matrix(n_sel, n_chunk), win_bias, gates5)


def _rope_tables(positions):
    inv = ROPE_THETA ** (-jnp.arange(0, ROT_DIM, 2, dtype=F32) / ROT_DIM)
    ang = positions.astype(F32)[..., None] * inv
    cos, sin = lax.optimization_barrier((jnp.cos(ang), jnp.sin(ang)))
    reps = (1,) * (ang.ndim - 1) + (LANES // ROT_HALF,)
    cos_t, sin_t = jnp.tile(cos, reps), jnp.tile(sin, reps)
    dim = jnp.arange(LANES) % HEAD_DIM
    lo, hi = dim < ROT_HALF, (dim >= ROT_HALF) & (dim < ROT_DIM)
    return (jnp.where(lo | hi, cos_t, 1.0), jnp.where(hi, sin_t, 0.0), jnp.where(lo, -sin_t, 0.0),
            jnp.where(lo | hi, sin_t, 0.0))


def _nsa_layer(hn, w_in, gate_b, pos_k, pos_v, k_w1, k_w2, v_w1, v_w2, positions, batch, seq):
    t, d = hn.shape
    pw, kw = PRIMARY_WIDTH, KV_WIDTH
    offs = np.cumsum([0, pw] + [kw] * 6 + [3 * KV_GROUPS * GROUP_HEADS, MEM_WIDTH])
    col = lambda i: w_in[:, offs[i]:offs[i + 1]]
    w_q, w_kc, w_vc, w_ks, w_vs, w_kw, w_vw, w_gl, w_qm = [col(i) for i in range(9)]

    wq_h = w_q.reshape(d, KV_GROUPS * GROUP_HEADS, 1, HEAD_DIM)
    wq_dup = jnp.broadcast_to(wq_h, (d, KV_GROUPS * GROUP_HEADS, 2, HEAD_DIM)).reshape(d, 2 * pw)
    wk_pair = jnp.stack([w_ks.reshape(d, KV_GROUPS, HEAD_DIM), w_kw.reshape(d, KV_GROUPS, HEAD_DIM)], axis=2)
    w_rope = jnp.concatenate([wq_dup, wk_pair.reshape(d, 2 * kw)], axis=1)
    scale = jnp.concatenate([jnp.full((1, 2 * pw), ATTN_SCALE * LOG2E, F32), jnp.ones((1, 2 * kw), F32)], axis=1)
    cos, sin_hi, sin_lo, _ = _rope_tables(positions.reshape(t))
    qk = _proj(hn, w_rope, tn=w_rope.shape[1] // 3, out_dtype=BF16,
               rope=(cos, sin_hi, sin_lo, scale), name="nsa_proj_rope")

    wv_pair = jnp.stack([w_vs.reshape(d, KV_GROUPS, HEAD_DIM), w_vw.reshape(d, KV_GROUPS, HEAD_DIM)], axis=2)
    wv_t = wv_pair.reshape(d, 2 * kw).T
    vt = _proj_t(hn, wv_t, jnp.zeros((2 * kw, 1), F32), tm=ATTN_TILE, tn=2 * kw,
                 out_dtype=BF16, gate=False, name="nsa_proj_values")

    per_group = 3 * GROUP_HEADS
    wg = jnp.pad(w_gl.reshape(d, KV_GROUPS, per_group), ((0, 0), (0, 0), (0, 16 - per_group)))
    bg = jnp.pad(gate_b.reshape(KV_GROUPS, per_group), ((0, 0), (0, 16 - per_group)))
    n_gate = LANES
    wg_t = jnp.pad(wg.reshape(d, KV_GROUPS * 16).T, ((0, n_gate - KV_GROUPS * 16), (0, 0)))
    bg_t = jnp.pad(bg.reshape(KV_GROUPS * 16, 1), ((0, n_gate - KV_GROUPS * 16), (0, 0)))
    gates = _proj_t(hn, wg_t, bg_t, tm=ATTN_TILE, tn=n_gate, out_dtype=F32, gate=True,
                    name="nsa_proj_gates")[:, :KV_GROUPS * 16, :]

    w_plain = jnp.concatenate([w_qm, w_kc, w_vc], axis=1)
    plain = _proj(hn, w_plain, tn=w_plain.shape[1], out_dtype=F32, name="nsa_proj_plain")
    q_mem = plain[:, :MEM_WIDTH].astype(BF16)
    raw_k = plain[:, MEM_WIDTH:MEM_WIDTH + kw]
    raw_v = plain[:, MEM_WIDTH + kw:]

    n_chunk = seq // CMP_STRIDE
    cmp_end = jnp.minimum(jnp.arange(n_chunk) * CMP_STRIDE + CMP_BLOCK - 1, seq - 1)
    cos_c, _, _, sin_c = _rope_tables(positions[:, cmp_end])
    kc, vct = _compress(raw_k, raw_v, pos_k, pos_v, k_w1, k_w2, v_w1, v_w2, cos_c, sin_c, batch, seq)
    prim = _nsa_attention(qk, vt, gates, kc, vct, batch, seq)
    return prim, q_mem


def _split_dot(x, w_bf16):
    hi = x.astype(BF16)
    lo = (x - hi.astype(F32)).astype(BF16)
    return _dot(hi, w_bf16) + _dot(lo, w_bf16)


def _head_ones(width):
    head = np.arange(width) // HEAD_DIM
    return jnp.asarray(head[:, None] == head[None, :], BF16)


def _rwkv_prep_kernel(hn_ref, w_ref, mu_ref, w0_ref, w2_ref, a0_ref, a2_ref, g2_ref, kk_ref, ka_ref, rk_ref,
                      ones_ref, r_ref, k_ref, v_ref, na_ref, b_ref, lw_ref, g_ref, bonus_ref, last_ref,
                      *, tiles_per_seq):
    pw = PRIMARY_WIDTH
    @pl.when(pl.program_id(0) == 0)
    def _():
        last_ref[...] = jnp.zeros_like(last_ref)

    x = _dot(hn_ref[...], w_ref[...])
    first_tile = (pl.program_id(0) % tiles_per_seq) == 0
    last_prev = jnp.where(first_tile, 0.0, last_ref[0:1, :])
    row = lax.broadcasted_iota(jnp.int32, x.shape, 0)
    prev = jnp.where(row == 0, last_prev, pltpu.roll(x, 1, 0))
    last_ref[0:1, :] = x[x.shape[0] - 1:, :]
    xs = x + (prev - x) * mu_ref[...]
    r = xs[:, 0:pw]
    k = xs[:, pw:2 * pw]
    v = xs[:, 2 * pw:3 * pw]
    lora = xs[:, 3 * pw:3 * pw + DECAY_LORA + AAA_LORA]
    gl = xs[:, 3 * pw + DECAY_LORA + AAA_LORA:]
    w_pre = w0_ref[...] + _dot(jnp.tanh(lora).astype(BF16), w2_ref[...])
    z = -w_pre
    w = -(jnp.maximum(z, 0.0) + jnp.log1p(jnp.exp(-jnp.abs(z)))) - 0.5
    lw_ref[...] = -jnp.exp(w)
    a = jax.nn.sigmoid(a0_ref[...] + _dot(lora.astype(BF16), a2_ref[...]))
    g_ref[...] = _dot(jax.nn.sigmoid(gl).astype(BF16), g2_ref[...]).astype(g_ref.dtype)
    ones = ones_ref[...]

    def head_sums(y):
        gw = GROUP_WIDTH
        return jnp.concatenate([_split_dot(y[:, i * gw:(i + 1) * gw], ones) for i in range(KV_GROUPS)], axis=1)

    kk = k * kk_ref[...]
    kk = kk * lax.rsqrt(jnp.maximum(head_sums(kk * kk), 1e-24))
    k2 = k * (1.0 + (a - 1.0) * ka_ref[...])
    r_ref[...] = r
    k_ref[...] = k2
    v_ref[...] = v.astype(v_ref.dtype)
    na_ref[...] = -kk
    b_ref[...] = kk * a
    bonus_ref[...] = (head_sums(r * k2 * rk_ref[...]) * v).astype(bonus_ref.dtype)


def _rwkv_prep(hn, w_in, mu, w0, w2, a0, a2, g2, k_k, k_a, r_k, seq):
    t, d = hn.shape
    width = w_in.shape[1]
    pw = PRIMARY_WIDTH
    tm = min(PREP_TILE, seq)
    row = lambda a: a.reshape(1, -1)
    lora_w = DECAY_LORA + AAA_LORA
    w2e = jnp.concatenate([w2, jnp.zeros((AAA_LORA, pw), F32)], axis=0).astype(BF16)
    a2e = jnp.concatenate([jnp.zeros((DECAY_LORA, pw), F32), a2], axis=0).astype(BF16)
    const = lambda shape: pl.BlockSpec(shape, lambda i: (0, 0))
    out_spec = pl.BlockSpec((tm, pw), lambda i: (i, 0))
    return pl.pallas_call(
        functools.partial(_rwkv_prep_kernel, tiles_per_seq=seq // tm),
        grid=(t // tm,),
        in_specs=[
            pl.BlockSpec((tm, d), lambda i: (i, 0)),
            const((d, width)),
            const((1, width)), const((1, pw)), const((lora_w, pw)), const((1, pw)), const((lora_w, pw)),
            const((GATE_LORA, pw)), const((1, pw)), const((1, pw)), const((1, pw)),
            const((GROUP_WIDTH, GROUP_WIDTH)),
        ],
        out_specs=[out_spec] * 8,
        out_shape=[jax.ShapeDtypeStruct((t, pw), dt) for dt in (F32, F32, BF16, F32, F32, F32, BF16, BF16)],
        scratch_shapes=[pltpu.VMEM((SUBLANES, width), F32)],
        compiler_params=_params("arbitrary"),
        name="rwkv_prep",
    )(hn, w_in.astype(BF16), row(mu), row(w0), w2e, row(a0), a2e, g2.astype(BF16), row(k_k), row(k_a), row(r_k),
      _head_ones(GROUP_WIDTH))


def _block_diag(x, mask):
    return jnp.where(mask, jnp.concatenate([x.astype(BF16)] * GROUP_HEADS, axis=0), jnp.zeros((), BF16))


def _rwkv_scan_kernel(r_ref, k_ref, v_ref, na_ref, b_ref, lw_ref, g_ref, bonus_ref, lng_ref, lnb_ref, o_ref, state_ref):
    n_batch, c, _ = r_ref.shape
    gw = GROUP_WIDTH
    units = [(bi, grp) for bi in range(n_batch) for grp in range(KV_GROUPS)]
    every = lambda fn, *lists: [fn(*args) for args in zip(*lists)] if lists else [fn(u) for u in units]

    @pl.when(pl.program_id(0) == 0)
    def _():
        state_ref[...] = jnp.zeros_like(state_ref)

    rr = lax.broadcasted_iota(jnp.int32, (gw, gw), 0)
    cc = lax.broadcasted_iota(jnp.int32, (gw, gw), 1)
    bd_mask = (rr // HEAD_DIM) == (cc // HEAD_DIM)
    t_idx = lax.broadcasted_iota(jnp.int32, (c, gw), 0)
    s_idx = lax.broadcasted_iota(jnp.int32, (c, gw), 1) % HEAD_DIM
    strict = t_idx > s_idx
    incl = t_idx >= s_idx
    eye = jnp.where(t_idx == s_idx, 1.0, 0.0)
    same_block = {}
    size = INV_BASE
    while size <= c:
        same_block[size] = (t_idx // size) == (s_idx // size)
        size *= 2
    tril =jnp.where(lax.broadcasted_iota(jnp.int32, (c, c), 0) >= lax.broadcasted_iota(jnp.int32, (c, c), 1),
                     1.0, 0.0).astype(BF16)
    ones_bd = jnp.where(bd_mask, 1.0, 0.0).astype(BF16)

    def cumsum_rows(x):
        hi = x.astype(BF16)
        rem = x - hi.astype(F32)
        mid = rem.astype(BF16)
        lo = (rem - mid.astype(F32)).astype(BF16)
        return _dot(tril, hi) + _dot(tril, mid) + _dot(tril, lo)

    bd = lambda x: _block_diag(x, bd_mask)
    cols = lambda grp: slice(grp * gw, (grp + 1) * gw)
    load = lambda ref: [ref[bi, :, cols(grp)] for bi, grp in units]
    r, k, v, na, bv, lw = (load(ref) for ref in (r_ref, k_ref, v_ref, na_ref, b_ref, lw_ref))
    state = [state_ref[bi * KV_GROUPS + grp] for bi, grp in units]

    cum = every(cumsum_rows, lw)
    p_incl = every(jnp.exp, cum)
    inv_p = every(lambda x: jnp.exp(-x), cum)
    b_t = every(lambda x, s: (x * s).astype(BF16), bv, inv_p)
    k_t = every(lambda x, s: (x * s).astype(BF16), k, inv_p)
    x2 = every(lambda a, cu, l, rr_, p: jnp.concatenate([a * jnp.exp(cu - l), rr_ * p], axis=0).astype(BF16),
               na, cum, lw, r, p_incl)

    g_b = every(lambda x, y: _dot_nt(x, bd(y)), x2, b_t)
    g_k = every(lambda x, y: _dot_nt(x, bd(y)), x2, k_t)
    xh = every(lambda x, s: _dot_nt(x, s.astype(BF16)), x2, state)
    l_ab = every(lambda g: jnp.where(strict, g[:c], 0.0), g_b)
    m_rb = every(lambda g: jnp.where(incl, g[c:], 0.0).astype(BF16), g_b)
    l_ak = every(lambda g: jnp.where(strict, g[:c], 0.0).astype(BF16), g_k)
    m_rk = every(lambda g: jnp.where(incl, g[c:], 0.0).astype(BF16), g_k)

    power = every(lambda l: jnp.where(same_block[INV_BASE], l, 0.0), l_ab)
    t_inv = every(lambda p: eye + p, power)
    for _ in range(int(np.log2(INV_BASE)) - 1):
        power = every(lambda p: _dot(p.astype(BF16), bd(p)), power)
        t_inv = every(lambda t, p: t + _dot(t.astype(BF16), bd(p)), t_inv, power)
    size = 2 * INV_BASE
    while size <= c:
        off_mask = same_block[size] & ~same_block[size // 2]
        half = every(lambda t, l: _dot(t.astype(BF16), bd(jnp.where(off_mask, l, 0.0))), t_inv, l_ab)
        t_inv = every(lambda t, hf: t + _dot(hf.astype(BF16), bd(t)), t_inv, half)
        size *= 2

    v_bd = every(bd, v)
    rhs = every(lambda x, l, vb: x[:c] + _dot(l, vb), xh, l_ak, v_bd)
    u = every(lambda t, rh: _dot(t.astype(BF16), bd(rh)), t_inv, rhs)
    out = every(lambda x, mb, uu, mk, vb: x[c:] + _dot(mb, bd(uu)) + _dot(mk, vb), xh, m_rb, u, m_rk, v_bd)
    delta = every(lambda uu, vv, b, kk_: lax.dot_general(jnp.concatenate([uu.astype(BF16), vv.astype(BF16)], axis=0),
                                                         jnp.concatenate([b, kk_], axis=0), _TN,
                                                         preferred_element_type=F32), u, v, b_t, k_t)
    mean = every(lambda o: _split_dot(o, ones_bd) * (1.0 / HEAD_DIM), out)
    dev = every(lambda o, m: o - m, out, mean)
    var = every(lambda dv: _split_dot(dv * dv, ones_bd) * (1.0 / HEAD_DIM), dev)
    for i, (bi, grp) in enumerate(units):
        state_ref[bi * KV_GROUPS + grp] = (state[i] + jnp.where(bd_mask, delta[i], 0.0)) * p_incl[i][c - 1:c, :]
        y = dev[i] * lax.rsqrt(var[i] + RWKV_GN_EPS) * lng_ref[:, cols(grp)] + lnb_ref[:, cols(grp)]
        o_ref[bi, :, cols(grp)] = ((y + bonus_ref[bi, :, cols(grp)]) * g_ref[bi, :, cols(grp)]).astype(o_ref.dtype)


def _rwkv_scan(r, k, v, na, b, lw, g, bonus, ln_g, ln_b, batch, seq):
    c = SCAN_CHUNK
    n_chunks = seq // c
    pw = PRIMARY_WIDTH
    assert GROUP_HEADS * c == GROUP_WIDTH, "block-diagonal packing needs a 64-token chunk"
    blk = pl.BlockSpec((batch, c, pw), lambda ci: (0, ci, 0))
    const = pl.BlockSpec((1, pw), lambda ci: (0, 0))
    per_batch = lambda a: a.reshape(batch, seq, pw)
    out = pl.pallas_call(
        _rwkv_scan_kernel,
        grid=(n_chunks,),
        in_specs=[blk] * 8 + [const, const],
        out_specs=blk,
        out_shape=jax.ShapeDtypeStruct((batch, seq, pw), BF16),
        scratch_shapes=[pltpu.VMEM((batch * KV_GROUPS, GROUP_WIDTH, GROUP_WIDTH), F32)],
        compiler_params=_params("arbitrary"),
        name="rwkv_scan",
    )(*(per_batch(a) for a in (r, k, v, na, b, lw, g, bonus)), ln_g.reshape(1, pw), ln_b.reshape(1, pw))
    return out.reshape(batch * seq, pw)


def _rwkv_layer(hn, w_in, mu, w0, w2, a0, a2, g2, k_k, k_a, r_k, ln_g, ln_b, batch, seq):
    q_mem = _proj(hn, w_in[:, RWKV_SHIFT_W:], tn=MEM_WIDTH, out_dtype=BF16, name="rwkv_proj_mem")
    r, k, v, na, b, lw, g, bonus = _rwkv_prep(hn, w_in[:, :RWKV_SHIFT_W], mu, w0, w2, a0, a2, g2, k_k, k_a,
                                              r_k.reshape(-1), seq)
    prim = _rwkv_scan(r, k, v, na, b, lw, g, bonus, ln_g, ln_b, batch, seq)
    return prim, q_mem


def kernel(x, mem, positions, mem_norm_g, w_mem_kv, pre_mix_g, post_mix_g, pre_ffn_g, post_ffn_g, w_out, w_ffn_in, w_ffn_out, nsa_w_in, nsa_gate_b, nsa_cmp_pos_k, nsa_cmp_pos_v, nsa_cmp_k_w1, nsa_cmp_k_w2, nsa_cmp_v_w1, nsa_cmp_v_w2, rwkv_w_in, rwkv_mu, rwkv_w0, rwkv_w2, rwkv_a0, rwkv_a2, rwkv_g2, rwkv_k_k, rwkv_k_a, rwkv_r_k, rwkv_ln_g, rwkv_ln_b):
    batch, seq, d = x.shape
    n_mem = mem.shape[1]
    depth = pre_mix_g.shape[0]
    mkv = _proj(_norm_cast(mem.reshape(batch * n_mem, d), mem_norm_g), w_mem_kv, tn=w_mem_kv.shape[1],
                out_dtype=BF16, name="mem_kv")
    mem_k = mkv[:, :MEM_WIDTH].reshape(batch, n_mem, MEM_WIDTH)
    mem_v = mkv[:, MEM_WIDTH:].reshape(batch, n_mem, MEM_WIDTH)
    h = x.reshape(batch * seq, d)
    hn = _norm_cast(h, pre_mix_g[0])
    for i in range(depth):
        j = i // 2
        if i % 2 == 0:
            prim, q_mem = _nsa_layer(hn, nsa_w_in[j], nsa_gate_b[j], nsa_cmp_pos_k[j], nsa_cmp_pos_v[j],
                                     nsa_cmp_k_w1[j], nsa_cmp_k_w2[j], nsa_cmp_v_w1[j], nsa_cmp_v_w2[j],
                                     positions, batch, seq)
        else:
            prim, q_mem = _rwkv_layer(hn, rwkv_w_in[j], rwkv_mu[j], rwkv_w0[j], rwkv_w2[j], rwkv_a0[j],
                                      rwkv_a2[j], rwkv_g2[j], rwkv_k_k[j], rwkv_k_a[j], rwkv_r_k[j], rwkv_ln_g[j],
                                      rwkv_ln_b[j], batch, seq)
        mo = _mem_attn(q_mem, mem_k, mem_v, seq)
        h = _out_proj(prim, mo, w_out[i], post_mix_g[i], h)
        if i + 1 < depth:
            h, hn = _ffn(h, pre_ffn_g[i], w_ffn_in[i], w_ffn_out[i], post_ffn_g[i], next_g=pre_mix_g[i + 1])
        else:
            h = _ffn(h, pre_ffn_g[i], w_ffn_in[i], w_ffn_out[i], post_ffn_g[i])
    return h.reshape(batch, seq, d)
```

```python
import functools

import numpy as np
import jax
import jax.numpy as jnp
from jax import lax
from jax.experimental import pallas as pl
from jax.experimental.pallas import tpu as pltpu

F32 = jnp.float32
BF16 = jnp.bfloat16

HEAD_DIM = 64
ROT_DIM = HEAD_DIM // 4
ROT_HALF = ROT_DIM // 2
ROPE_THETA = 500000.0
MEM_HEADS = 4
MEM_WIDTH = MEM_HEADS * HEAD_DIM
KV_GROUPS = 3
GROUP_HEADS = 4
GROUP_WIDTH = GROUP_HEADS * HEAD_DIM
PRIMARY_WIDTH = KV_GROUPS * GROUP_WIDTH
KV_WIDTH = KV_GROUPS * HEAD_DIM
CMP_BLOCK = 32
CMP_STRIDE = 16
SEL_BLOCK = 64
SEL_TOPK = 16
WINDOW = 512
DECAY_LORA = 64
AAA_LORA = 64
GATE_LORA = 128
RWKV_SHIFT_W = 3 * PRIMARY_WIDTH + DECAY_LORA + AAA_LORA + GATE_LORA
RWKV_GN_EPS = HEAD_DIM * 1e-5
NORM_EPS = 1e-6
NEG = -1e30
FORCE = 1e6
ATTN_SCALE = HEAD_DIM ** -0.5
LOG2E = float(np.log2(np.e))
SAFE_LOG2 = 80.0

LANES = 128
SUBLANES = 8
VMEM_LIMIT_BYTES = 48 * 1024 * 1024

ROW_TILE = 512
ATTN_TILE = 256
SCAN_CHUNK = 64
INV_BASE = 8
SEL_SPAN = 2
SEL_UNROLL = 2
WIN_TILES = WINDOW // ATTN_TILE + 1
N_FORCED = 3
ACC_ROWS = HEAD_DIM + 16
PREP_TILE = 256

_NT = (((1,), (1,)), ((), ()))
_TN = (((0,), (0,)), ((), ()))


def _params(*sem):
    return pltpu.CompilerParams(dimension_semantics=sem, vmem_limit_bytes=VMEM_LIMIT_BYTES)


def _rms(x, g):
    return x * lax.rsqrt(jnp.mean(x * x, axis=-1, keepdims=True) + NORM_EPS) * g


def _dot(a, b):
    return jnp.dot(a, b, preferred_element_type=F32)


def _dot_nt(a, b):
    return lax.dot_general(a, b, _NT, preferred_element_type=F32)


def _norm_cast_kernel(x_ref, g_ref, o_ref):
    o_ref[...] = _rms(x_ref[...], g_ref[...]).astype(o_ref.dtype)


def _norm_cast(x, g):
    t, d = x.shape
    tm = min(ROW_TILE, t)
    return pl.pallas_call(
        _norm_cast_kernel,
        grid=(t // tm,),
        in_specs=[pl.BlockSpec((tm, d), lambda i: (i, 0)), pl.BlockSpec((1, d), lambda i: (0, 0))],
        out_specs=pl.BlockSpec((tm, d), lambda i: (i, 0)),
        out_shape=jax.ShapeDtypeStruct((t, d), BF16),
        compiler_params=_params("parallel"),
        name="norm_cast",
    )(x, g.reshape(1, d))


def _proj_kernel(x_ref, w_ref, o_ref):
    o_ref[...] = _dot(x_ref[...], w_ref[...]).astype(o_ref.dtype)


def _proj_rope_kernel(x_ref, w_ref, c_ref, sp_ref, sm_ref, scale_ref, o_ref):
    y = _dot(x_ref[...], w_ref[...])
    width = y.shape[1]
    reps = width // LANES
    cos = jnp.concatenate([c_ref[...]] * reps, axis=1)
    sin_hi = jnp.concatenate([sp_ref[...]] * reps, axis=1)
    sin_lo = jnp.concatenate([sm_ref[...]] * reps, axis=1)
    y = y * cos + pltpu.roll(y, ROT_HALF, 1) * sin_hi + pltpu.roll(y, width - ROT_HALF, 1) * sin_lo
    o_ref[...] = (y * scale_ref[...]).astype(o_ref.dtype)


def _proj_t_kernel(x_ref, wt_ref, b_ref, o_ref, *, gate):
    y = _dot_nt(wt_ref[...], x_ref[...])
    if gate:
        y = jax.nn.sigmoid(y + b_ref[...])
    o_ref[0] = y.astype(o_ref.dtype)


def _proj(x, w, *, tn, out_dtype, rope=None, name):
    t, d = x.shape
    n = w.shape[1]
    tm = min(ROW_TILE, t)
    in_specs = [
        pl.BlockSpec((tm, d), lambda i, j: (i, 0)),
        pl.BlockSpec((d, tn), lambda i, j: (0, j)),
    ]
    args = [x, w.astype(BF16)]
    if rope is None:
        body = _proj_kernel
    else:
        body = _proj_rope_kernel
        cos, sin_hi, sin_lo, scale = rope
        in_specs += [pl.BlockSpec((tm, LANES), lambda i, j: (i, 0))] * 3
        in_specs += [pl.BlockSpec((1, tn), lambda i, j: (0, j))]
        args += [cos, sin_hi, sin_lo, scale]
    return pl.pallas_call(
        body,
        grid=(t // tm, n // tn),
        in_specs=in_specs,
        out_specs=pl.BlockSpec((tm, tn), lambda i, j: (i, j)),
        out_shape=jax.ShapeDtypeStruct((t, n), out_dtype),
        compiler_params=_params("parallel", "parallel"),
        name=name,
    )(*args)


def _proj_t(x, wt, bias, *, tm, tn, out_dtype, gate, name):
    t, d = x.shape
    n = wt.shape[0]
    return pl.pallas_call(
        functools.partial(_proj_t_kernel, gate=gate),
        grid=(t // tm, n // tn),
        in_specs=[
            pl.BlockSpec((tm, d), lambda i, j: (i, 0)),
            pl.BlockSpec((tn, d), lambda i, j: (j, 0)),
            pl.BlockSpec((tn, 1), lambda i, j: (j, 0)),
        ],
        out_specs=pl.BlockSpec((1, tn, tm), lambda i, j: (i, j, 0)),
        out_shape=jax.ShapeDtypeStruct((t // tm, n, tm), out_dtype),
        compiler_params=_params("parallel", "parallel"),
        name=name,
    )(x, wt.astype(BF16), bias)


def _outproj_kernel(a_ref, b_ref, wa_ref, wb_ref, g_ref, h_ref, o_ref):
    y = _dot(a_ref[...], wa_ref[...]) + _dot(b_ref[...], wb_ref[...])
    o_ref[...] = h_ref[...] + _rms(y, g_ref[...])


def _out_proj(prim, mo, w_out, g, h):
    t, d = h.shape
    tm = min(ROW_TILE, t)
    pw = prim.shape[1]
    return pl.pallas_call(
        _outproj_kernel,
        grid=(t // tm,),
        in_specs=[
            pl.BlockSpec((tm, pw), lambda i: (i, 0)),
            pl.BlockSpec((tm, MEM_WIDTH), lambda i: (i, 0)),
            pl.BlockSpec((pw, d), lambda i: (0, 0)),
            pl.BlockSpec((MEM_WIDTH, d), lambda i: (0, 0)),
            pl.BlockSpec((1, d), lambda i: (0, 0)),
            pl.BlockSpec((tm, d), lambda i: (i, 0)),
        ],
        out_specs=pl.BlockSpec((tm, d), lambda i: (i, 0)),
        out_shape=jax.ShapeDtypeStruct((t, d), F32),
        compiler_params=_params("parallel"),
        name="out_proj",
    )(prim, mo, w_out[:pw].astype(BF16), w_out[pw:].astype(BF16), g.reshape(1, d), h)


def _ffn_kernel(h_ref, g1_ref, wg_ref, wu_ref, wo_ref, g2_ref, *rest, feeds_next):
    if feeds_next:
        g3_ref, o_ref, next_ref, hn_ref, acc_ref = rest
    else:
        o_ref, hn_ref, acc_ref = rest
    j = pl.program_id(1)

    @pl.when(j == 0)
    def _():
        hn_ref[...] = _rms(h_ref[...], g1_ref[...]).astype(BF16)
        acc_ref[...] = jnp.zeros_like(acc_ref)

    hn = hn_ref[...]
    gate = _dot(hn, wg_ref[...])
    up = _dot(hn, wu_ref[...])
    act = (jax.nn.silu(gate) * up).astype(BF16)
    acc_ref[...] += _dot(act, wo_ref[...])

    @pl.when(j == pl.num_programs(1) - 1)
    def _():
        out = h_ref[...] + _rms(acc_ref[...], g2_ref[...])
        o_ref[...] = out
        if feeds_next:
            next_ref[...] = _rms(out, g3_ref[...]).astype(next_ref.dtype)


def _ffn_chunk(hidden):
    units = hidden // LANES
    for parts in range(2, units + 1):
        if units % parts == 0:
            return (units // parts) * LANES
    return hidden


def _ffn(h, g1, w_in, w_out, g2, next_g=None):
    t, d = h.shape
    hidden = w_out.shape[0]
    th = _ffn_chunk(hidden)
    nh = hidden // th
    tm = min(ROW_TILE, t)
    w_in = w_in.astype(BF16)
    row = pl.BlockSpec((tm, d), lambda i, j: (i, 0))
    gain = pl.BlockSpec((1, d), lambda i, j: (0, 0))
    feeds_next = next_g is not None
    in_specs = [row, gain,
                pl.BlockSpec((d, th), lambda i, j: (0, j)),
                pl.BlockSpec((d, th), lambda i, j: (0, j + nh)),
                pl.BlockSpec((th, d), lambda i, j: (j, 0)),
                gain]
    args = [h, g1.reshape(1, d), w_in, w_in, w_out.astype(BF16), g2.reshape(1, d)]
    out_specs, out_shape = row, jax.ShapeDtypeStruct((t, d), F32)
    if feeds_next:
        in_specs.append(gain)
        args.append(next_g.reshape(1, d))
        out_specs, out_shape = [row, row], [out_shape, jax.ShapeDtypeStruct((t, d), BF16)]
    return pl.pallas_call(
        functools.partial(_ffn_kernel, feeds_next=feeds_next),
        grid=(t // tm, nh),
        in_specs=in_specs,
        out_specs=out_specs,
        out_shape=out_shape,
        scratch_shapes=[pltpu.VMEM((tm, d), BF16), pltpu.VMEM((tm, d), F32)],
        compiler_params=_params("parallel", "arbitrary"),
        name="ffn",
    )(*args)


def _mem_attn_kernel(q_ref, mk_ref, mv_ref, o_ref):
    q = q_ref[...]
    mk = mk_ref[0]
    mv = mv_ref[0]
    head_of_lane = lax.broadcasted_iota(jnp.int32, mk.shape, 1) // HEAD_DIM
    acc = jnp.zeros(q.shape, F32)
    for h in range(MEM_HEADS):
        s = _dot_nt(q, jnp.where(head_of_lane == h, mk, 0)) * ATTN_SCALE
        e = jnp.exp(s - jnp.max(s, axis=-1, keepdims=True))
        p = e / jnp.sum(e, axis=-1, keepdims=True)
        acc = acc + _dot(p.astype(BF16), jnp.where(head_of_lane == h, mv, 0))
    o_ref[...] = acc.astype(o_ref.dtype)


def _mem_attn(q_mem, mem_k, mem_v, seq):
    t = q_mem.shape[0]
    m = mem_k.shape[1]
    tm = min(ROW_TILE, seq)
    per_seq = seq // tm
    return pl.pallas_call(
        _mem_attn_kernel,
        grid=(t // tm,),
        in_specs=[
            pl.BlockSpec((tm, MEM_WIDTH), lambda i: (i, 0)),
            pl.BlockSpec((1, m, MEM_WIDTH), lambda i: (i // per_seq, 0, 0)),
            pl.BlockSpec((1, m, MEM_WIDTH), lambda i: (i // per_seq, 0, 0)),
        ],
        out_specs=pl.BlockSpec((tm, MEM_WIDTH), lambda i: (i, 0)),
        out_shape=jax.ShapeDtypeStruct((t, MEM_WIDTH), BF16),
        compiler_params=_params("parallel"),
        name="mem_attn",
    )(q_mem, mem_k, mem_v)


def _compress_kernel(xk_ref, xv_ref, pk_ref, pv_ref, k1a_ref, k1b_ref, k2_ref, v1a_ref, v1b_ref, v2t_ref,
                     c_ref, s_ref, kc_ref, vct_ref):
    def hidden(x_ref, pos_ref, wa_ref, wb_ref):
        x = x_ref[0, 0]
        n = x.shape[0]
        first = _dot((x + pos_ref[0:1, :]).astype(BF16), wa_ref[...])
        second = _dot((x + pos_ref[1:2, :]).astype(BF16), wb_ref[...])
        return jax.nn.gelu(first + pltpu.roll(second, n - 1, 0)).astype(BF16)

    hk = hidden(xk_ref, pk_ref, k1a_ref, k1b_ref)
    both = _dot(hk, k2_ref[...])
    kc_ref[0, 0] = (both[:, :LANES] * c_ref[0] + both[:, LANES:] * s_ref[0]).astype(kc_ref.dtype)
    hv = hidden(xv_ref, pv_ref, v1a_ref, v1b_ref)
    vct_ref[0, 0] = _dot_nt(v2t_ref[...], hv).astype(vct_ref.dtype)


def _rope_partner_cols(w):
    d = np.arange(w.shape[1]) % HEAD_DIM
    src = np.where(d < ROT_HALF, np.arange(w.shape[1]) + ROT_HALF, np.arange(w.shape[1]) - ROT_HALF)
    src = np.clip(src, 0, w.shape[1] - 1)
    sign = np.where(d < ROT_HALF, -1.0, np.where(d < ROT_DIM, 1.0, 0.0)).astype(np.float32)
    return w[:, src] * sign


def _compress(raw_k, raw_v, pos_k, pos_v, k_w1, k_w2, v_w1, v_w2, cos_c, sin_c, batch, seq):
    n_chunk = seq // CMP_STRIDE
    feat = CMP_STRIDE * HEAD_DIM
    hid = k_w1.shape[1]

    def chunks(raw):
        x = raw.reshape(batch, n_chunk, CMP_STRIDE, KV_GROUPS, HEAD_DIM)
        return x.transpose(0, 3, 1, 2, 4).reshape(batch, KV_GROUPS, n_chunk, feat)

    zeros = jnp.zeros((hid, LANES - HEAD_DIM), F32)
    k2 = jnp.concatenate([k_w2, zeros, _rope_partner_cols(k_w2), zeros], axis=1).astype(BF16)
    x_spec = pl.BlockSpec((1, 1, n_chunk, feat), lambda b, g: (b, g, 0, 0))
    pos_spec = pl.BlockSpec((2, feat), lambda b, g: (0, 0))
    w1_spec = pl.BlockSpec((feat, hid), lambda b, g: (0, 0))
    tab_spec = pl.BlockSpec((1, n_chunk, LANES), lambda b, g: (b, 0, 0))
    return pl.pallas_call(
        _compress_kernel,
        grid=(batch, KV_GROUPS),
        in_specs=[x_spec, x_spec, pos_spec, pos_spec, w1_spec, w1_spec,
                  pl.BlockSpec((hid, 2 * LANES), lambda b, g: (0, 0)),
                  w1_spec, w1_spec,
                  pl.BlockSpec((HEAD_DIM, hid), lambda b, g: (0, 0)),
                  tab_spec, tab_spec],
        out_specs=[pl.BlockSpec((1, 1, n_chunk, LANES), lambda b, g: (b, g, 0, 0)),
                   pl.BlockSpec((1, 1, HEAD_DIM, n_chunk), lambda b, g: (b, g, 0, 0))],
        out_shape=[jax.ShapeDtypeStruct((batch, KV_GROUPS, n_chunk, LANES), BF16),
                   jax.ShapeDtypeStruct((batch, KV_GROUPS, HEAD_DIM, n_chunk), BF16)],
        compiler_params=_params("parallel", "parallel"),
        name="nsa_compress",
    )(chunks(raw_k), chunks(raw_v), pos_k.reshape(2, feat), pos_v.reshape(2, feat),
      k_w1[:feat].astype(BF16), k_w1[feat:].astype(BF16), k2,
      v_w1[:feat].astype(BF16), v_w1[feat:].astype(BF16), v_w2.T.astype(BF16), cos_c, sin_c)


def _split_dot_left(w_bf16, x):
    hi = x.astype(BF16)
    rem = x - hi.astype(F32)
    mid = rem.astype(BF16)
    lo = (rem - mid.astype(F32)).astype(BF16)
    return _dot(w_bf16, hi) + _dot(w_bf16, mid) + _dot(w_bf16, lo)


def _nsa_attn_kernel(q_ref, kk_ref, hot_ref, vt_ref, kc_ref, vct_ref, tap_ref, winb_ref, gate_ref, o_ref,
                     qx_ref, m_ref, acc_ref, kmax_ref):
    tq = q_ref.shape[0]
    qt = pl.program_id(2)
    t0 = qt * tq
    heads = range(GROUP_HEADS)

    q2 = q_ref[...]
    qs = jnp.concatenate([q2[:, h * LANES:(h + 1) * LANES] for h in heads], axis=0)
    lane = lax.broadcasted_iota(jnp.int32, qs.shape, 1)
    q_sel = jnp.where(lane < HEAD_DIM, qs, 0)
    q_win = jnp.where(lane >= HEAD_DIM, qs, 0)

    col_t = t0 + lax.broadcasted_iota(jnp.int32, (1, tq), 1)

    @pl.when(qt == 0)
    def _():
        half = lax.broadcasted_iota(jnp.int32, (1, LANES), 1) < HEAD_DIM
        k_abs = jnp.abs(kk_ref[...].astype(F32))
        kc_abs = jnp.abs(kc_ref[0, 0].astype(F32))
        row_sums = [jnp.sum(jnp.where(half, k_abs, 0.0), axis=1, keepdims=True),
                    jnp.sum(jnp.where(half, 0.0, k_abs), axis=1, keepdims=True),
                    jnp.sum(kc_abs, axis=1, keepdims=True)]
        kmax_ref[0] = functools.reduce(jnp.maximum, [jnp.max(s) for s in row_sums])
    bounded = jnp.max(jnp.abs(qs.astype(F32))) * kmax_ref[0] <= SAFE_LOG2

    @pl.when(bounded)
    def _():
        _nsa_attend(False, q_sel, q_win, col_t, qt, kk_ref, hot_ref, vt_ref, kc_ref, vct_ref, tap_ref, winb_ref,
                    gate_ref, o_ref, qx_ref, m_ref, acc_ref)

    @pl.when(jnp.logical_not(bounded))
    def _():
        _nsa_attend(True, q_sel, q_win, col_t, qt, kk_ref, hot_ref, vt_ref, kc_ref, vct_ref, tap_ref, winb_ref,
                    gate_ref, o_ref, qx_ref, m_ref, acc_ref)


def _nsa_attend(stabilized, q_sel, q_win, col_t, qt, kk_ref, hot_ref, vt_ref, kc_ref, vct_ref, tap_ref, winb_ref,
                gate_ref, o_ref, qx_ref, m_ref, acc_ref):
    tq = col_t.shape[1]
    tk = tq
    heads = range(GROUP_HEADS)
    head_rows = lambda h: slice(h * tq, (h + 1) * tq)

    def weights(scores):
        if not stabilized:
            return jnp.exp2(scores), None
        top = jnp.max(scores, axis=0, keepdims=True)
        return jnp.exp2(scores - top), top > 0.5 * NEG

    n_cmp = kc_ref.shape[2]
    blk_end = lax.broadcasted_iota(jnp.int32, (n_cmp, 1), 0) * CMP_STRIDE + (CMP_BLOCK - 1)
    valid = blk_end <= col_t
    cmp_scores = [_dot_nt(kc_ref[0, 0], q_sel[head_rows(h)]) for h in heads]
    o_cmp = []
    p_sum = jnp.zeros((n_cmp, tq), F32)
    for h in heads:
        e, live = weights(jnp.where(valid, cmp_scores[h], NEG))
        total = jnp.sum(e, axis=0, keepdims=True)
        live = total > 0.0 if live is None else live
        p = e * jnp.where(live, 1.0 / total, 0.0)
        o_cmp.append(_dot(vct_ref[0, 0], p.astype(BF16)))
        p_sum = p_sum + p
    imp = _split_dot_left(tap_ref[...], p_sum)

    n_sel = imp.shape[0]
    blk = lax.broadcasted_iota(jnp.int32, (n_sel, tq), 0)
    blk_f = blk.astype(F32)
    cur = col_t // SEL_BLOCK
    forced = (blk == 0) | (blk == cur) | (blk == cur - 1)
    work = jnp.where(forced, -jnp.inf, jnp.where(blk <= cur, imp, -FORCE))
    chosen = jnp.where(forced, 1.0, 0.0)
    for _ in range(max(min(SEL_TOPK, n_sel) - N_FORCED, 0)):
        best = jnp.max(work, axis=0, keepdims=True)
        hit = blk_f == jnp.min(jnp.where(work == best, blk_f, float(n_sel)), axis=0, keepdims=True)
        work = jnp.where(hit, -jnp.inf, work)
        chosen = jnp.where(hit, 1.0, chosen)
    if n_sel < LANES:
        chosen = jnp.concatenate([chosen, jnp.zeros((LANES - n_sel, tq), F32)], axis=0)
    block_bias = ((chosen - 1.0) * (-NEG)).T.astype(BF16)
    qx_ref[:, 0:LANES] = q_sel
    qx_ref[:, LANES:2 * LANES] = jnp.concatenate([block_bias] * GROUP_HEADS, axis=0)

    def key_pos(kt, n):
        return kt * tk + lax.broadcasted_iota(jnp.int32, (n * tk, 1), 0)

    def key_rows(ref, kt, n):
        return ref[pl.ds(pl.multiple_of(kt * tk, tk), n * tk), :]

    def value_rows(kt, n, first_row):
        vals = jnp.concatenate([vt_ref[kt + i, first_row:first_row + HEAD_DIM, :] for i in range(n)], axis=1)
        return jnp.concatenate([vals, jnp.ones((ACC_ROWS - HEAD_DIM, n * tk), BF16)], axis=0)

    def normalized(acc):
        return acc[0:HEAD_DIM] / acc[HEAD_DIM:HEAD_DIM + 1]

    win_start = jnp.maximum(qt - (WIN_TILES - 1), 0)
    win_bias = winb_ref[qt - win_start]
    win_keys = key_rows(kk_ref, win_start, WIN_TILES)
    win_vals = value_rows(win_start, WIN_TILES, HEAD_DIM)
    win_scores = [_dot_nt(win_keys, q_win[head_rows(h)]) for h in heads]
    o_win = [normalized(_dot(win_vals, weights(win_scores[h] + win_bias)[0].astype(BF16))) for h in heads]

    acc_ref[...] = jnp.zeros(acc_ref.shape, F32)
    full_spans = qt // SEL_SPAN

    def score_span(i):
        kt = i * SEL_SPAN
        keys = jnp.concatenate([key_rows(kk_ref, kt, SEL_SPAN), key_rows(hot_ref, kt, SEL_SPAN)], axis=1)
        return tuple(_dot_nt(keys, qx_ref[head_rows(h), :]) for h in heads)

    def causal_scores(i, scores):
        return jnp.where(key_pos(i * SEL_SPAN, SEL_SPAN) <= col_t, scores, NEG)

    if not stabilized:
        def plain_spans(first, count, causal_last):
            scores = [score_span(first + s) for s in range(count)]
            for s in range(count):
                v_ext = value_rows((first + s) * SEL_SPAN, SEL_SPAN, 0)
                for h in heads:
                    masked = causal_last and s == count - 1
                    e = jnp.exp2(causal_scores(first + s, scores[s][h]) if masked else scores[s][h]).astype(BF16)
                    acc_ref[:, head_rows(h)] += _dot(v_ext, e)

        def plain_step(i, carry):
            plain_spans(i * SEL_UNROLL, SEL_UNROLL, False)
            return carry

        whole = full_spans // SEL_UNROLL
        lax.fori_loop(0, whole, plain_step, 0)
        for left in range(SEL_UNROLL):
            @pl.when(full_spans - whole * SEL_UNROLL == left)
            def _():
                plain_spans(whole * SEL_UNROLL, left + 1, True)
    else:
        m_ref[...] = jnp.full(m_ref.shape, NEG, F32)

        def flash_span(i, span_scores, causal):
            v_ext = value_rows(i * SEL_SPAN, SEL_SPAN, 0)
            for h in heads:
                cols = head_rows(h)
                scores = causal_scores(i, span_scores[h]) if causal else span_scores[h]
                m_old = m_ref[:, cols]
                m_new = jnp.maximum(m_old, jnp.max(scores, axis=0, keepdims=True))
                e = jnp.exp2(scores - m_new).astype(BF16)
                acc_ref[:, cols] = jnp.exp2(m_old - m_new) * acc_ref[:, cols] + _dot(v_ext, e)
                m_ref[:, cols] = m_new

        def sel_step(i, span_scores):
            following = score_span(i + 1)
            flash_span(i, span_scores, False)
            return following

        last_scores = lax.fori_loop(0, full_spans, sel_step, score_span(0))
        flash_span(full_spans, last_scores, True)

    gates = gate_ref[...]
    outs = []
    for h in heads:
        outs.append(gates[3 * h:3 * h + 1, :] * o_cmp[h]
                    + gates[3 * h + 1:3 * h + 2, :] * normalized(acc_ref[:, head_rows(h)])
                    + gates[3 * h + 2:3 * h + 3, :] * o_win[h])
    o_ref[...] = jnp.concatenate(outs, axis=0).T.astype(o_ref.dtype)


def _tap_matrix(n_sel, n_chunk):
    ratio = SEL_BLOCK // CMP_STRIDE
    tap = np.zeros((n_sel, n_chunk), np.float32)
    n_cmp = n_chunk - (CMP_BLOCK // CMP_STRIDE - 1)
    for j in range(n_sel):
        for n in range(n_cmp):
            lo = max(n * CMP_STRIDE, j * SEL_BLOCK)
            hi = min(n * CMP_STRIDE + CMP_BLOCK, (j + 1) * SEL_BLOCK)
            if hi > lo:
                tap[j, n] = (hi - lo) / CMP_STRIDE
    return jnp.asarray(tap, BF16)


def _nsa_attention(qk, vt, gates, kc, vct, batch, seq):
    tq = ATTN_TILE
    n_tiles = seq // tq
    n_chunk = kc.shape[2]
    n_sel = seq // SEL_BLOCK
    q_blocks = GROUP_HEADS * LANES
    k_col0 = KV_GROUPS * q_blocks // LANES
    vt5 = vt.reshape(batch, n_tiles, KV_GROUPS, 2 * HEAD_DIM, tq)
    gates5 = gates.reshape(batch, n_tiles, KV_GROUPS, 16, tq)
    width = GROUP_HEADS * tq
    assert n_sel <= LANES, "block one-hot is one lane tile wide"
    assert n_tiles % SEL_SPAN == 0 and n_tiles >= WIN_TILES
    hot = jnp.asarray((np.arange(seq)[:, None] // SEL_BLOCK) == np.arange(LANES)[None, :], BF16)
    dist = (np.arange(WIN_TILES)[:, None, None] * tq + np.arange(tq)[None, None, :]
            - np.arange(WIN_TILES * tq)[None, :, None])
    win_bias = jnp.asarray(np.where((dist >= 0) & (dist < WINDOW), 0.0, NEG), F32)
    return pl.pallas_call(
        _nsa_attn_kernel,
        grid=(batch, KV_GROUPS, n_tiles),
        in_specs=[
            pl.BlockSpec((tq, q_blocks), lambda b, g, i: (b * n_tiles + i, g)),
            pl.BlockSpec((seq, LANES), lambda b, g, i: (b, k_col0 + g)),
            pl.BlockSpec((seq, LANES), lambda b, g, i: (0, 0)),
            pl.BlockSpec((None, n_tiles, None, 2 * HEAD_DIM, tq), lambda b, g, i: (b, 0, g, 0, 0)),
            pl.BlockSpec((1, 1, n_chunk, LANES), lambda b, g, i: (b, g, 0, 0)),
            pl.BlockSpec((1, 1, HEAD_DIM, n_chunk), lambda b, g, i: (b, g, 0, 0)),
            pl.BlockSpec((n_sel, n_chunk), lambda b, g, i: (0, 0)),
            pl.BlockSpec((WIN_TILES, WIN_TILES * tq, tq), lambda b, g, i: (0, 0, 0)),
            pl.BlockSpec((None, None, None, 16, tq), lambda b, g, i: (b, i, g, 0, 0)),
        ],
        out_specs=pl.BlockSpec((tq, GROUP_WIDTH), lambda b, g, i: (b * n_tiles + i, g)),
        out_shape=jax.ShapeDtypeStruct((batch * seq, PRIMARY_WIDTH), BF16),
        scratch_shapes=[
            pltpu.VMEM((width, 2 * LANES), BF16),
            pltpu.VMEM((1, width), F32),
            pltpu.VMEM((ACC_ROWS, width), F32),
            pltpu.SMEM((1,), F32),
        ],
        compiler_params=_params("parallel", "parallel", "arbitrary"),
        name="nsa_attention",
    )(qk, qk, hot, vt5, kc, vct, _tap_matrix(n_sel, n_chunk), win_bias, gates5)


def _rope_tables(positions):
    inv = ROPE_THETA ** (-jnp.arange(0, ROT_DIM, 2, dtype=F32) / ROT_DIM)
    ang = positions.astype(F32)[..., None] * inv
    cos, sin = lax.optimization_barrier((jnp.cos(ang), jnp.sin(ang)))
    reps = (1,) * (ang.ndim - 1) + (LANES // ROT_HALF,)
    cos_t, sin_t = jnp.tile(cos, reps), jnp.tile(sin, reps)
    dim = jnp.arange(LANES) % HEAD_DIM
    lo, hi = dim < ROT_HALF, (dim >= ROT_HALF) & (dim < ROT_DIM)
    return (jnp.where(lo | hi, cos_t, 1.0), jnp.where(hi, sin_t, 0.0), jnp.where(lo, -sin_t, 0.0),
            jnp.where(lo | hi, sin_t, 0.0))


def _nsa_layer(hn, w_in, gate_b, pos_k, pos_v, k_w1, k_w2, v_w1, v_w2, positions, batch, seq):
    t, d = hn.shape
    pw, kw = PRIMARY_WIDTH, KV_WIDTH
    offs = np.cumsum([0, pw] + [kw] * 6 + [3 * KV_GROUPS * GROUP_HEADS, MEM_WIDTH])
    col = lambda i: w_in[:, offs[i]:offs[i + 1]]
    w_q, w_kc, w_vc, w_ks, w_vs, w_kw, w_vw, w_gl, w_qm = [col(i) for i in range(9)]

    wq_h = w_q.reshape(d, KV_GROUPS * GROUP_HEADS, 1, HEAD_DIM)
    wq_dup = jnp.broadcast_to(wq_h, (d, KV_GROUPS * GROUP_HEADS, 2, HEAD_DIM)).reshape(d, 2 * pw)
    wk_pair = jnp.stack([w_ks.reshape(d, KV_GROUPS, HEAD_DIM), w_kw.reshape(d, KV_GROUPS, HEAD_DIM)], axis=2)
    w_rope = jnp.concatenate([wq_dup, wk_pair.reshape(d, 2 * kw)], axis=1)
    scale = jnp.concatenate([jnp.full((1, 2 * pw), ATTN_SCALE * LOG2E, F32), jnp.ones((1, 2 * kw), F32)], axis=1)
    cos, sin_hi, sin_lo, _ = _rope_tables(positions.reshape(t))
    qk = _proj(hn, w_rope, tn=w_rope.shape[1] // 3, out_dtype=BF16,
               rope=(cos, sin_hi, sin_lo, scale), name="nsa_proj_rope")

    wv_pair = jnp.stack([w_vs.reshape(d, KV_GROUPS, HEAD_DIM), w_vw.reshape(d, KV_GROUPS, HEAD_DIM)], axis=2)
    wv_t = wv_pair.reshape(d, 2 * kw).T
    vt = _proj_t(hn, wv_t, jnp.zeros((2 * kw, 1), F32), tm=ATTN_TILE, tn=2 * kw,
                 out_dtype=BF16, gate=False, name="nsa_proj_values")

    per_group = 3 * GROUP_HEADS
    wg = jnp.pad(w_gl.reshape(d, KV_GROUPS, per_group), ((0, 0), (0, 0), (0, 16 - per_group)))
    bg = jnp.pad(gate_b.reshape(KV_GROUPS, per_group), ((0, 0), (0, 16 - per_group)))
    n_gate = LANES
    wg_t = jnp.pad(wg.reshape(d, KV_GROUPS * 16).T, ((0, n_gate - KV_GROUPS * 16), (0, 0)))
    bg_t = jnp.pad(bg.reshape(KV_GROUPS * 16, 1), ((0, n_gate - KV_GROUPS * 16), (0, 0)))
    gates = _proj_t(hn, wg_t, bg_t, tm=ATTN_TILE, tn=n_gate, out_dtype=F32, gate=True,
                    name="nsa_proj_gates")[:, :KV_GROUPS * 16, :]

    w_plain = jnp.concatenate([w_qm, w_kc, w_vc], axis=1)
    plain = _proj(hn, w_plain, tn=w_plain.shape[1], out_dtype=F32, name="nsa_proj_plain")
    q_mem = plain[:, :MEM_WIDTH].astype(BF16)
    raw_k = plain[:, MEM_WIDTH:MEM_WIDTH + kw]
    raw_v = plain[:, MEM_WIDTH + kw:]

    n_chunk = seq // CMP_STRIDE
    cmp_end = jnp.minimum(jnp.arange(n_chunk) * CMP_STRIDE + CMP_BLOCK - 1, seq - 1)
    cos_c, _, _, sin_c = _rope_tables(positions[:, cmp_end])
    kc, vct = _compress(raw_k, raw_v, pos_k, pos_v, k_w1, k_w2, v_w1, v_w2, cos_c, sin_c, batch, seq)
    prim = _nsa_attention(qk, vt, gates, kc, vct, batch, seq)
    return prim, q_mem


def _split_dot(x, w_bf16):
    hi = x.astype(BF16)
    lo = (x - hi.astype(F32)).astype(BF16)
    return _dot(hi, w_bf16) + _dot(lo, w_bf16)


def _head_ones(width):
    head = np.arange(width) // HEAD_DIM
    return jnp.asarray(head[:, None] == head[None, :], BF16)


def _rwkv_prep_kernel(hn_ref, w_ref, mu_ref, w0_ref, w2_ref, a0_ref, a2_ref, g2_ref, kk_ref, ka_ref, rk_ref,
                      ones_ref, r_ref, k_ref, v_ref, na_ref, b_ref, lw_ref, g_ref, bonus_ref, last_ref,
                      *, tiles_per_seq):
    pw = PRIMARY_WIDTH
    @pl.when(pl.program_id(0) == 0)
    def _():
        last_ref[...] = jnp.zeros_like(last_ref)

    x = _dot(hn_ref[...], w_ref[...])
    first_tile = (pl.program_id(0) % tiles_per_seq) == 0
    last_prev = jnp.where(first_tile, 0.0, last_ref[0:1, :])
    row = lax.broadcasted_iota(jnp.int32, x.shape, 0)
    prev = jnp.where(row == 0, last_prev, pltpu.roll(x, 1, 0))
    last_ref[0:1, :] = x[x.shape[0] - 1:, :]
    xs = x + (prev - x) * mu_ref[...]
    r = xs[:, 0:pw]
    k = xs[:, pw:2 * pw]
    v = xs[:, 2 * pw:3 * pw]
    lora = xs[:, 3 * pw:3 * pw + DECAY_LORA + AAA_LORA]
    gl = xs[:, 3 * pw + DECAY_LORA + AAA_LORA:]
    w_pre = w0_ref[...] + _dot(jnp.tanh(lora).astype(BF16), w2_ref[...])
    z = -w_pre
    w = -(jnp.maximum(z, 0.0) + jnp.log1p(jnp.exp(-jnp.abs(z)))) - 0.5
    lw_ref[...] = -jnp.exp(w)
    a = jax.nn.sigmoid(a0_ref[...] + _dot(lora.astype(BF16), a2_ref[...]))
    g_ref[...] = _dot(jax.nn.sigmoid(gl).astype(BF16), g2_ref[...]).astype(g_ref.dtype)
    ones = ones_ref[...]

    def head_sums(y):
        gw = GROUP_WIDTH
        return jnp.concatenate([_split_dot(y[:, i * gw:(i + 1) * gw], ones) for i in range(KV_GROUPS)], axis=1)

    kk = k * kk_ref[...]
    kk = kk * lax.rsqrt(jnp.maximum(head_sums(kk * kk), 1e-24))
    k2 = k * (1.0 + (a - 1.0) * ka_ref[...])
    r_ref[...] = r
    k_ref[...] = k2
    v_ref[...] = v.astype(v_ref.dtype)
    na_ref[...] = -kk
    b_ref[...] = kk * a
    bonus_ref[...] = (head_sums(r * k2 * rk_ref[...]) * v).astype(bonus_ref.dtype)


def _rwkv_prep(hn, w_in, mu, w0, w2, a0, a2, g2, k_k, k_a, r_k, seq):
    t, d = hn.shape
    width = w_in.shape[1]
    pw = PRIMARY_WIDTH
    tm = min(PREP_TILE, seq)
    row = lambda a: a.reshape(1, -1)
    lora_w = DECAY_LORA + AAA_LORA
    w2e = jnp.concatenate([w2, jnp.zeros((AAA_LORA, pw), F32)], axis=0).astype(BF16)
    a2e = jnp.concatenate([jnp.zeros((DECAY_LORA, pw), F32), a2], axis=0).astype(BF16)
    const = lambda shape: pl.BlockSpec(shape, lambda i: (0, 0))
    out_spec = pl.BlockSpec((tm, pw), lambda i: (i, 0))
    return pl.pallas_call(
        functools.partial(_rwkv_prep_kernel, tiles_per_seq=seq // tm),
        grid=(t // tm,),
        in_specs=[
            pl.BlockSpec((tm, d), lambda i: (i, 0)),
            const((d, width)),
            const((1, width)), const((1, pw)), const((lora_w, pw)), const((1, pw)), const((lora_w, pw)),
            const((GATE_LORA, pw)), const((1, pw)), const((1, pw)), const((1, pw)),
            const((GROUP_WIDTH, GROUP_WIDTH)),
        ],
        out_specs=[out_spec] * 8,
        out_shape=[jax.ShapeDtypeStruct((t, pw), dt) for dt in (F32, F32, BF16, F32, F32, F32, BF16, BF16)],
        scratch_shapes=[pltpu.VMEM((SUBLANES, width), F32)],
        compiler_params=_params("arbitrary"),
        name="rwkv_prep",
    )(hn, w_in.astype(BF16), row(mu), row(w0), w2e, row(a0), a2e, g2.astype(BF16), row(k_k), row(k_a), row(r_k),
      _head_ones(GROUP_WIDTH))


def _block_diag(x, mask):
    return jnp.where(mask, jnp.concatenate([x.astype(BF16)] * GROUP_HEADS, axis=0), jnp.zeros((), BF16))


def _rwkv_scan_kernel(r_ref, k_ref, v_ref, na_ref, b_ref, lw_ref, g_ref, bonus_ref, lng_ref, lnb_ref, o_ref, state_ref):
    n_batch, c, _ = r_ref.shape
    gw = GROUP_WIDTH
    units = [(bi, grp) for bi in range(n_batch) for grp in range(KV_GROUPS)]
    every = lambda fn, *lists: [fn(*args) for args in zip(*lists)] if lists else [fn(u) for u in units]

    @pl.when(pl.program_id(0) == 0)
    def _():
        state_ref[...] = jnp.zeros_like(state_ref)

    rr = lax.broadcasted_iota(jnp.int32, (gw, gw), 0)
    cc = lax.broadcasted_iota(jnp.int32, (gw, gw), 1)
    bd_mask = (rr // HEAD_DIM) == (cc // HEAD_DIM)
    t_idx = lax.broadcasted_iota(jnp.int32, (c, gw), 0)
    s_idx = lax.broadcasted_iota(jnp.int32, (c, gw), 1) % HEAD_DIM
    strict = t_idx > s_idx
    incl = t_idx >= s_idx
    eye = jnp.where(t_idx == s_idx, 1.0, 0.0)
    same_block = {}
    size = INV_BASE
    while size <= c:
        same_block[size] = (t_idx // size) == (s_idx // size)
        size *= 2
    tril =jnp.where(lax.broadcasted_iota(jnp.int32, (c, c), 0) >= lax.broadcasted_iota(jnp.int32, (c, c), 1),
                     1.0, 0.0).astype(BF16)
    ones_bd = jnp.where(bd_mask, 1.0, 0.0).astype(BF16)

    def cumsum_rows(x):
        hi = x.astype(BF16)
        rem = x - hi.astype(F32)
        mid = rem.astype(BF16)
        lo = (rem - mid.astype(F32)).astype(BF16)
        return _dot(tril, hi) + _dot(tril, mid) + _dot(tril, lo)

    bd = lambda x: _block_diag(x, bd_mask)
    cols = lambda grp: slice(grp * gw, (grp + 1) * gw)
    load = lambda ref: [ref[bi, :, cols(grp)] for bi, grp in units]
    r, k, v, na, bv, lw = (load(ref) for ref in (r_ref, k_ref, v_ref, na_ref, b_ref, lw_ref))
    state = [state_ref[bi * KV_GROUPS + grp] for bi, grp in units]

    cum = every(cumsum_rows, lw)
    p_incl = every(jnp.exp, cum)
    inv_p = every(lambda x: jnp.exp(-x), cum)
    b_t = every(lambda x, s: (x * s).astype(BF16), bv, inv_p)
    k_t = every(lambda x, s: (x * s).astype(BF16), k, inv_p)
    x2 = every(lambda a, cu, l, rr_, p: jnp.concatenate([a * jnp.exp(cu - l), rr_ * p], axis=0).astype(BF16),
               na, cum, lw, r, p_incl)

    g_b = every(lambda x, y: _dot_nt(x, bd(y)), x2, b_t)
    g_k = every(lambda x, y: _dot_nt(x, bd(y)), x2, k_t)
    xh = every(lambda x, s: _dot_nt(x, s.astype(BF16)), x2, state)
    l_ab = every(lambda g: jnp.where(strict, g[:c], 0.0), g_b)
    m_rb = every(lambda g: jnp.where(incl, g[c:], 0.0).astype(BF16), g_b)
    l_ak = every(lambda g: jnp.where(strict, g[:c], 0.0).astype(BF16), g_k)
    m_rk = every(lambda g: jnp.where(incl, g[c:], 0.0).astype(BF16), g_k)

    power = every(lambda l: jnp.where(same_block[INV_BASE], l, 0.0), l_ab)
    t_inv = every(lambda p: eye + p, power)
    for _ in range(int(np.log2(INV_BASE)) - 1):
        power = every(lambda p: _dot(p.astype(BF16), bd(p)), power)
        t_inv = every(lambda t, p: t + _dot(t.astype(BF16), bd(p)), t_inv, power)
    size = 2 * INV_BASE
    while size <= c:
        off_mask = same_block[size] & ~same_block[size // 2]
        half = every(lambda t, l: _dot(t.astype(BF16), bd(jnp.where(off_mask, l, 0.0))), t_inv, l_ab)
        t_inv = every(lambda t, hf: t + _dot(hf.astype(BF16), bd(t)), t_inv, half)
        size *= 2

    v_bd = every(bd, v)
    rhs = every(lambda x, l, vb: x[:c] + _dot(l, vb), xh, l_ak, v_bd)
    u = every(lambda t, rh: _dot(t.astype(BF16), bd(rh)), t_inv, rhs)
    out = every(lambda x, mb, uu, mk, vb: x[c:] + _dot(mb, bd(uu)) + _dot(mk, vb), xh, m_rb, u, m_rk, v_bd)
    delta = every(lambda uu, vv, b, kk_: lax.dot_general(jnp.concatenate([uu.astype(BF16), vv.astype(BF16)], axis=0),
                                                         jnp.concatenate([b, kk_], axis=0), _TN,
                                                         preferred_element_type=F32), u, v, b_t, k_t)
    mean = every(lambda o: _split_dot(o, ones_bd) * (1.0 / HEAD_DIM), out)
    dev = every(lambda o, m: o - m, out, mean)
    var = every(lambda dv: _split_dot(dv * dv, ones_bd) * (1.0 / HEAD_DIM), dev)
    for i, (bi, grp) in enumerate(units):
        state_ref[bi * KV_GROUPS + grp] = (state[i] + jnp.where(bd_mask, delta[i], 0.0)) * p_incl[i][c - 1:c, :]
        y = dev[i] * lax.rsqrt(var[i] + RWKV_GN_EPS) * lng_ref[:, cols(grp)] + lnb_ref[:, cols(grp)]
        o_ref[bi, :, cols(grp)] = ((y + bonus_ref[bi, :, cols(grp)]) * g_ref[bi, :, cols(grp)]).astype(o_ref.dtype)


def _rwkv_scan(r, k, v, na, b, lw, g, bonus, ln_g, ln_b, batch, seq):
    c = SCAN_CHUNK
    n_chunks = seq // c
    pw = PRIMARY_WIDTH
    assert GROUP_HEADS * c == GROUP_WIDTH, "block-diagonal packing needs a 64-token chunk"
    blk = pl.BlockSpec((batch, c, pw), lambda ci: (0, ci, 0))
    const = pl.BlockSpec((1, pw), lambda ci: (0, 0))
    per_batch = lambda a: a.reshape(batch, seq, pw)
    out = pl.pallas_call(
        _rwkv_scan_kernel,
        grid=(n_chunks,),
        in_specs=[blk] * 8 + [const, const],
        out_specs=blk,
        out_shape=jax.ShapeDtypeStruct((batch, seq, pw), BF16),
        scratch_shapes=[pltpu.VMEM((batch * KV_GROUPS, GROUP_WIDTH, GROUP_WIDTH), F32)],
        compiler_params=_params("arbitrary"),
        name="rwkv_scan",
    )(*(per_batch(a) for a in (r, k, v, na, b, lw, g, bonus)), ln_g.reshape(1, pw), ln_b.reshape(1, pw))
    return out.reshape(batch * seq, pw)


def _rwkv_layer(hn, w_in, mu, w0, w2, a0, a2, g2, k_k, k_a, r_k, ln_g, ln_b, batch, seq):
    q_mem = _proj(hn, w_in[:, RWKV_SHIFT_W:], tn=MEM_WIDTH, out_dtype=BF16, name="rwkv_proj_mem")
    r, k, v, na, b, lw, g, bonus = _rwkv_prep(hn, w_in[:, :RWKV_SHIFT_W], mu, w0, w2, a0, a2, g2, k_k, k_a,
                                              r_k.reshape(-1), seq)
    prim = _rwkv_scan(r, k, v, na, b, lw, g, bonus, ln_g, ln_b, batch, seq)
    return prim, q_mem


def kernel(x, mem, positions, mem_norm_g, w_mem_kv, pre_mix_g, post_mix_g, pre_ffn_g, post_ffn_g, w_out, w_ffn_in, w_ffn_out, nsa_w_in, nsa_gate_b, nsa_cmp_pos_k, nsa_cmp_pos_v, nsa_cmp_k_w1, nsa_cmp_k_w2, nsa_cmp_v_w1, nsa_cmp_v_w2, rwkv_w_in, rwkv_mu, rwkv_w0, rwkv_w2, rwkv_a0, rwkv_a2, rwkv_g2, rwkv_k_k, rwkv_k_a, rwkv_r_k, rwkv_ln_g, rwkv_ln_b):
    batch, seq, d = x.shape
    n_mem = mem.shape[1]
    depth = pre_mix_g.shape[0]
    mkv = _proj(_norm_cast(mem.reshape(batch * n_mem, d), mem_norm_g), w_mem_kv, tn=w_mem_kv.shape[1],
                out_dtype=BF16, name="mem_kv")
    mem_k = mkv[:, :MEM_WIDTH].reshape(batch, n_mem, MEM_WIDTH)
    mem_v = mkv[:, MEM_WIDTH:].reshape(batch, n_mem, MEM_WIDTH)
    h = x.reshape(batch * seq, d)
    hn = _norm_cast(h, pre_mix_g[0])
    for i in range(depth):
        j = i // 2
        if i % 2 == 0:
            prim, q_mem = _nsa_layer(hn, nsa_w_in[j], nsa_gate_b[j], nsa_cmp_pos_k[j], nsa_cmp_pos_v[j],
                                     nsa_cmp_k_w1[j], nsa_cmp_k_w2[j], nsa_cmp_v_w1[j], nsa_cmp_v_w2[j],
                                     positions, batch, seq)
        else:
            prim, q_mem = _rwkv_layer(hn, rwkv_w_in[j], rwkv_mu[j], rwkv_w0[j], rwkv_w2[j], rwkv_a0[j],
                                      rwkv_a2[j], rwkv_g2[j], rwkv_k_k[j], rwkv_k_a[j], rwkv_r_k[j], rwkv_ln_g[j],
                                      rwkv_ln_b[j], batch, seq)
        mo = _mem_attn(q_mem, mem_k, mem_v, seq)
        h = _out_proj(prim, mo, w_out[i], post_mix_g[i], h)
        if i + 1 < depth:
            h, hn = _ffn(h, pre_ffn_g[i], w_ffn_in[i], w_ffn_out[i], post_ffn_g[i], next_g=pre_mix_g[i + 1])
        else:
            h = _ffn(h, pre_ffn_g[i], w_ffn_in[i], w_ffn_out[i], post_ffn_g[i])
    return h.reshape(batch, seq, d)
```

```python
import functools

import numpy as np
import jax
import jax.numpy as jnp
from jax import lax
from jax.experimental import pallas as pl
from jax.experimental.pallas import tpu as pltpu

F32 = jnp.float32
BF16 = jnp.bfloat16

HEAD_DIM = 64
ROT_DIM = HEAD_DIM // 4
ROT_HALF = ROT_DIM // 2
ROPE_THETA = 500000.0
MEM_HEADS = 4
MEM_WIDTH = MEM_HEADS * HEAD_DIM
KV_GROUPS = 3
GROUP_HEADS = 4
GROUP_WIDTH = GROUP_HEADS * HEAD_DIM
PRIMARY_WIDTH = KV_GROUPS * GROUP_WIDTH
KV_WIDTH = KV_GROUPS * HEAD_DIM
CMP_BLOCK = 32
CMP_STRIDE = 16
SEL_BLOCK = 64
SEL_TOPK = 16
WINDOW = 512
DECAY_LORA = 64
AAA_LORA = 64
GATE_LORA = 128
RWKV_SHIFT_W = 3 * PRIMARY_WIDTH + DECAY_LORA + AAA_LORA + GATE_LORA
RWKV_GN_EPS = HEAD_DIM * 1e-5
NORM_EPS = 1e-6
NEG = -1e30
FORCE = 1e6
ATTN_SCALE = HEAD_DIM ** -0.5
LOG2E = float(np.log2(np.e))
SAFE_LOG2 = 80.0

LANES = 128
SUBLANES = 8
VMEM_LIMIT_BYTES = 48 * 1024 * 1024

ROW_TILE = 512
ATTN_TILE = 256
SCAN_CHUNK = 64
SCAN_CHUNKS_PER_STEP = 2
INV_BASE = 8
SEL_SPAN = 2
SEL_UNROLL = 4
WIN_TILES = WINDOW // ATTN_TILE + 1
N_FORCED = 3
ACC_ROWS = HEAD_DIM + 16
PREP_TILE = 256

_NT = (((1,), (1,)), ((), ()))
_TN = (((0,), (0,)), ((), ()))


def _params(*sem):
    return pltpu.CompilerParams(dimension_semantics=sem, vmem_limit_bytes=VMEM_LIMIT_BYTES)


def _rms(x, g):
    return x * lax.rsqrt(jnp.mean(x * x, axis=-1, keepdims=True) + NORM_EPS) * g


def _dot(a, b):
    return jnp.dot(a, b, preferred_element_type=F32)


def _dot_nt(a, b):
    return lax.dot_general(a, b, _NT, preferred_element_type=F32)


def _norm_cast_kernel(x_ref, g_ref, o_ref):
    o_ref[...] = _rms(x_ref[...], g_ref[...]).astype(o_ref.dtype)


def _norm_cast(x, g):
    t, d = x.shape
    tm = min(ROW_TILE, t)
    return pl.pallas_call(
        _norm_cast_kernel,
        grid=(t // tm,),
        in_specs=[pl.BlockSpec((tm, d), lambda i: (i, 0)), pl.BlockSpec((1, d), lambda i: (0, 0))],
        out_specs=pl.BlockSpec((tm, d), lambda i: (i, 0)),
        out_shape=jax.ShapeDtypeStruct((t, d), BF16),
        compiler_params=_params("parallel"),
        name="norm_cast",
    )(x, g.reshape(1, d))


def _proj_kernel(x_ref, w_ref, o_ref):
    o_ref[...] = _dot(x_ref[...], w_ref[...]).astype(o_ref.dtype)


def _proj_rope_kernel(x_ref, w_ref, c_ref, sp_ref, sm_ref, scale_ref, o_ref):
    y = _dot(x_ref[...], w_ref[...])
    width = y.shape[1]
    reps = width // LANES
    cos = jnp.concatenate([c_ref[...]] * reps, axis=1)
    sin_hi = jnp.concatenate([sp_ref[...]] * reps, axis=1)
    sin_lo = jnp.concatenate([sm_ref[...]] * reps, axis=1)
    y = y * cos + pltpu.roll(y, ROT_HALF, 1) * sin_hi + pltpu.roll(y, width - ROT_HALF, 1) * sin_lo
    o_ref[...] = (y * scale_ref[...]).astype(o_ref.dtype)


def _proj_t_kernel(x_ref, wt_ref, b_ref, o_ref, *, gate):
    y = _dot_nt(wt_ref[...], x_ref[...])
    if gate:
        y = jax.nn.sigmoid(y + b_ref[...])
    o_ref[0] = y.astype(o_ref.dtype)


def _proj(x, w, *, tn, out_dtype, rope=None, name):
    t, d = x.shape
    n = w.shape[1]
    tm = min(ROW_TILE, t)
    in_specs = [
        pl.BlockSpec((tm, d), lambda i, j: (i, 0)),
        pl.BlockSpec((d, tn), lambda i, j: (0, j)),
    ]
    args = [x, w.astype(BF16)]
    if rope is None:
        body = _proj_kernel
    else:
        body = _proj_rope_kernel
        cos, sin_hi, sin_lo, scale = rope
        in_specs += [pl.BlockSpec((tm, LANES), lambda i, j: (i, 0))] * 3
        in_specs += [pl.BlockSpec((1, tn), lambda i, j: (0, j))]
        args += [cos, sin_hi, sin_lo, scale]
    return pl.pallas_call(
        body,
        grid=(t // tm, n // tn),
        in_specs=in_specs,
        out_specs=pl.BlockSpec((tm, tn), lambda i, j: (i, j)),
        out_shape=jax.ShapeDtypeStruct((t, n), out_dtype),
        compiler_params=_params("parallel", "parallel"),
        name=name,
    )(*args)


def _proj_t(x, wt, bias, *, tm, tn, out_dtype, gate, name):
    t, d = x.shape
    n = wt.shape[0]
    return pl.pallas_call(
        functools.partial(_proj_t_kernel, gate=gate),
        grid=(t // tm, n // tn),
        in_specs=[
            pl.BlockSpec((tm, d), lambda i, j: (i, 0)),
            pl.BlockSpec((tn, d), lambda i, j: (j, 0)),
            pl.BlockSpec((tn, 1), lambda i, j: (j, 0)),
        ],
        out_specs=pl.BlockSpec((1, tn, tm), lambda i, j: (i, j, 0)),
        out_shape=jax.ShapeDtypeStruct((t // tm, n, tm), out_dtype),
        compiler_params=_params("parallel", "parallel"),
        name=name,
    )(x, wt.astype(BF16), bias)


def _outproj_kernel(a_ref, b_ref, wa_ref, wb_ref, g_ref, h_ref, o_ref):
    y = _dot(a_ref[...], wa_ref[...]) + _dot(b_ref[...], wb_ref[...])
    o_ref[...] = h_ref[...] + _rms(y, g_ref[...])


def _out_proj(prim, mo, w_out, g, h):
    t, d = h.shape
    tm = min(ROW_TILE, t)
    pw = prim.shape[1]
    return pl.pallas_call(
        _outproj_kernel,
        grid=(t // tm,),
        in_specs=[
            pl.BlockSpec((tm, pw), lambda i: (i, 0)),
            pl.BlockSpec((tm, MEM_WIDTH), lambda i: (i, 0)),
            pl.BlockSpec((pw, d), lambda i: (0, 0)),
            pl.BlockSpec((MEM_WIDTH, d), lambda i: (0, 0)),
            pl.BlockSpec((1, d), lambda i: (0, 0)),
            pl.BlockSpec((tm, d), lambda i: (i, 0)),
        ],
        out_specs=pl.BlockSpec((tm, d), lambda i: (i, 0)),
        out_shape=jax.ShapeDtypeStruct((t, d), F32),
        compiler_params=_params("parallel"),
        name="out_proj",
    )(prim, mo, w_out[:pw].astype(BF16), w_out[pw:].astype(BF16), g.reshape(1, d), h)


def _ffn_kernel(h_ref, g1_ref, wg_ref, wu_ref, wo_ref, g2_ref, *rest, feeds_next):
    if feeds_next:
        g3_ref, o_ref, next_ref, hn_ref, acc_ref = rest
    else:
        o_ref, hn_ref, acc_ref = rest
    j = pl.program_id(1)

    @pl.when(j == 0)
    def _():
        hn_ref[...] = _rms(h_ref[...], g1_ref[...]).astype(BF16)
        acc_ref[...] = jnp.zeros_like(acc_ref)

    hn = hn_ref[...]
    gate = _dot(hn, wg_ref[...])
    up = _dot(hn, wu_ref[...])
    act = (jax.nn.silu(gate) * up).astype(BF16)
    acc_ref[...] += _dot(act, wo_ref[...])

    @pl.when(j == pl.num_programs(1) - 1)
    def _():
        out = h_ref[...] + _rms(acc_ref[...], g2_ref[...])
        o_ref[...] = out
        if feeds_next:
            next_ref[...] = _rms(out, g3_ref[...]).astype(next_ref.dtype)


def _ffn_chunk(hidden):
    units = hidden // LANES
    for parts in range(2, units + 1):
        if units % parts == 0:
            return (units // parts) * LANES
    return hidden


def _ffn(h, g1, w_in, w_out, g2, next_g=None):
    t, d = h.shape
    hidden = w_out.shape[0]
    th = _ffn_chunk(hidden)
    nh = hidden // th
    tm = min(ROW_TILE, t)
    w_in = w_in.astype(BF16)
    row = pl.BlockSpec((tm, d), lambda i, j: (i, 0))
    gain = pl.BlockSpec((1, d), lambda i, j: (0, 0))
    feeds_next = next_g is not None
    in_specs = [row, gain,
                pl.BlockSpec((d, th), lambda i, j: (0, j)),
                pl.BlockSpec((d, th), lambda i, j: (0, j + nh)),
                pl.BlockSpec((th, d), lambda i, j: (j, 0)),
                gain]
    args = [h, g1.reshape(1, d), w_in, w_in, w_out.astype(BF16), g2.reshape(1, d)]
    out_specs, out_shape = row, jax.ShapeDtypeStruct((t, d), F32)
    if feeds_next:
        in_specs.append(gain)
        args.append(next_g.reshape(1, d))
        out_specs, out_shape = [row, row], [out_shape, jax.ShapeDtypeStruct((t, d), BF16)]
    return pl.pallas_call(
        functools.partial(_ffn_kernel, feeds_next=feeds_next),
        grid=(t // tm, nh),
        in_specs=in_specs,
        out_specs=out_specs,
        out_shape=out_shape,
        scratch_shapes=[pltpu.VMEM((tm, d), BF16), pltpu.VMEM((tm, d), F32)],
        compiler_params=_params("parallel", "arbitrary"),
        name="ffn",
    )(*args)


def _mem_attn_kernel(q_ref, mk_ref, mv_ref, o_ref):
    q = q_ref[...]
    mk = mk_ref[0]
    mv = mv_ref[0]
    head_of_lane = lax.broadcasted_iota(jnp.int32, mk.shape, 1) // HEAD_DIM
    acc = jnp.zeros(q.shape, F32)
    for h in range(MEM_HEADS):
        s = _dot_nt(q, jnp.where(head_of_lane == h, mk, 0)) * ATTN_SCALE
        e = jnp.exp(s - jnp.max(s, axis=-1, keepdims=True))
        p = e / jnp.sum(e, axis=-1, keepdims=True)
        acc = acc + _dot(p.astype(BF16), jnp.where(head_of_lane == h, mv, 0))
    o_ref[...] = acc.astype(o_ref.dtype)


def _mem_attn(q_mem, mem_k, mem_v, seq):
    t = q_mem.shape[0]
    m = mem_k.shape[1]
    tm = min(ROW_TILE, seq)
    per_seq = seq // tm
    return pl.pallas_call(
        _mem_attn_kernel,
        grid=(t // tm,),
        in_specs=[
            pl.BlockSpec((tm, MEM_WIDTH), lambda i: (i, 0)),
            pl.BlockSpec((1, m, MEM_WIDTH), lambda i: (i // per_seq, 0, 0)),
            pl.BlockSpec((1, m, MEM_WIDTH), lambda i: (i // per_seq, 0, 0)),
        ],
        out_specs=pl.BlockSpec((tm, MEM_WIDTH), lambda i: (i, 0)),
        out_shape=jax.ShapeDtypeStruct((t, MEM_WIDTH), BF16),
        compiler_params=_params("parallel"),
        name="mem_attn",
    )(q_mem, mem_k, mem_v)


def _compress_kernel(xk_ref, xv_ref, pk_ref, pv_ref, k1a_ref, k1b_ref, k2_ref, v1a_ref, v1b_ref, v2t_ref,
                     c_ref, s_ref, kc_ref, vct_ref):
    def hidden(x_ref, pos_ref, wa_ref, wb_ref):
        x = x_ref[0, 0]
        n = x.shape[0]
        first = _dot((x + pos_ref[0:1, :]).astype(BF16), wa_ref[...])
        second = _dot((x + pos_ref[1:2, :]).astype(BF16), wb_ref[...])
        return jax.nn.gelu(first + pltpu.roll(second, n - 1, 0)).astype(BF16)

    hk = hidden(xk_ref, pk_ref, k1a_ref, k1b_ref)
    both = _dot(hk, k2_ref[...])
    kc_ref[0, 0] = (both[:, :LANES] * c_ref[0] + both[:, LANES:] * s_ref[0]).astype(kc_ref.dtype)
    hv = hidden(xv_ref, pv_ref, v1a_ref, v1b_ref)
    vct_ref[0, 0] = _dot_nt(v2t_ref[...], hv).astype(vct_ref.dtype)


def _rope_partner_cols(w):
    d = np.arange(w.shape[1]) % HEAD_DIM
    src = np.where(d < ROT_HALF, np.arange(w.shape[1]) + ROT_HALF, np.arange(w.shape[1]) - ROT_HALF)
    src = np.clip(src, 0, w.shape[1] - 1)
    sign = np.where(d < ROT_HALF, -1.0, np.where(d < ROT_DIM, 1.0, 0.0)).astype(np.float32)
    return w[:, src] * sign


def _compress(raw_k, raw_v, pos_k, pos_v, k_w1, k_w2, v_w1, v_w2, cos_c, sin_c, batch, seq):
    n_chunk = seq // CMP_STRIDE
    feat = CMP_STRIDE * HEAD_DIM
    hid = k_w1.shape[1]

    def chunks(raw):
        x = raw.reshape(batch, n_chunk, CMP_STRIDE, KV_GROUPS, HEAD_DIM)
        return x.transpose(0, 3, 1, 2, 4).reshape(batch, KV_GROUPS, n_chunk, feat)

    zeros = jnp.zeros((hid, LANES - HEAD_DIM), F32)
    k2 = jnp.concatenate([k_w2, zeros, _rope_partner_cols(k_w2), zeros], axis=1).astype(BF16)
    x_spec = pl.BlockSpec((1, 1, n_chunk, feat), lambda b, g: (b, g, 0, 0))
    pos_spec = pl.BlockSpec((2, feat), lambda b, g: (0, 0))
    w1_spec = pl.BlockSpec((feat, hid), lambda b, g: (0, 0))
    tab_spec = pl.BlockSpec((1, n_chunk, LANES), lambda b, g: (b, 0, 0))
    return pl.pallas_call(
        _compress_kernel,
        grid=(batch, KV_GROUPS),
        in_specs=[x_spec, x_spec, pos_spec, pos_spec, w1_spec, w1_spec,
                  pl.BlockSpec((hid, 2 * LANES), lambda b, g: (0, 0)),
                  w1_spec, w1_spec,
                  pl.BlockSpec((HEAD_DIM, hid), lambda b, g: (0, 0)),
                  tab_spec, tab_spec],
        out_specs=[pl.BlockSpec((1, 1, n_chunk, LANES), lambda b, g: (b, g, 0, 0)),
                   pl.BlockSpec((1, 1, HEAD_DIM, n_chunk), lambda b, g: (b, g, 0, 0))],
        out_shape=[jax.ShapeDtypeStruct((batch, KV_GROUPS, n_chunk, LANES), BF16),
                   jax.ShapeDtypeStruct((batch, KV_GROUPS, HEAD_DIM, n_chunk), BF16)],
        compiler_params=_params("parallel", "parallel"),
        name="nsa_compress",
    )(chunks(raw_k), chunks(raw_v), pos_k.reshape(2, feat), pos_v.reshape(2, feat),
      k_w1[:feat].astype(BF16), k_w1[feat:].astype(BF16), k2,
      v_w1[:feat].astype(BF16), v_w1[feat:].astype(BF16), v_w2.T.astype(BF16), cos_c, sin_c)


def _split_dot_left(w_bf16, x):
    hi = x.astype(BF16)
    rem = x - hi.astype(F32)
    mid = rem.astype(BF16)
    lo = (rem - mid.astype(F32)).astype(BF16)
    return _dot(w_bf16, hi) + _dot(w_bf16, mid) + _dot(w_bf16, lo)


def _nsa_attn_kernel(q_ref, kk_ref, hot_ref, vt_ref, kc_ref, vct_ref, tap_ref, winb_ref, gate_ref, o_ref,
                     qx_ref, m_ref, acc_ref, kmax_ref):
    tq = q_ref.shape[0]
    qt = pl.program_id(2)
    t0 = qt * tq
    heads = range(GROUP_HEADS)

    q2 = q_ref[...]
    qs = jnp.concatenate([q2[:, h * LANES:(h + 1) * LANES] for h in heads], axis=0)
    lane = lax.broadcasted_iota(jnp.int32, qs.shape, 1)
    q_sel = jnp.where(lane < HEAD_DIM, qs, 0)
    q_win = jnp.where(lane >= HEAD_DIM, qs, 0)

    col_t = t0 + lax.broadcasted_iota(jnp.int32, (1, tq), 1)

    @pl.when(qt == 0)
    def _():
        half = lax.broadcasted_iota(jnp.int32, (1, LANES), 1) < HEAD_DIM
        k_abs = jnp.abs(kk_ref[...].astype(F32))
        kc_abs = jnp.abs(kc_ref[0, 0].astype(F32))
        row_sums = [jnp.sum(jnp.where(half, k_abs, 0.0), axis=1, keepdims=True),
                    jnp.sum(jnp.where(half, 0.0, k_abs), axis=1, keepdims=True),
                    jnp.sum(kc_abs, axis=1, keepdims=True)]
        kmax_ref[0] = functools.reduce(jnp.maximum, [jnp.max(s) for s in row_sums])
    bounded = jnp.max(jnp.abs(qs.astype(F32))) * kmax_ref[0] <= SAFE_LOG2

    @pl.when(bounded)
    def _():
        _nsa_attend(False, q_sel, q_win, col_t, qt, kk_ref, hot_ref, vt_ref, kc_ref, vct_ref, tap_ref, winb_ref,
                    gate_ref, o_ref, qx_ref, m_ref, acc_ref)

    @pl.when(jnp.logical_not(bounded))
    def _():
        _nsa_attend(True, q_sel, q_win, col_t, qt, kk_ref, hot_ref, vt_ref, kc_ref, vct_ref, tap_ref, winb_ref,
                    gate_ref, o_ref, qx_ref, m_ref, acc_ref)


def _nsa_attend(stabilized, q_sel, q_win, col_t, qt, kk_ref, hot_ref, vt_ref, kc_ref, vct_ref, tap_ref, winb_ref,
                gate_ref, o_ref, qx_ref, m_ref, acc_ref):
    tq = col_t.shape[1]
    tk = tq
    heads = range(GROUP_HEADS)
    head_rows = lambda h: slice(h * tq, (h + 1) * tq)

    def weights(scores):
        if not stabilized:
            return jnp.exp2(scores), None
        top = jnp.max(scores, axis=0, keepdims=True)
        return jnp.exp2(scores - top), top > 0.5 * NEG

    n_cmp = kc_ref.shape[2]
    blk_end = lax.broadcasted_iota(jnp.int32, (n_cmp, 1), 0) * CMP_STRIDE + (CMP_BLOCK - 1)
    valid = blk_end <= col_t
    cmp_scores = [_dot_nt(kc_ref[0, 0], q_sel[head_rows(h)]) for h in heads]
    o_cmp = []
    p_sum = jnp.zeros((n_cmp, tq), F32)
    for h in heads:
        e, live = weights(jnp.where(valid, cmp_scores[h], NEG))
        total = jnp.sum(e, axis=0, keepdims=True)
        live = total > 0.0 if live is None else live
        p = e * jnp.where(live, 1.0 / total, 0.0)
        o_cmp.append(_dot(vct_ref[0, 0], p.astype(BF16)))
        p_sum = p_sum + p
    imp = _split_dot_left(tap_ref[...], p_sum)

    n_sel = imp.shape[0]
    blk = lax.broadcasted_iota(jnp.int32, (n_sel, tq), 0)
    blk_f = blk.astype(F32)
    cur = col_t // SEL_BLOCK
    forced = (blk == 0) | (blk == cur) | (blk == cur - 1)
    work = jnp.where(forced, -jnp.inf, jnp.where(blk <= cur, imp, -FORCE))
    for _ in range(max(min(SEL_TOPK, n_sel) - N_FORCED, 0)):
        best = jnp.max(work, axis=0, keepdims=True)
        hit = blk_f == jnp.min(jnp.where(work == best, blk_f, float(n_sel)), axis=0, keepdims=True)
        work = jnp.where(hit, -jnp.inf, work)
    chosen = jnp.where(work == -jnp.inf, 1.0, 0.0)
    if n_sel < LANES:
        chosen = jnp.concatenate([chosen, jnp.zeros((LANES - n_sel, tq), F32)], axis=0)
    block_bias = ((chosen - 1.0) * (-NEG)).T.astype(BF16)
    qx_ref[:, 0:LANES] = q_sel
    qx_ref[:, LANES:2 * LANES] = jnp.concatenate([block_bias] * GROUP_HEADS, axis=0)

    def key_pos(kt, n):
        return kt * tk + lax.broadcasted_iota(jnp.int32, (n * tk, 1), 0)

    def key_rows(ref, kt, n):
        return ref[pl.ds(pl.multiple_of(kt * tk, tk), n * tk), :]

    def value_rows(kt, n, first_row):
        vals = jnp.concatenate([vt_ref[kt + i, first_row:first_row + HEAD_DIM, :] for i in range(n)], axis=1)
        return jnp.concatenate([vals, jnp.ones((ACC_ROWS - HEAD_DIM, n * tk), BF16)], axis=0)

    def normalized(acc):
        return acc[0:HEAD_DIM] / acc[HEAD_DIM:HEAD_DIM + 1]

    win_start = jnp.maximum(qt - (WIN_TILES - 1), 0)
    win_bias = winb_ref[qt - win_start]
    win_keys = key_rows(kk_ref, win_start, WIN_TILES)
    win_vals = value_rows(win_start, WIN_TILES, HEAD_DIM)
    win_scores = [_dot_nt(win_keys, q_win[head_rows(h)]) for h in heads]
    o_win = [normalized(_dot(win_vals, weights(win_scores[h] + win_bias)[0].astype(BF16))) for h in heads]

    acc_ref[...] = jnp.zeros(acc_ref.shape, F32)
    full_spans = qt // SEL_SPAN

    def score_span(i):
        kt = i * SEL_SPAN
        keys = jnp.concatenate([key_rows(kk_ref, kt, SEL_SPAN), key_rows(hot_ref, kt, SEL_SPAN)], axis=1)
        return tuple(_dot_nt(keys, qx_ref[head_rows(h), :]) for h in heads)

    def causal_scores(i, scores):
        return jnp.where(key_pos(i * SEL_SPAN, SEL_SPAN) <= col_t, scores, NEG)

    if not stabilized:
        def plain_spans(first, count, causal_last):
            scores = [score_span(first + s) for s in range(count)]
            for s in range(count):
                v_ext = value_rows((first + s) * SEL_SPAN, SEL_SPAN, 0)
                for h in heads:
                    masked = causal_last and s == count - 1
                    e = jnp.exp2(causal_scores(first + s, scores[s][h]) if masked else scores[s][h]).astype(BF16)
                    acc_ref[:, head_rows(h)] += _dot(v_ext, e)

        def plain_step(i, carry):
            plain_spans(i * SEL_UNROLL, SEL_UNROLL, False)
            return carry

        whole = full_spans // SEL_UNROLL
        lax.fori_loop(0, whole, plain_step, 0)
        for left in range(SEL_UNROLL):
            @pl.when(full_spans - whole * SEL_UNROLL == left)
            def _():
                plain_spans(whole * SEL_UNROLL, left + 1, True)
    else:
        m_ref[...] = jnp.full(m_ref.shape, NEG, F32)

        def flash_span(i, span_scores, causal):
            v_ext = value_rows(i * SEL_SPAN, SEL_SPAN, 0)
            for h in heads:
                cols = head_rows(h)
                scores = causal_scores(i, span_scores[h]) if causal else span_scores[h]
                m_old = m_ref[:, cols]
                m_new = jnp.maximum(m_old, jnp.max(scores, axis=0, keepdims=True))
                e = jnp.exp2(scores - m_new).astype(BF16)
                acc_ref[:, cols] = jnp.exp2(m_old - m_new) * acc_ref[:, cols] + _dot(v_ext, e)
                m_ref[:, cols] = m_new

        def sel_step(i, span_scores):
            following = score_span(i + 1)
            flash_span(i, span_scores, False)
            return following

        last_scores = lax.fori_loop(0, full_spans, sel_step, score_span(0))
        flash_span(full_spans, last_scores, True)

    gates = gate_ref[...]
    outs = []
    for h in heads:
        outs.append(gates[3 * h:3 * h + 1, :] * o_cmp[h]
                    + gates[3 * h + 1:3 * h + 2, :] * normalized(acc_ref[:, head_rows(h)])
                    + gates[3 * h + 2:3 * h + 3, :] * o_win[h])
    o_ref[...] = jnp.concatenate(outs, axis=0).T.astype(o_ref.dtype)


def _tap_matrix(n_sel, n_chunk):
    ratio = SEL_BLOCK // CMP_STRIDE
    tap = np.zeros((n_sel, n_chunk), np.float32)
    n_cmp = n_chunk - (CMP_BLOCK // CMP_STRIDE - 1)
    for j in range(n_sel):
        for n in range(n_cmp):
            lo = max(n * CMP_STRIDE, j * SEL_BLOCK)
            hi = min(n * CMP_STRIDE + CMP_BLOCK, (j + 1) * SEL_BLOCK)
            if hi > lo:
                tap[j, n] = (hi - lo) / CMP_STRIDE
    return jnp.asarray(tap, BF16)


def _nsa_attention(qk, vt, gates, kc, vct, batch, seq):
    tq = ATTN_TILE
    n_tiles = seq // tq
    n_chunk = kc.shape[2]
    n_sel = seq // SEL_BLOCK
    q_blocks = GROUP_HEADS * LANES
    k_col0 = KV_GROUPS * q_blocks // LANES
    vt5 = vt.reshape(batch, n_tiles, KV_GROUPS, 2 * HEAD_DIM, tq)
    gates5 = gates.reshape(batch, n_tiles, KV_GROUPS, 16, tq)
    width = GROUP_HEADS * tq
    assert n_sel <= LANES, "block one-hot is one lane tile wide"
    assert n_tiles % SEL_SPAN == 0 and n_tiles >= WIN_TILES
    hot = jnp.asarray((np.arange(seq)[:, None] // SEL_BLOCK) == np.arange(LANES)[None, :], BF16)
    dist = (np.arange(WIN_TILES)[:, None, None] * tq + np.arange(tq)[None, None, :]
            - np.arange(WIN_TILES * tq)[None, :, None])
    win_bias = jnp.asarray(np.where((dist >= 0) & (dist < WINDOW), 0.0, NEG), F32)
    return pl.pallas_call(
        _nsa_attn_kernel,
        grid=(batch, KV_GROUPS, n_tiles),
        in_specs=[
            pl.BlockSpec((tq, q_blocks), lambda b, g, i: (b * n_tiles + i, g)),
            pl.BlockSpec((seq, LANES), lambda b, g, i: (b, k_col0 + g)),
            pl.BlockSpec((seq, LANES), lambda b, g, i: (0, 0)),
            pl.BlockSpec((None, n_tiles, None, 2 * HEAD_DIM, tq), lambda b, g, i: (b, 0, g, 0, 0)),
            pl.BlockSpec((1, 1, n_chunk, LANES), lambda b, g, i: (b, g, 0, 0)),
            pl.BlockSpec((1, 1, HEAD_DIM, n_chunk), lambda b, g, i: (b, g, 0, 0)),
            pl.BlockSpec((n_sel, n_chunk), lambda b, g, i: (0, 0)),
            pl.BlockSpec((WIN_TILES, WIN_TILES * tq, tq), lambda b, g, i: (0, 0, 0)),
            pl.BlockSpec((None, None, None, 16, tq), lambda b, g, i: (b, i, g, 0, 0)),
        ],
        out_specs=pl.BlockSpec((tq, GROUP_WIDTH), lambda b, g, i: (b * n_tiles + i, g)),
        out_shape=jax.ShapeDtypeStruct((batch * seq, PRIMARY_WIDTH), BF16),
        scratch_shapes=[
            pltpu.VMEM((width, 2 * LANES), BF16),
            pltpu.VMEM((1, width), F32),
            pltpu.VMEM((ACC_ROWS, width), F32),
            pltpu.SMEM((1,), F32),
        ],
        compiler_params=_params("parallel", "parallel", "arbitrary"),
        name="nsa_attention",
    )(qk, qk, hot, vt5, kc, vct, _tap_matrix(n_sel, n_chunk), win_bias, gates5)


def _rope_tables(positions):
    inv = ROPE_THETA ** (-jnp.arange(0, ROT_DIM, 2, dtype=F32) / ROT_DIM)
    ang = positions.astype(F32)[..., None] * inv
    cos, sin = lax.optimization_barrier((jnp.cos(ang), jnp.sin(ang)))
    reps = (1,) * (ang.ndim - 1) + (LANES // ROT_HALF,)
    cos_t, sin_t = jnp.tile(cos, reps), jnp.tile(sin, reps)
    dim = jnp.arange(LANES) % HEAD_DIM
    lo, hi = dim < ROT_HALF, (dim >= ROT_HALF) & (dim < ROT_DIM)
    return (jnp.where(lo | hi, cos_t, 1.0), jnp.where(hi, sin_t, 0.0), jnp.where(lo, -sin_t, 0.0),
            jnp.where(lo | hi, sin_t, 0.0))


def _nsa_layer(hn, w_in, gate_b, pos_k, pos_v, k_w1, k_w2, v_w1, v_w2, positions, batch, seq):
    t, d = hn.shape
    pw, kw = PRIMARY_WIDTH, KV_WIDTH
    offs = np.cumsum([0, pw] + [kw] * 6 + [3 * KV_GROUPS * GROUP_HEADS, MEM_WIDTH])
    col = lambda i: w_in[:, offs[i]:offs[i + 1]]
    w_q, w_kc, w_vc, w_ks, w_vs, w_kw, w_vw, w_gl, w_qm = [col(i) for i in range(9)]

    wq_h = w_q.reshape(d, KV_GROUPS * GROUP_HEADS, 1, HEAD_DIM)
    wq_dup = jnp.broadcast_to(wq_h, (d, KV_GROUPS * GROUP_HEADS, 2, HEAD_DIM)).reshape(d, 2 * pw)
    wk_pair = jnp.stack([w_ks.reshape(d, KV_GROUPS, HEAD_DIM), w_kw.reshape(d, KV_GROUPS, HEAD_DIM)], axis=2)
    w_rope = jnp.concatenate([wq_dup, wk_pair.reshape(d, 2 * kw)], axis=1)
    scale = jnp.concatenate([jnp.full((1, 2 * pw), ATTN_SCALE * LOG2E, F32), jnp.ones((1, 2 * kw), F32)], axis=1)
    cos, sin_hi, sin_lo, _ = _rope_tables(positions.reshape(t))
    qk = _proj(hn, w_rope, tn=w_rope.shape[1] // 3, out_dtype=BF16,
               rope=(cos, sin_hi, sin_lo, scale), name="nsa_proj_rope")

    wv_pair = jnp.stack([w_vs.reshape(d, KV_GROUPS, HEAD_DIM), w_vw.reshape(d, KV_GROUPS, HEAD_DIM)], axis=2)
    wv_t = wv_pair.reshape(d, 2 * kw).T
    vt = _proj_t(hn, wv_t, jnp.zeros((2 * kw, 1), F32), tm=ATTN_TILE, tn=2 * kw,
                 out_dtype=BF16, gate=False, name="nsa_proj_values")

    per_group = 3 * GROUP_HEADS
    wg = jnp.pad(w_gl.reshape(d, KV_GROUPS, per_group), ((0, 0), (0, 0), (0, 16 - per_group)))
    bg = jnp.pad(gate_b.reshape(KV_GROUPS, per_group), ((0, 0), (0, 16 - per_group)))
    n_gate = LANES
    wg_t = jnp.pad(wg.reshape(d, KV_GROUPS * 16).T, ((0, n_gate - KV_GROUPS * 16), (0, 0)))
    bg_t = jnp.pad(bg.reshape(KV_GROUPS * 16, 1), ((0, n_gate - KV_GROUPS * 16), (0, 0)))
    gates = _proj_t(hn, wg_t, bg_t, tm=ATTN_TILE, tn=n_gate, out_dtype=F32, gate=True,
                    name="nsa_proj_gates")[:, :KV_GROUPS * 16, :]

    w_plain = jnp.concatenate([w_qm, w_kc, w_vc], axis=1)
    plain = _proj(hn, w_plain, tn=w_plain.shape[1], out_dtype=F32, name="nsa_proj_plain")
    q_mem = plain[:, :MEM_WIDTH].astype(BF16)
    raw_k = plain[:, MEM_WIDTH:MEM_WIDTH + kw]
    raw_v = plain[:, MEM_WIDTH + kw:]

    n_chunk = seq // CMP_STRIDE
    cmp_end = jnp.minimum(jnp.arange(n_chunk) * CMP_STRIDE + CMP_BLOCK - 1, seq - 1)
    cos_c, _, _, sin_c = _rope_tables(positions[:, cmp_end])
    kc, vct = _compress(raw_k, raw_v, pos_k, pos_v, k_w1, k_w2, v_w1, v_w2, cos_c, sin_c, batch, seq)
    prim = _nsa_attention(qk, vt, gates, kc, vct, batch, seq)
    return prim, q_mem


def _split_dot(x, w_bf16):
    hi = x.astype(BF16)
    lo = (x - hi.astype(F32)).astype(BF16)
    return _dot(hi, w_bf16) + _dot(lo, w_bf16)


def _head_ones(width):
    head = np.arange(width) // HEAD_DIM
    return jnp.asarray(head[:, None] == head[None, :], BF16)


def _rwkv_prep_kernel(hn_ref, w_ref, mu_ref, w0_ref, w2_ref, a0_ref, a2_ref, g2_ref, kk_ref, ka_ref, rk_ref,
                      ones_ref, r_ref, k_ref, v_ref, na_ref, b_ref, lw_ref, g_ref, bonus_ref, last_ref,
                      *, tiles_per_seq):
    pw = PRIMARY_WIDTH
    @pl.when(pl.program_id(0) == 0)
    def _():
        last_ref[...] = jnp.zeros_like(last_ref)

    x = _dot(hn_ref[...], w_ref[...])
    first_tile = (pl.program_id(0) % tiles_per_seq) == 0
    last_prev = jnp.where(first_tile, 0.0, last_ref[0:1, :])
    row = lax.broadcasted_iota(jnp.int32, x.shape, 0)
    prev = jnp.where(row == 0, last_prev, pltpu.roll(x, 1, 0))
    last_ref[0:1, :] = x[x.shape[0] - 1:, :]
    xs = x + (prev - x) * mu_ref[...]
    r = xs[:, 0:pw]
    k = xs[:, pw:2 * pw]
    v = xs[:, 2 * pw:3 * pw]
    lora = xs[:, 3 * pw:3 * pw + DECAY_LORA + AAA_LORA]
    gl = xs[:, 3 * pw + DECAY_LORA + AAA_LORA:]
    w_pre = w0_ref[...] + _dot(jnp.tanh(lora).astype(BF16), w2_ref[...])
    z = -w_pre
    w = -(jnp.maximum(z, 0.0) + jnp.log1p(jnp.exp(-jnp.abs(z)))) - 0.5
    lw_ref[...] = -jnp.exp(w)
    a = jax.nn.sigmoid(a0_ref[...] + _dot(lora.astype(BF16), a2_ref[...]))
    g_ref[...] = _dot(jax.nn.sigmoid(gl).astype(BF16), g2_ref[...]).astype(g_ref.dtype)
    ones = ones_ref[...]

    def head_sums(y):
        gw = GROUP_WIDTH
        return jnp.concatenate([_split_dot(y[:, i * gw:(i + 1) * gw], ones) for i in range(KV_GROUPS)], axis=1)

    kk = k * kk_ref[...]
    kk = kk * lax.rsqrt(jnp.maximum(head_sums(kk * kk), 1e-24))
    k2 = k * (1.0 + (a - 1.0) * ka_ref[...])
    r_ref[...] = r
    k_ref[...] = k2
    v_ref[...] = v.astype(v_ref.dtype)
    na_ref[...] = -kk
    b_ref[...] = kk * a
    bonus_ref[...] = (head_sums(r * k2 * rk_ref[...]) * v).astype(bonus_ref.dtype)


def _rwkv_prep(hn, w_in, mu, w0, w2, a0, a2, g2, k_k, k_a, r_k, seq):
    t, d = hn.shape
    width = w_in.shape[1]
    pw = PRIMARY_WIDTH
    tm = min(PREP_TILE, seq)
    row = lambda a: a.reshape(1, -1)
    lora_w = DECAY_LORA + AAA_LORA
    w2e = jnp.concatenate([w2, jnp.zeros((AAA_LORA, pw), F32)], axis=0).astype(BF16)
    a2e = jnp.concatenate([jnp.zeros((DECAY_LORA, pw), F32), a2], axis=0).astype(BF16)
    const = lambda shape: pl.BlockSpec(shape, lambda i: (0, 0))
    out_spec = pl.BlockSpec((tm, pw), lambda i: (i, 0))
    return pl.pallas_call(
        functools.partial(_rwkv_prep_kernel, tiles_per_seq=seq // tm),
        grid=(t // tm,),
        in_specs=[
            pl.BlockSpec((tm, d), lambda i: (i, 0)),
            const((d, width)),
            const((1, width)), const((1, pw)), const((lora_w, pw)), const((1, pw)), const((lora_w, pw)),
            const((GATE_LORA, pw)), const((1, pw)), const((1, pw)), const((1, pw)),
            const((GROUP_WIDTH, GROUP_WIDTH)),
        ],
        out_specs=[out_spec] * 8,
        out_shape=[jax.ShapeDtypeStruct((t, pw), dt) for dt in (F32, F32, BF16, F32, F32, F32, BF16, BF16)],
        scratch_shapes=[pltpu.VMEM((SUBLANES, width), F32)],
        compiler_params=_params("arbitrary"),
        name="rwkv_prep",
    )(hn, w_in.astype(BF16), row(mu), row(w0), w2e, row(a0), a2e, g2.astype(BF16), row(k_k), row(k_a), row(r_k),
      _head_ones(GROUP_WIDTH))


def _block_diag(x, mask):
    return jnp.where(mask, jnp.concatenate([x.astype(BF16)] * GROUP_HEADS, axis=0), jnp.zeros((), BF16))


def _rwkv_scan_kernel(r_ref, k_ref, v_ref, na_ref, b_ref, lw_ref, g_ref, bonus_ref, lng_ref, lnb_ref, o_ref, state_ref):
    n_batch, rows, _ = r_ref.shape
    c = SCAN_CHUNK
    gw = GROUP_WIDTH
    pairs = [(bi, grp) for bi in range(n_batch) for grp in range(KV_GROUPS)]
    units = [(bi, grp, sub) for sub in range(rows // c) for bi, grp in pairs]
    every = lambda fn, *lists: [fn(*args) for args in zip(*lists)]

    @pl.when(pl.program_id(0) == 0)
    def _():
        state_ref[...] = jnp.zeros_like(state_ref)

    rr = lax.broadcasted_iota(jnp.int32, (gw, gw), 0)
    cc = lax.broadcasted_iota(jnp.int32, (gw, gw), 1)
    bd_mask = (rr // HEAD_DIM) == (cc // HEAD_DIM)
    t_idx = lax.broadcasted_iota(jnp.int32, (c, gw), 0)
    s_idx = lax.broadcasted_iota(jnp.int32, (c, gw), 1) % HEAD_DIM
    strict = t_idx > s_idx
    incl = t_idx >= s_idx
    eye = jnp.where(t_idx == s_idx, 1.0, 0.0)
    same_block = {}
    size = INV_BASE
    while size <= c:
        same_block[size] = (t_idx // size) == (s_idx // size)
        size *= 2
    tril =jnp.where(lax.broadcasted_iota(jnp.int32, (c, c), 0) >= lax.broadcasted_iota(jnp.int32, (c, c), 1),
                     1.0, 0.0).astype(BF16)
    ones_bd = jnp.where(bd_mask, 1.0, 0.0).astype(BF16)

    def cumsum_rows(x):
        hi = x.astype(BF16)
        rem = x - hi.astype(F32)
        mid = rem.astype(BF16)
        lo = (rem - mid.astype(F32)).astype(BF16)
        return _dot(tril, hi) + _dot(tril, mid) + _dot(tril, lo)

    bd = lambda x: _block_diag(x, bd_mask)
    cols = lambda grp: slice(grp * gw, (grp + 1) * gw)
    rows_of = lambda sub: slice(sub * c, (sub + 1) * c)
    load = lambda ref: [ref[bi, rows_of(sub), cols(grp)] for bi, grp, sub in units]
    r, k, v, na, bv, lw = (load(ref) for ref in (r_ref, k_ref, v_ref, na_ref, b_ref, lw_ref))

    cum = every(cumsum_rows, lw)
    p_incl = every(jnp.exp, cum)
    inv_p = every(lambda x: jnp.exp(-x), cum)
    b_t = every(lambda x, s: (x * s).astype(BF16), bv, inv_p)
    k_t = every(lambda x, s: (x * s).astype(BF16), k, inv_p)
    x2 = every(lambda a, cu, l, rr_, p: jnp.concatenate([a * jnp.exp(cu - l), rr_ * p], axis=0).astype(BF16),
               na, cum, lw, r, p_incl)

    g_b = every(lambda x, y: _dot_nt(x, bd(y)), x2, b_t)
    g_k = every(lambda x, y: _dot_nt(x, bd(y)), x2, k_t)
    l_ab = every(lambda g: jnp.where(strict, g[:c], 0.0), g_b)
    m_rb = every(lambda g: jnp.where(incl, g[c:], 0.0).astype(BF16), g_b)
    l_ak = every(lambda g: jnp.where(strict, g[:c], 0.0).astype(BF16), g_k)
    m_rk = every(lambda g: jnp.where(incl, g[c:], 0.0).astype(BF16), g_k)

    power = every(lambda l: jnp.where(same_block[INV_BASE], l, 0.0), l_ab)
    t_inv = every(lambda p: eye + p, power)
    for _ in range(int(np.log2(INV_BASE)) - 1):
        power = every(lambda p: _dot(p.astype(BF16), bd(p)), power)
        t_inv = every(lambda t, p: t + _dot(t.astype(BF16), bd(p)), t_inv, power)
    size = 2 * INV_BASE
    while size <= c:
        off_mask = same_block[size] & ~same_block[size // 2]
        half = every(lambda t, l: _dot(t.astype(BF16), bd(jnp.where(off_mask, l, 0.0))), t_inv, l_ab)
        t_inv = every(lambda t, hf: t + _dot(hf.astype(BF16), bd(t)), t_inv, half)
        size *= 2

    v_bd = every(bd, v)
    t_inv = every(lambda t: t.astype(BF16), t_inv)
    lk_v = every(_dot, l_ak, v_bd)
    mk_v = every(_dot, m_rk, v_bd)

    state = [state_ref[bi * KV_GROUPS + grp] for bi, grp in pairs]
    for sub in range(rows // c):
        of = lambda lst: lst[sub * len(pairs):(sub + 1) * len(pairs)]
        xh = every(lambda x, s: _dot_nt(x, s.astype(BF16)), of(x2), state)
        u = every(lambda t, x, lv: _dot(t, bd(x[:c] + lv)), of(t_inv), xh, of(lk_v))
        out = every(lambda x, mb, uu, mv: x[c:] + _dot(mb, bd(uu)) + mv, xh, of(m_rb), u, of(mk_v))
        delta = every(lambda uu, vv, b, kk_: lax.dot_general(
            jnp.concatenate([uu.astype(BF16), vv.astype(BF16)], axis=0), jnp.concatenate([b, kk_], axis=0), _TN,
            preferred_element_type=F32), u, of(v), of(b_t), of(k_t))
        state = every(lambda s, dl, p: (s + jnp.where(bd_mask, dl, 0.0)) * p[c - 1:c, :], state, delta, of(p_incl))
        mean = every(lambda o: _split_dot(o, ones_bd) * (1.0 / HEAD_DIM), out)
        dev = every(lambda o, m: o - m, out, mean)
        var = every(lambda dv: _split_dot(dv * dv, ones_bd) * (1.0 / HEAD_DIM), dev)
        for i, (bi, grp) in enumerate(pairs):
            y = dev[i] * lax.rsqrt(var[i] + RWKV_GN_EPS) * lng_ref[:, cols(grp)] + lnb_ref[:, cols(grp)]
            o_ref[bi, rows_of(sub), cols(grp)] = (
                (y + bonus_ref[bi, rows_of(sub), cols(grp)]) * g_ref[bi, rows_of(sub), cols(grp)]).astype(o_ref.dtype)
    for i, (bi, grp) in enumerate(pairs):
        state_ref[bi * KV_GROUPS + grp] = state[i]


def _rwkv_scan(r, k, v, na, b, lw, g, bonus, ln_g, ln_b, batch, seq):
    c = SCAN_CHUNK
    n_chunks = seq // c
    pw = PRIMARY_WIDTH
    assert GROUP_HEADS * c == GROUP_WIDTH, "block-diagonal packing needs a 64-token chunk"
    assert n_chunks % SCAN_CHUNKS_PER_STEP == 0
    blk = pl.BlockSpec((batch, SCAN_CHUNKS_PER_STEP * c, pw), lambda ci: (0, ci, 0))
    const = pl.BlockSpec((1, pw), lambda ci: (0, 0))
    per_batch = lambda a: a.reshape(batch, seq, pw)
    out = pl.pallas_call(
        _rwkv_scan_kernel,
        grid=(n_chunks // SCAN_CHUNKS_PER_STEP,),
        in_specs=[blk] * 8 + [const, const],
        out_specs=blk,
        out_shape=jax.ShapeDtypeStruct((batch, seq, pw), BF16),
        scratch_shapes=[pltpu.VMEM((batch * KV_GROUPS, GROUP_WIDTH, GROUP_WIDTH), F32)],
        compiler_params=_params("arbitrary"),
        name="rwkv_scan",
    )(*(per_batch(a) for a in (r, k, v, na, b, lw, g, bonus)), ln_g.reshape(1, pw), ln_b.reshape(1, pw))
    return out.reshape(batch * seq, pw)


def _rwkv_layer(hn, w_in, mu, w0, w2, a0, a2, g2, k_k, k_a, r_k, ln_g, ln_b, batch, seq):
    q_mem = _proj(hn, w_in[:, RWKV_SHIFT_W:], tn=MEM_WIDTH, out_dtype=BF16, name="rwkv_proj_mem")
    r, k, v, na, b, lw, g, bonus = _rwkv_prep(hn, w_in[:, :RWKV_SHIFT_W], mu, w0, w2, a0, a2, g2, k_k, k_a,
                                              r_k.reshape(-1), seq)
    prim = _rwkv_scan(r, k, v, na, b, lw, g, bonus, ln_g, ln_b, batch, seq)
    return prim, q_mem


def kernel(x, mem, positions, mem_norm_g, w_mem_kv, pre_mix_g, post_mix_g, pre_ffn_g, post_ffn_g, w_out, w_ffn_in, w_ffn_out, nsa_w_in, nsa_gate_b, nsa_cmp_pos_k, nsa_cmp_pos_v, nsa_cmp_k_w1, nsa_cmp_k_w2, nsa_cmp_v_w1, nsa_cmp_v_w2, rwkv_w_in, rwkv_mu, rwkv_w0, rwkv_w2, rwkv_a0, rwkv_a2, rwkv_g2, rwkv_k_k, rwkv_k_a, rwkv_r_k, rwkv_ln_g, rwkv_ln_b):
    batch, seq, d = x.shape
    n_mem = mem.shape[1]
    depth = pre_mix_g.shape[0]
    mkv = _proj(_norm_cast(mem.reshape(batch * n_mem, d), mem_norm_g), w_mem_kv, tn=w_mem_kv.shape[1],
                out_dtype=BF16, name="mem_kv")
    mem_k = mkv[:, :MEM_WIDTH].reshape(batch, n_mem, MEM_WIDTH)
    mem_v = mkv[:, MEM_WIDTH:].reshape(batch, n_mem, MEM_WIDTH)
    h = x.reshape(batch * seq, d)
    hn = _norm_cast(h, pre_mix_g[0])
    for i in range(depth):
        j = i // 2
        if i % 2 == 0:
            prim, q_mem = _nsa_layer(hn, nsa_w_in[j], nsa_gate_b[j], nsa_cmp_pos_k[j], nsa_cmp_pos_v[j],
                                     nsa_cmp_k_w1[j], nsa_cmp_k_w2[j], nsa_cmp_v_w1[j], nsa_cmp_v_w2[j],
                                     positions, batch, seq)
        else:
            prim, q_mem = _rwkv_layer(hn, rwkv_w_in[j], rwkv_mu[j], rwkv_w0[j], rwkv_w2[j], rwkv_a0[j],
                                      rwkv_a2[j], rwkv_g2[j], rwkv_k_k[j], rwkv_k_a[j], rwkv_r_k[j], rwkv_ln_g[j],
                                      rwkv_ln_b[j], batch, seq)
        mo = _mem_attn(q_mem, mem_k, mem_v, seq)
        h = _out_proj(prim, mo, w_out[i], post_mix_g[i], h)
        if i + 1 < depth:
            h, hn = _ffn(h, pre_ffn_g[i], w_ffn_in[i], w_ffn_out[i], post_ffn_g[i], next_g=pre_mix_g[i + 1])
        else:
            h = _ffn(h, pre_ffn_g[i], w_ffn_in[i], w_ffn_out[i], post_ffn_g[i])
    return h.reshape(batch, seq, d)
```

```python
import functools

import numpy as np
import jax
import jax.numpy as jnp
from jax import lax
from jax.experimental import pallas as pl
from jax.experimental.pallas import tpu as pltpu

F32 = jnp.float32
BF16 = jnp.bfloat16

HEAD_DIM = 64
ROT_DIM = HEAD_DIM // 4
ROT_HALF = ROT_DIM // 2
ROPE_THETA = 500000.0
MEM_HEADS = 4
MEM_WIDTH = MEM_HEADS * HEAD_DIM
KV_GROUPS = 3
GROUP_HEADS = 4
GROUP_WIDTH = GROUP_HEADS * HEAD_DIM
PRIMARY_WIDTH = KV_GROUPS * GROUP_WIDTH
KV_WIDTH = KV_GROUPS * HEAD_DIM
CMP_BLOCK = 32
CMP_STRIDE = 16
SEL_BLOCK = 64
SEL_TOPK = 16
WINDOW = 512
DECAY_LORA = 64
AAA_LORA = 64
GATE_LORA = 128
RWKV_SHIFT_W = 3 * PRIMARY_WIDTH + DECAY_LORA + AAA_LORA + GATE_LORA
RWKV_GN_EPS = HEAD_DIM * 1e-5
NORM_EPS = 1e-6
NEG = -1e30
FORCE = 1e6
ATTN_SCALE = HEAD_DIM ** -0.5
LOG2E = float(np.log2(np.e))
SAFE_LOG2 = 80.0

LANES = 128
SUBLANES = 8
VMEM_LIMIT_BYTES = 48 * 1024 * 1024

ROW_TILE = 512
ATTN_TILE = 256
SCAN_CHUNK = 64
SCAN_CHUNKS_PER_STEP = 2
INV_BASE = 8
SEL_SPAN = 2
CMP_ROWS_STEP = 128
SEL_UNROLL = 4
WIN_TILES = WINDOW // ATTN_TILE + 1
N_FORCED = 3
ACC_ROWS = HEAD_DIM + 16
PREP_TILE = 256

_NT = (((1,), (1,)), ((), ()))
_TN = (((0,), (0,)), ((), ()))


def _params(*sem):
    return pltpu.CompilerParams(dimension_semantics=sem, vmem_limit_bytes=VMEM_LIMIT_BYTES)


def _rms(x, g):
    return x * lax.rsqrt(jnp.mean(x * x, axis=-1, keepdims=True) + NORM_EPS) * g


def _dot(a, b):
    return jnp.dot(a, b, preferred_element_type=F32)


def _dot_nt(a, b):
    return lax.dot_general(a, b, _NT, preferred_element_type=F32)


def _norm_cast_kernel(x_ref, g_ref, o_ref):
    o_ref[...] = _rms(x_ref[...], g_ref[...]).astype(o_ref.dtype)


def _norm_cast(x, g):
    t, d = x.shape
    tm = min(ROW_TILE, t)
    return pl.pallas_call(
        _norm_cast_kernel,
        grid=(t // tm,),
        in_specs=[pl.BlockSpec((tm, d), lambda i: (i, 0)), pl.BlockSpec((1, d), lambda i: (0, 0))],
        out_specs=pl.BlockSpec((tm, d), lambda i: (i, 0)),
        out_shape=jax.ShapeDtypeStruct((t, d), BF16),
        compiler_params=_params("parallel"),
        name="norm_cast",
    )(x, g.reshape(1, d))


def _proj_kernel(x_ref, w_ref, o_ref):
    o_ref[...] = _dot(x_ref[...], w_ref[...]).astype(o_ref.dtype)


def _proj_parts_kernel(x_ref, w_ref, *o_refs, starts):
    y = _dot(x_ref[...], w_ref[...])
    for o_ref, start in zip(o_refs, starts):
        o_ref[...] = y[:, start:start + o_ref.shape[1]].astype(o_ref.dtype)


def _proj_parts(x, parts, name):
    t, d = x.shape
    tm = min(ROW_TILE, t)
    padded = [jnp.pad(w, ((0, 0), (0, -w.shape[1] % LANES))) for w, _ in parts]
    starts = tuple(int(s) for s in np.cumsum([0] + [w.shape[1] for w in padded[:-1]]))
    w_all = jnp.concatenate(padded, axis=1).astype(BF16)
    return pl.pallas_call(
        functools.partial(_proj_parts_kernel, starts=starts),
        grid=(t // tm,),
        in_specs=[pl.BlockSpec((tm, d), lambda i: (i, 0)), pl.BlockSpec(w_all.shape, lambda i: (0, 0))],
        out_specs=[pl.BlockSpec((tm, w.shape[1]), lambda i: (i, 0)) for w, _ in parts],
        out_shape=[jax.ShapeDtypeStruct((t, w.shape[1]), dt) for w, dt in parts],
        compiler_params=_params("parallel"),
        name=name,
    )(x, w_all)


def _proj_rope_kernel(x_ref, w_ref, c_ref, sp_ref, sm_ref, scale_ref, o_ref):
    y = _dot(x_ref[...], w_ref[...])
    width = y.shape[1]
    reps = width // LANES
    cos = jnp.concatenate([c_ref[...]] * reps, axis=1)
    sin_hi = jnp.concatenate([sp_ref[...]] * reps, axis=1)
    sin_lo = jnp.concatenate([sm_ref[...]] * reps, axis=1)
    y = y * cos + pltpu.roll(y, ROT_HALF, 1) * sin_hi + pltpu.roll(y, width - ROT_HALF, 1) * sin_lo
    o_ref[...] = (y * scale_ref[...]).astype(o_ref.dtype)


def _proj_t_kernel(x_ref, wt_ref, b_ref, o_ref, *, gate):
    y = _dot_nt(wt_ref[...], x_ref[...])
    if gate:
        y = jax.nn.sigmoid(y + b_ref[...])
    o_ref[0] = y.astype(o_ref.dtype)


def _proj(x, w, *, tn, out_dtype, rope=None, name):
    t, d = x.shape
    n = w.shape[1]
    tm = min(ROW_TILE, t)
    in_specs = [
        pl.BlockSpec((tm, d), lambda i, j: (i, 0)),
        pl.BlockSpec((d, tn), lambda i, j: (0, j)),
    ]
    args = [x, w.astype(BF16)]
    if rope is None:
        body = _proj_kernel
    else:
        body = _proj_rope_kernel
        cos, sin_hi, sin_lo, scale = rope
        in_specs += [pl.BlockSpec((tm, LANES), lambda i, j: (i, 0))] * 3
        in_specs += [pl.BlockSpec((1, tn), lambda i, j: (0, j))]
        args += [cos, sin_hi, sin_lo, scale]
    return pl.pallas_call(
        body,
        grid=(t // tm, n // tn),
        in_specs=in_specs,
        out_specs=pl.BlockSpec((tm, tn), lambda i, j: (i, j)),
        out_shape=jax.ShapeDtypeStruct((t, n), out_dtype),
        compiler_params=_params("parallel", "parallel"),
        name=name,
    )(*args)


def _proj_t(x, wt, bias, *, tm, tn, out_dtype, gate, name):
    t, d = x.shape
    n = wt.shape[0]
    return pl.pallas_call(
        functools.partial(_proj_t_kernel, gate=gate),
        grid=(t // tm, n // tn),
        in_specs=[
            pl.BlockSpec((tm, d), lambda i, j: (i, 0)),
            pl.BlockSpec((tn, d), lambda i, j: (j, 0)),
            pl.BlockSpec((tn, 1), lambda i, j: (j, 0)),
        ],
        out_specs=pl.BlockSpec((1, tn, tm), lambda i, j: (i, j, 0)),
        out_shape=jax.ShapeDtypeStruct((t // tm, n, tm), out_dtype),
        compiler_params=_params("parallel", "parallel"),
        name=name,
    )(x, wt.astype(BF16), bias)


def _outproj_kernel(a_ref, b_ref, wa_ref, wb_ref, g_ref, h_ref, o_ref):
    y = _dot(a_ref[...], wa_ref[...]) + _dot(b_ref[...], wb_ref[...])
    o_ref[...] = h_ref[...] + _rms(y, g_ref[...])


def _out_proj(prim, mo, w_out, g, h):
    t, d = h.shape
    tm = min(ROW_TILE, t)
    pw = prim.shape[1]
    return pl.pallas_call(
        _outproj_kernel,
        grid=(t // tm,),
        in_specs=[
            pl.BlockSpec((tm, pw), lambda i: (i, 0)),
            pl.BlockSpec((tm, MEM_WIDTH), lambda i: (i, 0)),
            pl.BlockSpec((pw, d), lambda i: (0, 0)),
            pl.BlockSpec((MEM_WIDTH, d), lambda i: (0, 0)),
            pl.BlockSpec((1, d), lambda i: (0, 0)),
            pl.BlockSpec((tm, d), lambda i: (i, 0)),
        ],
        out_specs=pl.BlockSpec((tm, d), lambda i: (i, 0)),
        out_shape=jax.ShapeDtypeStruct((t, d), F32),
        compiler_params=_params("parallel"),
        name="out_proj",
    )(prim, mo, w_out[:pw].astype(BF16), w_out[pw:].astype(BF16), g.reshape(1, d), h)


def _ffn_kernel(h_ref, g1_ref, wg_ref, wu_ref, wo_ref, g2_ref, *rest, feeds_next):
    if feeds_next:
        g3_ref, o_ref, next_ref, hn_ref, acc_ref = rest
    else:
        o_ref, hn_ref, acc_ref = rest
    j = pl.program_id(1)

    @pl.when(j == 0)
    def _():
        hn_ref[...] = _rms(h_ref[...], g1_ref[...]).astype(BF16)
        acc_ref[...] = jnp.zeros_like(acc_ref)

    hn = hn_ref[...]
    gate = _dot(hn, wg_ref[...])
    up = _dot(hn, wu_ref[...])
    act = (jax.nn.silu(gate) * up).astype(BF16)
    acc_ref[...] += _dot(act, wo_ref[...])

    @pl.when(j == pl.num_programs(1) - 1)
    def _():
        out = h_ref[...] + _rms(acc_ref[...], g2_ref[...])
        o_ref[...] = out
        if feeds_next:
            next_ref[...] = _rms(out, g3_ref[...]).astype(next_ref.dtype)


def _ffn_chunk(hidden):
    units = hidden // LANES
    for parts in range(2, units + 1):
        if units % parts == 0:
            return (units // parts) * LANES
    return hidden


def _ffn(h, g1, w_in, w_out, g2, next_g=None):
    t, d = h.shape
    hidden = w_out.shape[0]
    th = _ffn_chunk(hidden)
    nh = hidden // th
    tm = min(ROW_TILE, t)
    w_in = w_in.astype(BF16)
    row = pl.BlockSpec((tm, d), lambda i, j: (i, 0))
    gain = pl.BlockSpec((1, d), lambda i, j: (0, 0))
    feeds_next = next_g is not None
    in_specs = [row, gain,
                pl.BlockSpec((d, th), lambda i, j: (0, j)),
                pl.BlockSpec((d, th), lambda i, j: (0, j + nh)),
                pl.BlockSpec((th, d), lambda i, j: (j, 0)),
                gain]
    args = [h, g1.reshape(1, d), w_in, w_in, w_out.astype(BF16), g2.reshape(1, d)]
    out_specs, out_shape = row, jax.ShapeDtypeStruct((t, d), F32)
    if feeds_next:
        in_specs.append(gain)
        args.append(next_g.reshape(1, d))
        out_specs, out_shape = [row, row], [out_shape, jax.ShapeDtypeStruct((t, d), BF16)]
    return pl.pallas_call(
        functools.partial(_ffn_kernel, feeds_next=feeds_next),
        grid=(t // tm, nh),
        in_specs=in_specs,
        out_specs=out_specs,
        out_shape=out_shape,
        scratch_shapes=[pltpu.VMEM((tm, d), BF16), pltpu.VMEM((tm, d), F32)],
        compiler_params=_params("parallel", "arbitrary"),
        name="ffn",
    )(*args)


def _mem_attn_kernel(q_ref, mk_ref, mv_ref, o_ref):
    q = q_ref[...]
    mk = mk_ref[0]
    mv = mv_ref[0]
    head_of_lane = lax.broadcasted_iota(jnp.int32, mk.shape, 1) // HEAD_DIM
    acc = jnp.zeros(q.shape, F32)
    for h in range(MEM_HEADS):
        s = _dot_nt(q, jnp.where(head_of_lane == h, mk, 0)) * ATTN_SCALE
        e = jnp.exp(s - jnp.max(s, axis=-1, keepdims=True))
        p = e / jnp.sum(e, axis=-1, keepdims=True)
        acc = acc + _dot(p.astype(BF16), jnp.where(head_of_lane == h, mv, 0))
    o_ref[...] = acc.astype(o_ref.dtype)


def _mem_attn(q_mem, mem_k, mem_v, seq):
    t = q_mem.shape[0]
    m = mem_k.shape[1]
    tm = min(ROW_TILE, seq)
    per_seq = seq // tm
    return pl.pallas_call(
        _mem_attn_kernel,
        grid=(t // tm,),
        in_specs=[
            pl.BlockSpec((tm, MEM_WIDTH), lambda i: (i, 0)),
            pl.BlockSpec((1, m, MEM_WIDTH), lambda i: (i // per_seq, 0, 0)),
            pl.BlockSpec((1, m, MEM_WIDTH), lambda i: (i // per_seq, 0, 0)),
        ],
        out_specs=pl.BlockSpec((tm, MEM_WIDTH), lambda i: (i, 0)),
        out_shape=jax.ShapeDtypeStruct((t, MEM_WIDTH), BF16),
        compiler_params=_params("parallel"),
        name="mem_attn",
    )(q_mem, mem_k, mem_v)


def _compress_kernel(xk_ref, xv_ref, pk_ref, pv_ref, k1a_ref, k1b_ref, k2_ref, v1a_ref, v1b_ref, v2t_ref,
                     c_ref, s_ref, kc_ref, vct_ref):
    def hidden(x_ref, pos_ref, wa_ref, wb_ref):
        x = x_ref[0]
        n = x.shape[0]
        first = _dot((x + pos_ref[0:1, :]).astype(BF16), wa_ref[...])
        second = _dot((x + pos_ref[1:2, :]).astype(BF16), wb_ref[...])
        return jax.nn.gelu(first + pltpu.roll(second, n - 1, 0)).astype(BF16)

    hk = hidden(xk_ref, pk_ref, k1a_ref, k1b_ref)
    both = _dot(hk, k2_ref[...])
    kc_ref[0, 0] = (both[:, :LANES] * c_ref[0] + both[:, LANES:] * s_ref[0]).astype(kc_ref.dtype)
    hv = hidden(xv_ref, pv_ref, v1a_ref, v1b_ref)
    vct_ref[0, 0] = _dot_nt(v2t_ref[...], hv).astype(vct_ref.dtype)


def _rope_partner_cols(w):
    d = np.arange(w.shape[1]) % HEAD_DIM
    src = np.where(d < ROT_HALF, np.arange(w.shape[1]) + ROT_HALF, np.arange(w.shape[1]) - ROT_HALF)
    src = np.clip(src, 0, w.shape[1] - 1)
    sign = np.where(d < ROT_HALF, -1.0, np.where(d < ROT_DIM, 1.0, 0.0)).astype(np.float32)
    return w[:, src] * sign


def _compress(raw_k, raw_v, pos_k, pos_v, k_w1, k_w2, v_w1, v_w2, cos_c, sin_c, batch, seq):
    n_chunk = seq // CMP_STRIDE
    feat = CMP_STRIDE * KV_WIDTH
    hid = k_w1.shape[1]

    chunks = lambda raw: raw.reshape(batch, n_chunk, feat)

    def per_group(w_half):
        w = w_half.astype(BF16).reshape(CMP_STRIDE, 1, 1, HEAD_DIM, hid)
        own = (np.arange(KV_GROUPS)[:, None] == np.arange(KV_GROUPS)[None, :]).reshape(1, KV_GROUPS, KV_GROUPS, 1, 1)
        return jnp.where(own, w, jnp.zeros((), BF16)).transpose(1, 0, 2, 3, 4).reshape(KV_GROUPS, feat, hid)

    def pos_rows(pos):
        p = pos.reshape(2, CMP_STRIDE, 1, HEAD_DIM)
        return jnp.broadcast_to(p, (2, CMP_STRIDE, KV_GROUPS, HEAD_DIM)).reshape(2, feat)

    half_rows = CMP_STRIDE * HEAD_DIM
    zeros = jnp.zeros((hid, LANES - HEAD_DIM), F32)
    k2 = jnp.concatenate([k_w2, zeros, _rope_partner_cols(k_w2), zeros], axis=1).astype(BF16)
    x_spec = pl.BlockSpec((1, n_chunk, feat), lambda b, g: (b, 0, 0))
    pos_spec = pl.BlockSpec((2, feat), lambda b, g: (0, 0))
    w1_spec = pl.BlockSpec((None, feat, hid), lambda b, g: (g, 0, 0))
    tab_spec = pl.BlockSpec((1, n_chunk, LANES), lambda b, g: (b, 0, 0))
    return pl.pallas_call(
        _compress_kernel,
        grid=(batch, KV_GROUPS),
        in_specs=[x_spec, x_spec, pos_spec, pos_spec, w1_spec, w1_spec,
                  pl.BlockSpec((hid, 2 * LANES), lambda b, g: (0, 0)),
                  w1_spec, w1_spec,
                  pl.BlockSpec((HEAD_DIM, hid), lambda b, g: (0, 0)),
                  tab_spec, tab_spec],
        out_specs=[pl.BlockSpec((1, 1, n_chunk, LANES), lambda b, g: (b, g, 0, 0)),
                   pl.BlockSpec((1, 1, HEAD_DIM, n_chunk), lambda b, g: (b, g, 0, 0))],
        out_shape=[jax.ShapeDtypeStruct((batch, KV_GROUPS, n_chunk, LANES), BF16),
                   jax.ShapeDtypeStruct((batch, KV_GROUPS, HEAD_DIM, n_chunk), BF16)],
        compiler_params=_params("parallel", "parallel"),
        name="nsa_compress",
    )(chunks(raw_k), chunks(raw_v), pos_rows(pos_k), pos_rows(pos_v),
      per_group(k_w1[:half_rows]), per_group(k_w1[half_rows:]), k2,
      per_group(v_w1[:half_rows]), per_group(v_w1[half_rows:]), v_w2.T.astype(BF16), cos_c, sin_c)


def _split_dot_left(w_bf16, x):
    hi = x.astype(BF16)
    rem = x - hi.astype(F32)
    mid = rem.astype(BF16)
    lo = (rem - mid.astype(F32)).astype(BF16)
    return _dot(w_bf16, hi) + _dot(w_bf16, mid) + _dot(w_bf16, lo)


def _nsa_attn_kernel(q_ref, kk_ref, hot_ref, vt_ref, kc_ref, vct_ref, tap_ref, winb_ref, gate_ref, o_ref,
                     qx_ref, m_ref, acc_ref, cmp_ref, imp_ref, kmax_ref):
    tq = q_ref.shape[0]
    qt = pl.program_id(2)
    t0 = qt * tq
    heads = range(GROUP_HEADS)

    q2 = q_ref[...]
    qs = jnp.concatenate([q2[:, h * LANES:(h + 1) * LANES] for h in heads], axis=0)
    lane = lax.broadcasted_iota(jnp.int32, qs.shape, 1)
    q_sel = jnp.where(lane < HEAD_DIM, qs, 0)
    q_win = jnp.where(lane >= HEAD_DIM, qs, 0)

    col_t = t0 + lax.broadcasted_iota(jnp.int32, (1, tq), 1)

    @pl.when(qt == 0)
    def _():
        half = lax.broadcasted_iota(jnp.int32, (1, LANES), 1) < HEAD_DIM
        k_abs = jnp.abs(kk_ref[...].astype(F32))
        kc_abs = jnp.abs(kc_ref[0, 0].astype(F32))
        row_sums = [jnp.sum(jnp.where(half, k_abs, 0.0), axis=1, keepdims=True),
                    jnp.sum(jnp.where(half, 0.0, k_abs), axis=1, keepdims=True),
                    jnp.sum(kc_abs, axis=1, keepdims=True)]
        kmax_ref[0] = functools.reduce(jnp.maximum, [jnp.max(s) for s in row_sums])
    bounded = jnp.max(jnp.abs(qs.astype(F32))) * kmax_ref[0] <= SAFE_LOG2

    @pl.when(bounded)
    def _():
        _nsa_attend(False, q_sel, q_win, col_t, qt, kk_ref, hot_ref, vt_ref, kc_ref, vct_ref, tap_ref, winb_ref,
                    gate_ref, o_ref, qx_ref, m_ref, acc_ref, cmp_ref, imp_ref)

    @pl.when(jnp.logical_not(bounded))
    def _():
        _nsa_attend(True, q_sel, q_win, col_t, qt, kk_ref, hot_ref, vt_ref, kc_ref, vct_ref, tap_ref, winb_ref,
                    gate_ref, o_ref, qx_ref, m_ref, acc_ref, cmp_ref, imp_ref)


def _nsa_attend(stabilized, q_sel, q_win, col_t, qt, kk_ref, hot_ref, vt_ref, kc_ref, vct_ref, tap_ref, winb_ref,
                gate_ref, o_ref, qx_ref, m_ref, acc_ref, cmp_ref, imp_ref):
    tq = col_t.shape[1]
    tk = tq
    heads = range(GROUP_HEADS)
    head_rows = lambda h: slice(h * tq, (h + 1) * tq)

    def weights(scores):
        if not stabilized:
            return jnp.exp2(scores), None
        top = jnp.max(scores, axis=0, keepdims=True)
        return jnp.exp2(scores - top), top > 0.5 * NEG

    n_cmp = kc_ref.shape[2]
    step = min(CMP_ROWS_STEP, n_cmp)

    def compressed(rows):
        blk_end = lax.broadcasted_iota(jnp.int32, (rows, 1), 0) * CMP_STRIDE + (CMP_BLOCK - 1)
        valid = blk_end <= col_t
        cmp_scores = [_dot_nt(kc_ref[0, 0, 0:rows, :], q_sel[head_rows(h)]) for h in heads]
        p_sum = jnp.zeros((rows, tq), F32)
        for h in heads:
            e, live = weights(jnp.where(valid, cmp_scores[h], NEG))
            total = jnp.sum(e, axis=0, keepdims=True)
            live = total > 0.0 if live is None else live
            p = e * jnp.where(live, 1.0 / total, 0.0)
            cmp_ref[:, head_rows(h)] = _dot(vct_ref[0, 0, :, 0:rows], p.astype(BF16))
            p_sum = p_sum + p
        imp_ref[...] = _split_dot_left(tap_ref[:, 0:rows], p_sum)

    last_visible = (qt * tq + tq - 1 - (CMP_BLOCK - 1)) // CMP_STRIDE
    steps_needed = jnp.maximum(last_visible, 0) // step + 1
    for count in range(1, n_cmp // step + 1):
        @pl.when(steps_needed == count)
        def _():
            compressed(count * step)

    imp = imp_ref[...]
    o_cmp = [cmp_ref[:, head_rows(h)] for h in heads]

    n_sel = imp.shape[0]
    blk = lax.broadcasted_iota(jnp.int32, (n_sel, tq), 0)
    blk_f = blk.astype(F32)
    cur = col_t // SEL_BLOCK
    forced = (blk == 0) | (blk == cur) | (blk == cur - 1)
    work = jnp.where(forced, -jnp.inf, jnp.where(blk <= cur, imp, -FORCE))
    for _ in range(max(min(SEL_TOPK, n_sel) - N_FORCED, 0)):
        best = jnp.max(work, axis=0, keepdims=True)
        hit = blk_f == jnp.min(jnp.where(work == best, blk_f, float(n_sel)), axis=0, keepdims=True)
        work = jnp.where(hit, -jnp.inf, work)
    chosen = jnp.where(work == -jnp.inf, 1.0, 0.0)
    if n_sel < LANES:
        chosen = jnp.concatenate([chosen, jnp.zeros((LANES - n_sel, tq), F32)], axis=0)
    block_bias = ((chosen - 1.0) * (-NEG)).T.astype(BF16)
    qx_ref[:, 0:LANES] = q_sel
    qx_ref[:, LANES:2 * LANES] = jnp.concatenate([block_bias] * GROUP_HEADS, axis=0)

    def key_pos(kt, n):
        return kt * tk + lax.broadcasted_iota(jnp.int32, (n * tk, 1), 0)

    def key_rows(ref, kt, n):
        return ref[pl.ds(pl.multiple_of(kt * tk, tk), n * tk), :]

    def value_rows(kt, n, first_row):
        vals = jnp.concatenate([vt_ref[kt + i, first_row:first_row + HEAD_DIM, :] for i in range(n)], axis=1)
        return jnp.concatenate([vals, jnp.ones((ACC_ROWS - HEAD_DIM, n * tk), BF16)], axis=0)

    def normalized(acc):
        return acc[0:HEAD_DIM] / acc[HEAD_DIM:HEAD_DIM + 1]

    win_start = jnp.maximum(qt - (WIN_TILES - 1), 0)
    win_bias = winb_ref[qt - win_start]
    win_keys = key_rows(kk_ref, win_start, WIN_TILES)
    win_vals = value_rows(win_start, WIN_TILES, HEAD_DIM)
    win_scores = [_dot_nt(win_keys, q_win[head_rows(h)]) for h in heads]
    o_win = [normalized(_dot(win_vals, weights(win_scores[h] + win_bias)[0].astype(BF16))) for h in heads]

    acc_ref[...] = jnp.zeros(acc_ref.shape, F32)
    full_spans = qt // SEL_SPAN

    def score_span(i):
        kt = i * SEL_SPAN
        keys = jnp.concatenate([key_rows(kk_ref, kt, SEL_SPAN), key_rows(hot_ref, kt, SEL_SPAN)], axis=1)
        return tuple(_dot_nt(keys, qx_ref[head_rows(h), :]) for h in heads)

    def causal_scores(i, scores):
        return jnp.where(key_pos(i * SEL_SPAN, SEL_SPAN) <= col_t, scores, NEG)

    if not stabilized:
        def plain_spans(first, count, causal_last):
            scores = [score_span(first + s) for s in range(count)]
            for s in range(count):
                v_ext = value_rows((first + s) * SEL_SPAN, SEL_SPAN, 0)
                for h in heads:
                    masked = causal_last and s == count - 1
                    e = jnp.exp2(causal_scores(first + s, scores[s][h]) if masked else scores[s][h]).astype(BF16)
                    acc_ref[:, head_rows(h)] += _dot(v_ext, e)

        def plain_step(i, carry):
            plain_spans(i * SEL_UNROLL, SEL_UNROLL, False)
            return carry

        whole = full_spans // SEL_UNROLL
        lax.fori_loop(0, whole, plain_step, 0)
        for left in range(SEL_UNROLL):
            @pl.when(full_spans - whole * SEL_UNROLL == left)
            def _():
                plain_spans(whole * SEL_UNROLL, left + 1, True)
    else:
        m_ref[...] = jnp.full(m_ref.shape, NEG, F32)

        def flash_span(i, span_scores, causal):
            v_ext = value_rows(i * SEL_SPAN, SEL_SPAN, 0)
            for h in heads:
                cols = head_rows(h)
                scores = causal_scores(i, span_scores[h]) if causal else span_scores[h]
                m_old = m_ref[:, cols]
                m_new = jnp.maximum(m_old, jnp.max(scores, axis=0, keepdims=True))
                e = jnp.exp2(scores - m_new).astype(BF16)
                acc_ref[:, cols] = jnp.exp2(m_old - m_new) * acc_ref[:, cols] + _dot(v_ext, e)
                m_ref[:, cols] = m_new

        def sel_step(i, span_scores):
            following = score_span(i + 1)
            flash_span(i, span_scores, False)
            return following

        last_scores = lax.fori_loop(0, full_spans, sel_step, score_span(0))
        flash_span(full_spans, last_scores, True)

    gates = gate_ref[...]
    outs = []
    for h in heads:
        outs.append(gates[3 * h:3 * h + 1, :] * o_cmp[h]
                    + gates[3 * h + 1:3 * h + 2, :] * normalized(acc_ref[:, head_rows(h)])
                    + gates[3 * h + 2:3 * h + 3, :] * o_win[h])
    o_ref[...] = jnp.concatenate(outs, axis=0).T.astype(o_ref.dtype)


def _tap_matrix(n_sel, n_chunk):
    ratio = SEL_BLOCK // CMP_STRIDE
    tap = np.zeros((n_sel, n_chunk), np.float32)
    n_cmp = n_chunk - (CMP_BLOCK // CMP_STRIDE - 1)
    for j in range(n_sel):
        for n in range(n_cmp):
            lo = max(n * CMP_STRIDE, j * SEL_BLOCK)
            hi = min(n * CMP_STRIDE + CMP_BLOCK, (j + 1) * SEL_BLOCK)
            if hi > lo:
                tap[j, n] = (hi - lo) / CMP_STRIDE
    return jnp.asarray(tap, BF16)


def _nsa_attention(qk, vt, gates, kc, vct, batch, seq):
    tq = ATTN_TILE
    n_tiles = seq // tq
    n_chunk = kc.shape[2]
    n_sel = seq // SEL_BLOCK
    q_blocks = GROUP_HEADS * LANES
    k_col0 = KV_GROUPS * q_blocks // LANES
    vt5 = vt.reshape(batch, n_tiles, KV_GROUPS, 2 * HEAD_DIM, tq)
    gates5 = gates.reshape(batch, n_tiles, KV_GROUPS, 16, tq)
    width = GROUP_HEADS * tq
    assert n_sel <= LANES, "block one-hot is one lane tile wide"
    assert n_tiles % SEL_SPAN == 0 and n_tiles >= WIN_TILES
    hot = jnp.asarray((np.arange(seq)[:, None] // SEL_BLOCK) == np.arange(LANES)[None, :], BF16)
    dist = (np.arange(WIN_TILES)[:, None, None] * tq + np.arange(tq)[None, None, :]
            - np.arange(WIN_TILES * tq)[None, :, None])
    win_bias = jnp.asarray(np.where((dist >= 0) & (dist < WINDOW), 0.0, NEG), F32)
    return pl.pallas_call(
        _nsa_attn_kernel,
        grid=(batch, KV_GROUPS, n_tiles),
        in_specs=[
            pl.BlockSpec((tq, q_blocks), lambda b, g, i: (b * n_tiles + i, g)),
            pl.BlockSpec((seq, LANES), lambda b, g, i: (b, k_col0 + g)),
            pl.BlockSpec((seq, LANES), lambda b, g, i: (0, 0)),
            pl.BlockSpec((None, n_tiles, None, 2 * HEAD_DIM, tq), lambda b, g, i: (b, 0, g, 0, 0)),
            pl.BlockSpec((1, 1, n_chunk, LANES), lambda b, g, i: (b, g, 0, 0)),
            pl.BlockSpec((1, 1, HEAD_DIM, n_chunk), lambda b, g, i: (b, g, 0, 0)),
            pl.BlockSpec((n_sel, n_chunk), lambda b, g, i: (0, 0)),
            pl.BlockSpec((WIN_TILES, WIN_TILES * tq, tq), lambda b, g, i: (0, 0, 0)),
            pl.BlockSpec((None, None, None, 16, tq), lambda b, g, i: (b, i, g, 0, 0)),
        ],
        out_specs=pl.BlockSpec((tq, GROUP_WIDTH), lambda b, g, i: (b * n_tiles + i, g)),
        out_shape=jax.ShapeDtypeStruct((batch * seq, PRIMARY_WIDTH), BF16),
        scratch_shapes=[
            pltpu.VMEM((width, 2 * LANES), BF16),
            pltpu.VMEM((1, width), F32),
            pltpu.VMEM((ACC_ROWS, width), F32),
            pltpu.VMEM((HEAD_DIM, width), F32),
            pltpu.VMEM((n_sel, tq), F32),
            pltpu.SMEM((1,), F32),
        ],
        compiler_params=_params("parallel", "parallel", "arbitrary"),
        name="nsa_attention",
    )(qk, qk, hot, vt5, kc, vct, _tap_matrix(n_sel, n_chunk), win_bias, gates5)


def _rope_tables(positions):
    inv = ROPE_THETA ** (-jnp.arange(0, ROT_DIM, 2, dtype=F32) / ROT_DIM)
    ang = positions.astype(F32)[..., None] * inv
    cos, sin = lax.optimization_barrier((jnp.cos(ang), jnp.sin(ang)))
    reps = (1,) * (ang.ndim - 1) + (LANES // ROT_HALF,)
    cos_t, sin_t = jnp.tile(cos, reps), jnp.tile(sin, reps)
    dim = jnp.arange(LANES) % HEAD_DIM
    lo, hi = dim < ROT_HALF, (dim >= ROT_HALF) & (dim < ROT_DIM)
    return (jnp.where(lo | hi, cos_t, 1.0), jnp.where(hi, sin_t, 0.0), jnp.where(lo, -sin_t, 0.0),
            jnp.where(lo | hi, sin_t, 0.0))


def _nsa_layer(hn, w_in, gate_b, pos_k, pos_v, k_w1, k_w2, v_w1, v_w2, positions, batch, seq):
    t, d = hn.shape
    pw, kw = PRIMARY_WIDTH, KV_WIDTH
    offs = np.cumsum([0, pw] + [kw] * 6 + [3 * KV_GROUPS * GROUP_HEADS, MEM_WIDTH])
    col = lambda i: w_in[:, offs[i]:offs[i + 1]]
    w_q, w_kc, w_vc, w_ks, w_vs, w_kw, w_vw, w_gl, w_qm = [col(i) for i in range(9)]

    wq_h = w_q.reshape(d, KV_GROUPS * GROUP_HEADS, 1, HEAD_DIM)
    wq_dup = jnp.broadcast_to(wq_h, (d, KV_GROUPS * GROUP_HEADS, 2, HEAD_DIM)).reshape(d, 2 * pw)
    wk_pair = jnp.stack([w_ks.reshape(d, KV_GROUPS, HEAD_DIM), w_kw.reshape(d, KV_GROUPS, HEAD_DIM)], axis=2)
    w_rope = jnp.concatenate([wq_dup, wk_pair.reshape(d, 2 * kw)], axis=1)
    scale = jnp.concatenate([jnp.full((1, 2 * pw), ATTN_SCALE * LOG2E, F32), jnp.ones((1, 2 * kw), F32)], axis=1)
    cos, sin_hi, sin_lo, _ = _rope_tables(positions.reshape(t))
    qk = _proj(hn, w_rope, tn=w_rope.shape[1] // 3, out_dtype=BF16,
               rope=(cos, sin_hi, sin_lo, scale), name="nsa_proj_rope")

    wv_pair = jnp.stack([w_vs.reshape(d, KV_GROUPS, HEAD_DIM), w_vw.reshape(d, KV_GROUPS, HEAD_DIM)], axis=2)
    wv_t = wv_pair.reshape(d, 2 * kw).T
    vt = _proj_t(hn, wv_t, jnp.zeros((2 * kw, 1), F32), tm=ATTN_TILE, tn=2 * kw,
                 out_dtype=BF16, gate=False, name="nsa_proj_values")

    per_group = 3 * GROUP_HEADS
    wg = jnp.pad(w_gl.reshape(d, KV_GROUPS, per_group), ((0, 0), (0, 0), (0, 16 - per_group)))
    bg = jnp.pad(gate_b.reshape(KV_GROUPS, per_group), ((0, 0), (0, 16 - per_group)))
    n_gate = LANES
    wg_t = jnp.pad(wg.reshape(d, KV_GROUPS * 16).T, ((0, n_gate - KV_GROUPS * 16), (0, 0)))
    bg_t = jnp.pad(bg.reshape(KV_GROUPS * 16, 1), ((0, n_gate - KV_GROUPS * 16), (0, 0)))
    gates = _proj_t(hn, wg_t, bg_t, tm=ATTN_TILE, tn=n_gate, out_dtype=F32, gate=True,
                    name="nsa_proj_gates")[:, :KV_GROUPS * 16, :]

    q_mem, raw_k, raw_v = _proj_parts(hn, [(w_qm, BF16), (w_kc, F32), (w_vc, F32)], "nsa_proj_plain")

    n_chunk = seq // CMP_STRIDE
    cmp_end = jnp.minimum(jnp.arange(n_chunk) * CMP_STRIDE + CMP_BLOCK - 1, seq - 1)
    cos_c, _, _, sin_c = _rope_tables(positions[:, cmp_end])
    kc, vct = _compress(raw_k, raw_v, pos_k, pos_v, k_w1, k_w2, v_w1, v_w2, cos_c, sin_c, batch, seq)
    prim = _nsa_attention(qk, vt, gates, kc, vct, batch, seq)
    return prim, q_mem


def _split_dot(x, w_bf16):
    hi = x.astype(BF16)
    lo = (x - hi.astype(F32)).astype(BF16)
    return _dot(hi, w_bf16) + _dot(lo, w_bf16)


def _head_ones(width):
    head = np.arange(width) // HEAD_DIM
    return jnp.asarray(head[:, None] == head[None, :], BF16)


def _rwkv_prep_kernel(hn_ref, w_ref, mu_ref, w0_ref, w2_ref, a0_ref, a2_ref, g2_ref, kk_ref, ka_ref, rk_ref,
                      ones_ref, r_ref, k_ref, v_ref, na_ref, b_ref, lw_ref, g_ref, bonus_ref, last_ref,
                      *, tiles_per_seq):
    pw = PRIMARY_WIDTH
    @pl.when(pl.program_id(0) == 0)
    def _():
        last_ref[...] = jnp.zeros_like(last_ref)

    x = _dot(hn_ref[...], w_ref[...])
    first_tile = (pl.program_id(0) % tiles_per_seq) == 0
    last_prev = jnp.where(first_tile, 0.0, last_ref[0:1, :])
    row = lax.broadcasted_iota(jnp.int32, x.shape, 0)
    prev = jnp.where(row == 0, last_prev, pltpu.roll(x, 1, 0))
    last_ref[0:1, :] = x[x.shape[0] - 1:, :]
    xs = x + (prev - x) * mu_ref[...]
    r = xs[:, 0:pw]
    k = xs[:, pw:2 * pw]
    v = xs[:, 2 * pw:3 * pw]
    lora = xs[:, 3 * pw:3 * pw + DECAY_LORA + AAA_LORA]
    gl = xs[:, 3 * pw + DECAY_LORA + AAA_LORA:]
    w_pre = w0_ref[...] + _dot(jnp.tanh(lora).astype(BF16), w2_ref[...])
    z = -w_pre
    w = -(jnp.maximum(z, 0.0) + jnp.log1p(jnp.exp(-jnp.abs(z)))) - 0.5
    lw_ref[...] = -jnp.exp(w)
    a = jax.nn.sigmoid(a0_ref[...] + _dot(lora.astype(BF16), a2_ref[...]))
    g_ref[...] = _dot(jax.nn.sigmoid(gl).astype(BF16), g2_ref[...]).astype(g_ref.dtype)
    ones = ones_ref[...]

    def head_sums(y):
        gw = GROUP_WIDTH
        return jnp.concatenate([_split_dot(y[:, i * gw:(i + 1) * gw], ones) for i in range(KV_GROUPS)], axis=1)

    kk = k * kk_ref[...]
    kk = kk * lax.rsqrt(jnp.maximum(head_sums(kk * kk), 1e-24))
    k2 = k * (1.0 + (a - 1.0) * ka_ref[...])
    r_ref[...] = r
    k_ref[...] = k2
    v_ref[...] = v.astype(v_ref.dtype)
    na_ref[...] = -kk
    b_ref[...] = kk * a
    bonus_ref[...] = (head_sums(r * k2 * rk_ref[...]) * v).astype(bonus_ref.dtype)


def _rwkv_prep(hn, w_in, mu, w0, w2, a0, a2, g2, k_k, k_a, r_k, seq):
    t, d = hn.shape
    width = w_in.shape[1]
    pw = PRIMARY_WIDTH
    tm = min(PREP_TILE, seq)
    row = lambda a: a.reshape(1, -1)
    lora_w = DECAY_LORA + AAA_LORA
    w2e = jnp.concatenate([w2, jnp.zeros((AAA_LORA, pw), F32)], axis=0).astype(BF16)
    a2e = jnp.concatenate([jnp.zeros((DECAY_LORA, pw), F32), a2], axis=0).astype(BF16)
    const = lambda shape: pl.BlockSpec(shape, lambda i: (0, 0))
    out_spec = pl.BlockSpec((tm, pw), lambda i: (i, 0))
    return pl.pallas_call(
        functools.partial(_rwkv_prep_kernel, tiles_per_seq=seq // tm),
        grid=(t // tm,),
        in_specs=[
            pl.BlockSpec((tm, d), lambda i: (i, 0)),
            const((d, width)),
            const((1, width)), const((1, pw)), const((lora_w, pw)), const((1, pw)), const((lora_w, pw)),
            const((GATE_LORA, pw)), const((1, pw)), const((1, pw)), const((1, pw)),
            const((GROUP_WIDTH, GROUP_WIDTH)),
        ],
        out_specs=[out_spec] * 8,
        out_shape=[jax.ShapeDtypeStruct((t, pw), dt) for dt in (F32, F32, BF16, F32, F32, F32, BF16, BF16)],
        scratch_shapes=[pltpu.VMEM((SUBLANES, width), F32)],
        compiler_params=_params("arbitrary"),
        name="rwkv_prep",
    )(hn, w_in.astype(BF16), row(mu), row(w0), w2e, row(a0), a2e, g2.astype(BF16), row(k_k), row(k_a), row(r_k),
      _head_ones(GROUP_WIDTH))


def _block_diag(x, mask):
    return jnp.where(mask, jnp.concatenate([x.astype(BF16)] * GROUP_HEADS, axis=0), jnp.zeros((), BF16))


def _rwkv_scan_kernel(r_ref, k_ref, v_ref, na_ref, b_ref, lw_ref, g_ref, bonus_ref, lng_ref, lnb_ref, o_ref, state_ref):
    n_batch, rows, _ = r_ref.shape
    c = SCAN_CHUNK
    gw = GROUP_WIDTH
    pairs = [(bi, grp) for bi in range(n_batch) for grp in range(KV_GROUPS)]
    units = [(bi, grp, sub) for sub in range(rows // c) for bi, grp in pairs]
    every = lambda fn, *lists: [fn(*args) for args in zip(*lists)]

    @pl.when(pl.program_id(0) == 0)
    def _():
        state_ref[...] = jnp.zeros_like(state_ref)

    rr = lax.broadcasted_iota(jnp.int32, (gw, gw), 0)
    cc = lax.broadcasted_iota(jnp.int32, (gw, gw), 1)
    bd_mask = (rr // HEAD_DIM) == (cc // HEAD_DIM)
    t_idx = lax.broadcasted_iota(jnp.int32, (c, gw), 0)
    s_idx = lax.broadcasted_iota(jnp.int32, (c, gw), 1) % HEAD_DIM
    strict = t_idx > s_idx
    incl = t_idx >= s_idx
    eye = jnp.where(t_idx == s_idx, 1.0, 0.0)
    same_block = {}
    size = INV_BASE
    while size <= c:
        same_block[size] = (t_idx // size) == (s_idx // size)
        size *= 2
    tril =jnp.where(lax.broadcasted_iota(jnp.int32, (c, c), 0) >= lax.broadcasted_iota(jnp.int32, (c, c), 1),
                     1.0, 0.0).astype(BF16)
    ones_bd = jnp.where(bd_mask, 1.0, 0.0).astype(BF16)

    def cumsum_rows(x):
        hi = x.astype(BF16)
        rem = x - hi.astype(F32)
        mid = rem.astype(BF16)
        lo = (rem - mid.astype(F32)).astype(BF16)
        return _dot(tril, hi) + _dot(tril, mid) + _dot(tril, lo)

    bd = lambda x: _block_diag(x, bd_mask)
    cols = lambda grp: slice(grp * gw, (grp + 1) * gw)
    rows_of = lambda sub: slice(sub * c, (sub + 1) * c)
    load = lambda ref: [ref[bi, rows_of(sub), cols(grp)] for bi, grp, sub in units]
    r, k, v, na, bv, lw = (load(ref) for ref in (r_ref, k_ref, v_ref, na_ref, b_ref, lw_ref))

    cum = every(cumsum_rows, lw)
    p_incl = every(jnp.exp, cum)
    inv_p = every(lambda x: jnp.exp(-x), cum)
    b_t = every(lambda x, s: (x * s).astype(BF16), bv, inv_p)
    k_t = every(lambda x, s: (x * s).astype(BF16), k, inv_p)
    x2 = every(lambda a, cu, l, rr_, p: jnp.concatenate([a * jnp.exp(cu - l), rr_ * p], axis=0).astype(BF16),
               na, cum, lw, r, p_incl)

    g_b = every(lambda x, y: _dot_nt(x, bd(y)), x2, b_t)
    g_k = every(lambda x, y: _dot_nt(x, bd(y)), x2, k_t)
    l_ab = every(lambda g: jnp.where(strict, g[:c], 0.0), g_b)
    m_rb = every(lambda g: jnp.where(incl, g[c:], 0.0).astype(BF16), g_b)
    l_ak = every(lambda g: jnp.where(strict, g[:c], 0.0).astype(BF16), g_k)
    m_rk = every(lambda g: jnp.where(incl, g[c:], 0.0).astype(BF16), g_k)

    power = every(lambda l: jnp.where(same_block[INV_BASE], l, 0.0), l_ab)
    t_inv = every(lambda p: eye + p, power)
    for _ in range(int(np.log2(INV_BASE)) - 1):
        power = every(lambda p: _dot(p.astype(BF16), bd(p)), power)
        t_inv = every(lambda t, p: t + _dot(t.astype(BF16), bd(p)), t_inv, power)
    size = 2 * INV_BASE
    while size <= c:
        off_mask = same_block[size] & ~same_block[size // 2]
        half = every(lambda t, l: _dot(t.astype(BF16), bd(jnp.where(off_mask, l, 0.0))), t_inv, l_ab)
        t_inv = every(lambda t, hf: t + _dot(hf.astype(BF16), bd(t)), t_inv, half)
        size *= 2

    v_bd = every(bd, v)
    t_inv = every(lambda t: t.astype(BF16), t_inv)
    lk_v = every(_dot, l_ak, v_bd)
    mk_v = every(_dot, m_rk, v_bd)

    state = [state_ref[bi * KV_GROUPS + grp] for bi, grp in pairs]
    for sub in range(rows // c):
        of = lambda lst: lst[sub * len(pairs):(sub + 1) * len(pairs)]
        xh = every(lambda x, s: _dot_nt(x, s.astype(BF16)), of(x2), state)
        u = every(lambda t, x, lv: _dot(t, bd(x[:c] + lv)), of(t_inv), xh, of(lk_v))
        out = every(lambda x, mb, uu, mv: x[c:] + _dot(mb, bd(uu)) + mv, xh, of(m_rb), u, of(mk_v))
        delta = every(lambda uu, vv, b, kk_: lax.dot_general(
            jnp.concatenate([uu.astype(BF16), vv.astype(BF16)], axis=0), jnp.concatenate([b, kk_], axis=0), _TN,
            preferred_element_type=F32), u, of(v), of(b_t), of(k_t))
        state = every(lambda s, dl, p: (s + jnp.where(bd_mask, dl, 0.0)) * p[c - 1:c, :], state, delta, of(p_incl))
        mean = every(lambda o: _split_dot(o, ones_bd) * (1.0 / HEAD_DIM), out)
        dev = every(lambda o, m: o - m, out, mean)
        var = every(lambda dv: _split_dot(dv * dv, ones_bd) * (1.0 / HEAD_DIM), dev)
        for i, (bi, grp) in enumerate(pairs):
            y = dev[i] * lax.rsqrt(var[i] + RWKV_GN_EPS) * lng_ref[:, cols(grp)] + lnb_ref[:, cols(grp)]
            o_ref[bi, rows_of(sub), cols(grp)] = (
                (y + bonus_ref[bi, rows_of(sub), cols(grp)]) * g_ref[bi, rows_of(sub), cols(grp)]).astype(o_ref.dtype)
    for i, (bi, grp) in enumerate(pairs):
        state_ref[bi * KV_GROUPS + grp] = state[i]


def _rwkv_scan(r, k, v, na, b, lw, g, bonus, ln_g, ln_b, batch, seq):
    c = SCAN_CHUNK
    n_chunks = seq // c
    pw = PRIMARY_WIDTH
    assert GROUP_HEADS * c == GROUP_WIDTH, "block-diagonal packing needs a 64-token chunk"
    assert n_chunks % SCAN_CHUNKS_PER_STEP == 0
    blk = pl.BlockSpec((batch, SCAN_CHUNKS_PER_STEP * c, pw), lambda ci: (0, ci, 0))
    const = pl.BlockSpec((1, pw), lambda ci: (0, 0))
    per_batch = lambda a: a.reshape(batch, seq, pw)
    out = pl.pallas_call(
        _rwkv_scan_kernel,
        grid=(n_chunks // SCAN_CHUNKS_PER_STEP,),
        in_specs=[blk] * 8 + [const, const],
        out_specs=blk,
        out_shape=jax.ShapeDtypeStruct((batch, seq, pw), BF16),
        scratch_shapes=[pltpu.VMEM((batch * KV_GROUPS, GROUP_WIDTH, GROUP_WIDTH), F32)],
        compiler_params=_params("arbitrary"),
        name="rwkv_scan",
    )(*(per_batch(a) for a in (r, k, v, na, b, lw, g, bonus)), ln_g.reshape(1, pw), ln_b.reshape(1, pw))
    return out.reshape(batch * seq, pw)


def _rwkv_layer(hn, w_in, mu, w0, w2, a0, a2, g2, k_k, k_a, r_k, ln_g, ln_b, batch, seq):
    q_mem = _proj(hn, w_in[:, RWKV_SHIFT_W:], tn=MEM_WIDTH, out_dtype=BF16, name="rwkv_proj_mem")
    r, k, v, na, b, lw, g, bonus = _rwkv_prep(hn, w_in[:, :RWKV_SHIFT_W], mu, w0, w2, a0, a2, g2, k_k, k_a,
                                              r_k.reshape(-1), seq)
    prim = _rwkv_scan(r, k, v, na, b, lw, g, bonus, ln_g, ln_b, batch, seq)
    return prim, q_mem


def kernel(x, mem, positions, mem_norm_g, w_mem_kv, pre_mix_g, post_mix_g, pre_ffn_g, post_ffn_g, w_out, w_ffn_in, w_ffn_out, nsa_w_in, nsa_gate_b, nsa_cmp_pos_k, nsa_cmp_pos_v, nsa_cmp_k_w1, nsa_cmp_k_w2, nsa_cmp_v_w1, nsa_cmp_v_w2, rwkv_w_in, rwkv_mu, rwkv_w0, rwkv_w2, rwkv_a0, rwkv_a2, rwkv_g2, rwkv_k_k, rwkv_k_a, rwkv_r_k, rwkv_ln_g, rwkv_ln_b):
    batch, seq, d = x.shape
    n_mem = mem.shape[1]
    depth = pre_mix_g.shape[0]
    mkv = _proj(_norm_cast(mem.reshape(batch * n_mem, d), mem_norm_g), w_mem_kv, tn=w_mem_kv.shape[1],
                out_dtype=BF16, name="mem_kv")
    mem_k = mkv[:, :MEM_WIDTH].reshape(batch, n_mem, MEM_WIDTH)
    mem_v = mkv[:, MEM_WIDTH:].reshape(batch, n_mem, MEM_WIDTH)
    h = x.reshape(batch * seq, d)
    hn = _norm_cast(h, pre_mix_g[0])
    for i in range(depth):
        j = i // 2
        if i % 2 == 0:
            prim, q_mem = _nsa_layer(hn, nsa_w_in[j], nsa_gate_b[j], nsa_cmp_pos_k[j], nsa_cmp_pos_v[j],
                                     nsa_cmp_k_w1[j], nsa_cmp_k_w2[j], nsa_cmp_v_w1[j], nsa_cmp_v_w2[j],
                                     positions, batch, seq)
        else:
            prim, q_mem = _rwkv_layer(hn, rwkv_w_in[j], rwkv_mu[j], rwkv_w0[j], rwkv_w2[j], rwkv_a0[j],
                                      rwkv_a2[j], rwkv_g2[j], rwkv_k_k[j], rwkv_k_a[j], rwkv_r_k[j], rwkv_ln_g[j],
                                      rwkv_ln_b[j], batch, seq)
        mo = _mem_attn(q_mem, mem_k, mem_v, seq)
        h = _out_proj(prim, mo, w_out[i], post_mix_g[i], h)
        if i + 1 < depth:
            h, hn = _ffn(h, pre_ffn_g[i], w_ffn_in[i], w_ffn_out[i], post_ffn_g[i], next_g=pre_mix_g[i + 1])
        else:
            h = _ffn(h, pre_ffn_g[i], w_ffn_in[i], w_ffn_out[i], post_ffn_g[i])
    return h.reshape(batch, seq, d)
```

```python
import functools

import numpy as np
import jax
import jax.numpy as jnp
from jax import lax
from jax.experimental import pallas as pl
from jax.experimental.pallas import tpu as pltpu

F32 = jnp.float32
BF16 = jnp.bfloat16

HEAD_DIM = 64
ROT_DIM = HEAD_DIM // 4
ROT_HALF = ROT_DIM // 2
ROPE_THETA = 500000.0
MEM_HEADS = 4
MEM_WIDTH = MEM_HEADS * HEAD_DIM
KV_GROUPS = 3
GROUP_HEADS = 4
GROUP_WIDTH = GROUP_HEADS * HEAD_DIM
PRIMARY_WIDTH = KV_GROUPS * GROUP_WIDTH
KV_WIDTH = KV_GROUPS * HEAD_DIM
CMP_BLOCK = 32
CMP_STRIDE = 16
SEL_BLOCK = 64
SEL_TOPK = 16
WINDOW = 512
DECAY_LORA = 64
AAA_LORA = 64
GATE_LORA = 128
RWKV_SHIFT_W = 3 * PRIMARY_WIDTH + DECAY_LORA + AAA_LORA + GATE_LORA
RWKV_GN_EPS = HEAD_DIM * 1e-5
NORM_EPS = 1e-6
NEG = -1e30
FORCE = 1e6
ATTN_SCALE = HEAD_DIM ** -0.5
LOG2E = float(np.log2(np.e))
SAFE_LOG2 = 80.0

LANES = 128
SUBLANES = 8
VMEM_LIMIT_BYTES = 48 * 1024 * 1024

ROW_TILE = 512
ATTN_TILE = 256
SCAN_CHUNK = 64
SCAN_CHUNKS_PER_STEP = 2
INV_BASE = 8
SEL_SPAN = 2
SEL_UNROLL = 4
WIN_TILES = WINDOW // ATTN_TILE + 1
N_FORCED = 3
ACC_ROWS = HEAD_DIM + 16
PREP_TILE = 256

_NT = (((1,), (1,)), ((), ()))
_TN = (((0,), (0,)), ((), ()))


def _params(*sem):
    return pltpu.CompilerParams(dimension_semantics=sem, vmem_limit_bytes=VMEM_LIMIT_BYTES)


def _rms(x, g):
    return x * lax.rsqrt(jnp.mean(x * x, axis=-1, keepdims=True) + NORM_EPS) * g


def _dot(a, b):
    return jnp.dot(a, b, preferred_element_type=F32)


def _dot_nt(a, b):
    return lax.dot_general(a, b, _NT, preferred_element_type=F32)


def _norm_cast_kernel(x_ref, g_ref, o_ref):
    o_ref[...] = _rms(x_ref[...], g_ref[...]).astype(o_ref.dtype)


def _norm_cast(x, g):
    t, d = x.shape
    tm = min(ROW_TILE, t)
    return pl.pallas_call(
        _norm_cast_kernel,
        grid=(t // tm,),
        in_specs=[pl.BlockSpec((tm, d), lambda i: (i, 0)), pl.BlockSpec((1, d), lambda i: (0, 0))],
        out_specs=pl.BlockSpec((tm, d), lambda i: (i, 0)),
        out_shape=jax.ShapeDtypeStruct((t, d), BF16),
        compiler_params=_params("parallel"),
        name="norm_cast",
    )(x, g.reshape(1, d))


def _proj_kernel(x_ref, w_ref, o_ref):
    o_ref[...] = _dot(x_ref[...], w_ref[...]).astype(o_ref.dtype)


def _proj_parts_kernel(x_ref, w_ref, *o_refs, starts):
    y = _dot(x_ref[...], w_ref[...])
    for o_ref, start in zip(o_refs, starts):
        o_ref[...] = y[:, start:start + o_ref.shape[1]].astype(o_ref.dtype)


def _proj_parts(x, parts, name):
    t, d = x.shape
    tm = min(ROW_TILE, t)
    padded = [jnp.pad(w, ((0, 0), (0, -w.shape[1] % LANES))) for w, _ in parts]
    starts = tuple(int(s) for s in np.cumsum([0] + [w.shape[1] for w in padded[:-1]]))
    w_all = jnp.concatenate(padded, axis=1).astype(BF16)
    return pl.pallas_call(
        functools.partial(_proj_parts_kernel, starts=starts),
        grid=(t // tm,),
        in_specs=[pl.BlockSpec((tm, d), lambda i: (i, 0)), pl.BlockSpec(w_all.shape, lambda i: (0, 0))],
        out_specs=[pl.BlockSpec((tm, w.shape[1]), lambda i: (i, 0)) for w, _ in parts],
        out_shape=[jax.ShapeDtypeStruct((t, w.shape[1]), dt) for w, dt in parts],
        compiler_params=_params("parallel"),
        name=name,
    )(x, w_all)


def _proj_rope_kernel(x_ref, w_ref, c_ref, sp_ref, sm_ref, scale_ref, o_ref):
    y = _dot(x_ref[...], w_ref[...])
    width = y.shape[1]
    reps = width // LANES
    cos = jnp.concatenate([c_ref[...]] * reps, axis=1)
    sin_hi = jnp.concatenate([sp_ref[...]] * reps, axis=1)
    sin_lo = jnp.concatenate([sm_ref[...]] * reps, axis=1)
    y = y * cos + pltpu.roll(y, ROT_HALF, 1) * sin_hi + pltpu.roll(y, width - ROT_HALF, 1) * sin_lo
    o_ref[...] = (y * scale_ref[...]).astype(o_ref.dtype)


def _proj_t_kernel(x_ref, wt_ref, b_ref, o_ref, *, gate):
    y = _dot_nt(wt_ref[...], x_ref[...])
    if gate:
        y = jax.nn.sigmoid(y + b_ref[...])
    o_ref[0] = y.astype(o_ref.dtype)


def _proj(x, w, *, tn, out_dtype, rope=None, name):
    t, d = x.shape
    n = w.shape[1]
    tm = min(ROW_TILE, t)
    in_specs = [
        pl.BlockSpec((tm, d), lambda i, j: (i, 0)),
        pl.BlockSpec((d, tn), lambda i, j: (0, j)),
    ]
    args = [x, w.astype(BF16)]
    if rope is None:
        body = _proj_kernel
    else:
        body = _proj_rope_kernel
        cos, sin_hi, sin_lo, scale = rope
        in_specs += [pl.BlockSpec((tm, LANES), lambda i, j: (i, 0))] * 3
        in_specs += [pl.BlockSpec((1, tn), lambda i, j: (0, j))]
        args += [cos, sin_hi, sin_lo, scale]
    return pl.pallas_call(
        body,
        grid=(t // tm, n // tn),
        in_specs=in_specs,
        out_specs=pl.BlockSpec((tm, tn), lambda i, j: (i, j)),
        out_shape=jax.ShapeDtypeStruct((t, n), out_dtype),
        compiler_params=_params("parallel", "parallel"),
        name=name,
    )(*args)


def _proj_t(x, wt, bias, *, tm, tn, out_dtype, gate, name):
    t, d = x.shape
    n = wt.shape[0]
    return pl.pallas_call(
        functools.partial(_proj_t_kernel, gate=gate),
        grid=(t // tm, n // tn),
        in_specs=[
            pl.BlockSpec((tm, d), lambda i, j: (i, 0)),
            pl.BlockSpec((tn, d), lambda i, j: (j, 0)),
            pl.BlockSpec((tn, 1), lambda i, j: (j, 0)),
        ],
        out_specs=pl.BlockSpec((1, tn, tm), lambda i, j: (i, j, 0)),
        out_shape=jax.ShapeDtypeStruct((t // tm, n, tm), out_dtype),
        compiler_params=_params("parallel", "parallel"),
        name=name,
    )(x, wt.astype(BF16), bias)


def _outproj_kernel(a_ref, b_ref, wa_ref, wb_ref, g_ref, h_ref, o_ref):
    y = _dot(a_ref[...], wa_ref[...]) + _dot(b_ref[...], wb_ref[...])
    o_ref[...] = h_ref[...] + _rms(y, g_ref[...])


def _out_proj(prim, mo, w_out, g, h):
    t, d = h.shape
    tm = min(ROW_TILE, t)
    pw = prim.shape[1]
    return pl.pallas_call(
        _outproj_kernel,
        grid=(t // tm,),
        in_specs=[
            pl.BlockSpec((tm, pw), lambda i: (i, 0)),
            pl.BlockSpec((tm, MEM_WIDTH), lambda i: (i, 0)),
            pl.BlockSpec((pw, d), lambda i: (0, 0)),
            pl.BlockSpec((MEM_WIDTH, d), lambda i: (0, 0)),
            pl.BlockSpec((1, d), lambda i: (0, 0)),
            pl.BlockSpec((tm, d), lambda i: (i, 0)),
        ],
        out_specs=pl.BlockSpec((tm, d), lambda i: (i, 0)),
        out_shape=jax.ShapeDtypeStruct((t, d), F32),
        compiler_params=_params("parallel"),
        name="out_proj",
    )(prim, mo, w_out[:pw].astype(BF16), w_out[pw:].astype(BF16), g.reshape(1, d), h)


def _ffn_kernel(h_ref, g1_ref, wg_ref, wu_ref, wo_ref, g2_ref, *rest, feeds_next):
    if feeds_next:
        g3_ref, o_ref, next_ref, hn_ref, acc_ref = rest
    else:
        o_ref, hn_ref, acc_ref = rest
    j = pl.program_id(1)

    @pl.when(j == 0)
    def _():
        hn_ref[...] = _rms(h_ref[...], g1_ref[...]).astype(BF16)
        acc_ref[...] = jnp.zeros_like(acc_ref)

    hn = hn_ref[...]
    gate = _dot(hn, wg_ref[...])
    up = _dot(hn, wu_ref[...])
    act = (jax.nn.silu(gate) * up).astype(BF16)
    acc_ref[...] += _dot(act, wo_ref[...])

    @pl.when(j == pl.num_programs(1) - 1)
    def _():
        out = h_ref[...] + _rms(acc_ref[...], g2_ref[...])
        o_ref[...] = out
        if feeds_next:
            next_ref[...] = _rms(out, g3_ref[...]).astype(next_ref.dtype)


def _ffn_chunk(hidden):
    units = hidden // LANES
    for parts in range(2, units + 1):
        if units % parts == 0:
            return (units // parts) * LANES
    return hidden


def _ffn(h, g1, w_in, w_out, g2, next_g=None):
    t, d = h.shape
    hidden = w_out.shape[0]
    th = _ffn_chunk(hidden)
    nh = hidden // th
    tm = min(ROW_TILE, t)
    w_in = w_in.astype(BF16)
    row = pl.BlockSpec((tm, d), lambda i, j: (i, 0))
    gain = pl.BlockSpec((1, d), lambda i, j: (0, 0))
    feeds_next = next_g is not None
    in_specs = [row, gain,
                pl.BlockSpec((d, th), lambda i, j: (0, j)),
                pl.BlockSpec((d, th), lambda i, j: (0, j + nh)),
                pl.BlockSpec((th, d), lambda i, j: (j, 0)),
                gain]
    args = [h, g1.reshape(1, d), w_in, w_in, w_out.astype(BF16), g2.reshape(1, d)]
    out_specs, out_shape = row, jax.ShapeDtypeStruct((t, d), F32)
    if feeds_next:
        in_specs.append(gain)
        args.append(next_g.reshape(1, d))
        out_specs, out_shape = [row, row], [out_shape, jax.ShapeDtypeStruct((t, d), BF16)]
    return pl.pallas_call(
        functools.partial(_ffn_kernel, feeds_next=feeds_next),
        grid=(t // tm, nh),
        in_specs=in_specs,
        out_specs=out_specs,
        out_shape=out_shape,
        scratch_shapes=[pltpu.VMEM((tm, d), BF16), pltpu.VMEM((tm, d), F32)],
        compiler_params=_params("parallel", "arbitrary"),
        name="ffn",
    )(*args)


def _mem_attn_kernel(q_ref, mk_ref, mv_ref, o_ref):
    q = q_ref[...]
    mk = mk_ref[0]
    mv = mv_ref[0]
    head_of_lane = lax.broadcasted_iota(jnp.int32, mk.shape, 1) // HEAD_DIM
    acc = jnp.zeros(q.shape, F32)
    for h in range(MEM_HEADS):
        s = _dot_nt(q, jnp.where(head_of_lane == h, mk, 0)) * ATTN_SCALE
        e = jnp.exp(s - jnp.max(s, axis=-1, keepdims=True))
        p = e / jnp.sum(e, axis=-1, keepdims=True)
        acc = acc + _dot(p.astype(BF16), jnp.where(head_of_lane == h, mv, 0))
    o_ref[...] = acc.astype(o_ref.dtype)


def _mem_attn(q_mem, mem_k, mem_v, seq):
    t = q_mem.shape[0]
    m = mem_k.shape[1]
    tm = min(ROW_TILE, seq)
    per_seq = seq // tm
    return pl.pallas_call(
        _mem_attn_kernel,
        grid=(t // tm,),
        in_specs=[
            pl.BlockSpec((tm, MEM_WIDTH), lambda i: (i, 0)),
            pl.BlockSpec((1, m, MEM_WIDTH), lambda i: (i // per_seq, 0, 0)),
            pl.BlockSpec((1, m, MEM_WIDTH), lambda i: (i // per_seq, 0, 0)),
        ],
        out_specs=pl.BlockSpec((tm, MEM_WIDTH), lambda i: (i, 0)),
        out_shape=jax.ShapeDtypeStruct((t, MEM_WIDTH), BF16),
        compiler_params=_params("parallel"),
        name="mem_attn",
    )(q_mem, mem_k, mem_v)


def _compress_kernel(xk_ref, xv_ref, pk_ref, pv_ref, k1a_ref, k1b_ref, k2_ref, v1a_ref, v1b_ref, v2t_ref,
                     c_ref, s_ref, kc_ref, vct_ref):
    def hidden(x_ref, pos_ref, wa_ref, wb_ref):
        x = x_ref[0]
        n = x.shape[0]
        first = _dot((x + pos_ref[0:1, :]).astype(BF16), wa_ref[...])
        second = _dot((x + pos_ref[1:2, :]).astype(BF16), wb_ref[...])
        return jax.nn.gelu(first + pltpu.roll(second, n - 1, 0)).astype(BF16)

    hk = hidden(xk_ref, pk_ref, k1a_ref, k1b_ref)
    both = _dot(hk, k2_ref[...])
    kc_ref[0, 0] = (both[:, :LANES] * c_ref[0] + both[:, LANES:] * s_ref[0]).astype(kc_ref.dtype)
    hv = hidden(xv_ref, pv_ref, v1a_ref, v1b_ref)
    vct_ref[0, 0] = _dot_nt(v2t_ref[...], hv).astype(vct_ref.dtype)


def _rope_partner_cols(w):
    d = np.arange(w.shape[1]) % HEAD_DIM
    src = np.where(d < ROT_HALF, np.arange(w.shape[1]) + ROT_HALF, np.arange(w.shape[1]) - ROT_HALF)
    src = np.clip(src, 0, w.shape[1] - 1)
    sign = np.where(d < ROT_HALF, -1.0, np.where(d < ROT_DIM, 1.0, 0.0)).astype(np.float32)
    return w[:, src] * sign


def _compress(raw_k, raw_v, pos_k, pos_v, k_w1, k_w2, v_w1, v_w2, cos_c, sin_c, batch, seq):
    n_chunk = seq // CMP_STRIDE
    feat = CMP_STRIDE * KV_WIDTH
    hid = k_w1.shape[1]

    chunks = lambda raw: raw.reshape(batch, n_chunk, feat)

    def per_group(w_half):
        w = w_half.astype(BF16).reshape(CMP_STRIDE, 1, 1, HEAD_DIM, hid)
        own = (np.arange(KV_GROUPS)[:, None] == np.arange(KV_GROUPS)[None, :]).reshape(1, KV_GROUPS, KV_GROUPS, 1, 1)
        return jnp.where(own, w, jnp.zeros((), BF16)).transpose(1, 0, 2, 3, 4).reshape(KV_GROUPS, feat, hid)

    def pos_rows(pos):
        p = pos.reshape(2, CMP_STRIDE, 1, HEAD_DIM)
        return jnp.broadcast_to(p, (2, CMP_STRIDE, KV_GROUPS, HEAD_DIM)).reshape(2, feat)

    half_rows = CMP_STRIDE * HEAD_DIM
    zeros = jnp.zeros((hid, LANES - HEAD_DIM), F32)
    k2 = jnp.concatenate([k_w2, zeros, _rope_partner_cols(k_w2), zeros], axis=1).astype(BF16)
    x_spec = pl.BlockSpec((1, n_chunk, feat), lambda b, g: (b, 0, 0))
    pos_spec = pl.BlockSpec((2, feat), lambda b, g: (0, 0))
    w1_spec = pl.BlockSpec((None, feat, hid), lambda b, g: (g, 0, 0))
    tab_spec = pl.BlockSpec((1, n_chunk, LANES), lambda b, g: (b, 0, 0))
    return pl.pallas_call(
        _compress_kernel,
        grid=(batch, KV_GROUPS),
        in_specs=[x_spec, x_spec, pos_spec, pos_spec, w1_spec, w1_spec,
                  pl.BlockSpec((hid, 2 * LANES), lambda b, g: (0, 0)),
                  w1_spec, w1_spec,
                  pl.BlockSpec((HEAD_DIM, hid), lambda b, g: (0, 0)),
                  tab_spec, tab_spec],
        out_specs=[pl.BlockSpec((1, 1, n_chunk, LANES), lambda b, g: (b, g, 0, 0)),
                   pl.BlockSpec((1, 1, HEAD_DIM, n_chunk), lambda b, g: (b, g, 0, 0))],
        out_shape=[jax.ShapeDtypeStruct((batch, KV_GROUPS, n_chunk, LANES), BF16),
                   jax.ShapeDtypeStruct((batch, KV_GROUPS, HEAD_DIM, n_chunk), BF16)],
        compiler_params=_params("parallel", "parallel"),
        name="nsa_compress",
    )(chunks(raw_k), chunks(raw_v), pos_rows(pos_k), pos_rows(pos_v),
      per_group(k_w1[:half_rows]), per_group(k_w1[half_rows:]), k2,
      per_group(v_w1[:half_rows]), per_group(v_w1[half_rows:]), v_w2.T.astype(BF16), cos_c, sin_c)


def _split_dot_left(w_bf16, x):
    hi = x.astype(BF16)
    rem = x - hi.astype(F32)
    mid = rem.astype(BF16)
    lo = (rem - mid.astype(F32)).astype(BF16)
    return _dot(w_bf16, hi) + _dot(w_bf16, mid) + _dot(w_bf16, lo)


def _nsa_attn_kernel(q_ref, kk_ref, hot_ref, vt_ref, kc_ref, vct_ref, tap_ref, winb_ref, gate_ref, o_ref,
                     qx_ref, m_ref, acc_ref, kmax_ref):
    tq = q_ref.shape[0]
    qt = pl.program_id(2)
    t0 = qt * tq
    heads = range(GROUP_HEADS)

    q2 = q_ref[...]
    qs = jnp.concatenate([q2[:, h * LANES:(h + 1) * LANES] for h in heads], axis=0)
    lane = lax.broadcasted_iota(jnp.int32, qs.shape, 1)
    q_sel = jnp.where(lane < HEAD_DIM, qs, 0)
    q_win = jnp.where(lane >= HEAD_DIM, qs, 0)

    col_t = t0 + lax.broadcasted_iota(jnp.int32, (1, tq), 1)

    @pl.when(qt == 0)
    def _():
        half = lax.broadcasted_iota(jnp.int32, (1, LANES), 1) < HEAD_DIM
        k_abs = jnp.abs(kk_ref[...].astype(F32))
        kc_abs = jnp.abs(kc_ref[0, 0].astype(F32))
        row_sums = [jnp.sum(jnp.where(half, k_abs, 0.0), axis=1, keepdims=True),
                    jnp.sum(jnp.where(half, 0.0, k_abs), axis=1, keepdims=True),
                    jnp.sum(kc_abs, axis=1, keepdims=True)]
        kmax_ref[0] = functools.reduce(jnp.maximum, [jnp.max(s) for s in row_sums])
    bounded = jnp.max(jnp.abs(qs.astype(F32))) * kmax_ref[0] <= SAFE_LOG2

    @pl.when(bounded)
    def _():
        _nsa_attend(False, q_sel, q_win, col_t, qt, kk_ref, hot_ref, vt_ref, kc_ref, vct_ref, tap_ref, winb_ref,
                    gate_ref, o_ref, qx_ref, m_ref, acc_ref)

    @pl.when(jnp.logical_not(bounded))
    def _():
        _nsa_attend(True, q_sel, q_win, col_t, qt, kk_ref, hot_ref, vt_ref, kc_ref, vct_ref, tap_ref, winb_ref,
                    gate_ref, o_ref, qx_ref, m_ref, acc_ref)


def _nsa_attend(stabilized, q_sel, q_win, col_t, qt, kk_ref, hot_ref, vt_ref, kc_ref, vct_ref, tap_ref, winb_ref,
                gate_ref, o_ref, qx_ref, m_ref, acc_ref):
    tq = col_t.shape[1]
    tk = tq
    heads = range(GROUP_HEADS)
    head_rows = lambda h: slice(h * tq, (h + 1) * tq)

    def weights(scores):
        if not stabilized:
            return jnp.exp2(scores), None
        top = jnp.max(scores, axis=0, keepdims=True)
        return jnp.exp2(scores - top), top > 0.5 * NEG

    n_cmp = kc_ref.shape[2]
    blk_end = lax.broadcasted_iota(jnp.int32, (n_cmp, 1), 0) * CMP_STRIDE + (CMP_BLOCK - 1)
    valid = blk_end <= col_t
    cmp_scores = [_dot_nt(kc_ref[0, 0], q_sel[head_rows(h)]) for h in heads]
    o_cmp = []
    p_sum = jnp.zeros((n_cmp, tq), F32)
    for h in heads:
        e, live = weights(jnp.where(valid, cmp_scores[h], NEG))
        total = jnp.sum(e, axis=0, keepdims=True)
        live = total > 0.0 if live is None else live
        p = e * jnp.where(live, 1.0 / total, 0.0)
        o_cmp.append(_dot(vct_ref[0, 0], p.astype(BF16)))
        p_sum = p_sum + p
    imp = _split_dot_left(tap_ref[...], p_sum)

    n_sel = imp.shape[0]
    blk = lax.broadcasted_iota(jnp.int32, (n_sel, tq), 0)
    blk_f = blk.astype(F32)
    cur = col_t // SEL_BLOCK
    forced = (blk == 0) | (blk == cur) | (blk == cur - 1)
    work = jnp.where(forced, -jnp.inf, jnp.where(blk <= cur, imp, -FORCE))
    for _ in range(max(min(SEL_TOPK, n_sel) - N_FORCED, 0)):
        best = jnp.max(work, axis=0, keepdims=True)
        hit = blk_f == jnp.min(jnp.where(work == best, blk_f, float(n_sel)), axis=0, keepdims=True)
        work = jnp.where(hit, -jnp.inf, work)
    chosen = jnp.where(work == -jnp.inf, 1.0, 0.0)
    if n_sel < LANES:
        chosen = jnp.concatenate([chosen, jnp.zeros((LANES - n_sel, tq), F32)], axis=0)
    block_bias = ((chosen - 1.0) * (-NEG)).T.astype(BF16)
    qx_ref[:, 0:LANES] = q_sel
    qx_ref[:, LANES:2 * LANES] = jnp.concatenate([block_bias] * GROUP_HEADS, axis=0)

    def key_pos(kt, n):
        return kt * tk + lax.broadcasted_iota(jnp.int32, (n * tk, 1), 0)

    def key_rows(ref, kt, n):
        return ref[pl.ds(pl.multiple_of(kt * tk, tk), n * tk), :]

    def value_rows(kt, n, first_row):
        vals = jnp.concatenate([vt_ref[kt + i, first_row:first_row + HEAD_DIM, :] for i in range(n)], axis=1)
        return jnp.concatenate([vals, jnp.ones((ACC_ROWS - HEAD_DIM, n * tk), BF16)], axis=0)

    def normalized(acc):
        return acc[0:HEAD_DIM] / acc[HEAD_DIM:HEAD_DIM + 1]

    win_start = jnp.maximum(qt - (WIN_TILES - 1), 0)
    win_bias = winb_ref[qt - win_start]
    win_keys = key_rows(kk_ref, win_start, WIN_TILES)
    win_vals = value_rows(win_start, WIN_TILES, HEAD_DIM)
    win_scores = [_dot_nt(win_keys, q_win[head_rows(h)]) for h in heads]
    o_win = [normalized(_dot(win_vals, weights(win_scores[h] + win_bias)[0].astype(BF16))) for h in heads]

    acc_ref[...] = jnp.zeros(acc_ref.shape, F32)
    full_spans = qt // SEL_SPAN

    def score_span(i):
        kt = i * SEL_SPAN
        keys = jnp.concatenate([key_rows(kk_ref, kt, SEL_SPAN), key_rows(hot_ref, kt, SEL_SPAN)], axis=1)
        return tuple(_dot_nt(keys, qx_ref[head_rows(h), :]) for h in heads)

    def causal_scores(i, scores):
        return jnp.where(key_pos(i * SEL_SPAN, SEL_SPAN) <= col_t, scores, NEG)

    if not stabilized:
        def plain_spans(first, count, causal_last):
            scores = [score_span(first + s) for s in range(count)]
            for s in range(count):
                v_ext = value_rows((first + s) * SEL_SPAN, SEL_SPAN, 0)
                for h in heads:
                    masked = causal_last and s == count - 1
                    e = jnp.exp2(causal_scores(first + s, scores[s][h]) if masked else scores[s][h]).astype(BF16)
                    acc_ref[:, head_rows(h)] += _dot(v_ext, e)

        def plain_step(i, carry):
            plain_spans(i * SEL_UNROLL, SEL_UNROLL, False)
            return carry

        whole = full_spans // SEL_UNROLL
        lax.fori_loop(0, whole, plain_step, 0)
        for left in range(SEL_UNROLL):
            @pl.when(full_spans - whole * SEL_UNROLL == left)
            def _():
                plain_spans(whole * SEL_UNROLL, left + 1, True)
    else:
        m_ref[...] = jnp.full(m_ref.shape, NEG, F32)

        def flash_span(i, span_scores, causal):
            v_ext = value_rows(i * SEL_SPAN, SEL_SPAN, 0)
            for h in heads:
                cols = head_rows(h)
                scores = causal_scores(i, span_scores[h]) if causal else span_scores[h]
                m_old = m_ref[:, cols]
                m_new = jnp.maximum(m_old, jnp.max(scores, axis=0, keepdims=True))
                e = jnp.exp2(scores - m_new).astype(BF16)
                acc_ref[:, cols] = jnp.exp2(m_old - m_new) * acc_ref[:, cols] + _dot(v_ext, e)
                m_ref[:, cols] = m_new

        def sel_step(i, span_scores):
            following = score_span(i + 1)
            flash_span(i, span_scores, False)
            return following

        last_scores = lax.fori_loop(0, full_spans, sel_step, score_span(0))
        flash_span(full_spans, last_scores, True)

    gates = gate_ref[...]
    outs = []
    for h in heads:
        outs.append(gates[3 * h:3 * h + 1, :] * o_cmp[h]
                    + gates[3 * h + 1:3 * h + 2, :] * normalized(acc_ref[:, head_rows(h)])
                    + gates[3 * h + 2:3 * h + 3, :] * o_win[h])
    o_ref[...] = jnp.concatenate(outs, axis=0).T.astype(o_ref.dtype)


def _tap_matrix(n_sel, n_chunk):
    ratio = SEL_BLOCK // CMP_STRIDE
    tap = np.zeros((n_sel, n_chunk), np.float32)
    n_cmp = n_chunk - (CMP_BLOCK // CMP_STRIDE - 1)
    for j in range(n_sel):
        for n in range(n_cmp):
            lo = max(n * CMP_STRIDE, j * SEL_BLOCK)
            hi = min(n * CMP_STRIDE + CMP_BLOCK, (j + 1) * SEL_BLOCK)
            if hi > lo:
                tap[j, n] = (hi - lo) / CMP_STRIDE
    return jnp.asarray(tap, BF16)


def _nsa_attention(qk, vt, gates, kc, vct, batch, seq):
    tq = ATTN_TILE
    n_tiles = seq // tq
    n_chunk = kc.shape[2]
    n_sel = seq // SEL_BLOCK
    q_blocks = GROUP_HEADS * LANES
    k_col0 = KV_GROUPS * q_blocks // LANES
    vt5 = vt.reshape(batch, n_tiles, KV_GROUPS, 2 * HEAD_DIM, tq)
    gates5 = gates.reshape(batch, n_tiles, gates.shape[1] // 16, 16, tq)
    width = GROUP_HEADS * tq
    assert n_sel <= LANES, "block one-hot is one lane tile wide"
    assert n_tiles % SEL_SPAN == 0 and n_tiles >= WIN_TILES
    hot = jnp.asarray((np.arange(seq)[:, None] // SEL_BLOCK) == np.arange(LANES)[None, :], BF16)
    dist = (np.arange(WIN_TILES)[:, None, None] * tq + np.arange(tq)[None, None, :]
            - np.arange(WIN_TILES * tq)[None, :, None])
    win_bias = jnp.asarray(np.where((dist >= 0) & (dist < WINDOW), 0.0, NEG), F32)
    return pl.pallas_call(
        _nsa_attn_kernel,
        grid=(batch, KV_GROUPS, n_tiles),
        in_specs=[
            pl.BlockSpec((tq, q_blocks), lambda b, g, i: (b * n_tiles + i, g)),
            pl.BlockSpec((seq, LANES), lambda b, g, i: (b, k_col0 + g)),
            pl.BlockSpec((seq, LANES), lambda b, g, i: (0, 0)),
            pl.BlockSpec((None, n_tiles, None, 2 * HEAD_DIM, tq), lambda b, g, i: (b, 0, g, 0, 0)),
            pl.BlockSpec((1, 1, n_chunk, LANES), lambda b, g, i: (b, g, 0, 0)),
            pl.BlockSpec((1, 1, HEAD_DIM, n_chunk), lambda b, g, i: (b, g, 0, 0)),
            pl.BlockSpec((n_sel, n_chunk), lambda b, g, i: (0, 0)),
            pl.BlockSpec((WIN_TILES, WIN_TILES * tq, tq), lambda b, g, i: (0, 0, 0)),
            pl.BlockSpec((None, None, None, 16, tq), lambda b, g, i: (b, i, g, 0, 0)),
        ],
        out_specs=pl.BlockSpec((tq, GROUP_WIDTH), lambda b, g, i: (b * n_tiles + i, g)),
        out_shape=jax.ShapeDtypeStruct((batch * seq, PRIMARY_WIDTH), BF16),
        scratch_shapes=[
            pltpu.VMEM((width, 2 * LANES), BF16),
            pltpu.VMEM((1, width), F32),
            pltpu.VMEM((ACC_ROWS, width), F32),
            pltpu.SMEM((1,), F32),
        ],
        compiler_params=_params("parallel", "parallel", "arbitrary"),
        name="nsa_attention",
    )(qk, qk, hot, vt5, kc, vct, _tap_matrix(n_sel, n_chunk), win_bias, gates5)


def _rope_tables(positions):
    inv = ROPE_THETA ** (-jnp.arange(0, ROT_DIM, 2, dtype=F32) / ROT_DIM)
    ang = positions.astype(F32)[..., None] * inv
    cos, sin = lax.optimization_barrier((jnp.cos(ang), jnp.sin(ang)))
    reps = (1,) * (ang.ndim - 1) + (LANES // ROT_HALF,)
    cos_t, sin_t = jnp.tile(cos, reps), jnp.tile(sin, reps)
    dim = jnp.arange(LANES) % HEAD_DIM
    lo, hi = dim < ROT_HALF, (dim >= ROT_HALF) & (dim < ROT_DIM)
    return (jnp.where(lo | hi, cos_t, 1.0), jnp.where(hi, sin_t, 0.0), jnp.where(lo, -sin_t, 0.0),
            jnp.where(lo | hi, sin_t, 0.0))


def _nsa_layer(hn, w_in, gate_b, pos_k, pos_v, k_w1, k_w2, v_w1, v_w2, positions, batch, seq):
    t, d = hn.shape
    pw, kw = PRIMARY_WIDTH, KV_WIDTH
    offs = np.cumsum([0, pw] + [kw] * 6 + [3 * KV_GROUPS * GROUP_HEADS, MEM_WIDTH])
    col = lambda i: w_in[:, offs[i]:offs[i + 1]]
    w_q, w_kc, w_vc, w_ks, w_vs, w_kw, w_vw, w_gl, w_qm = [col(i) for i in range(9)]

    wq_h = w_q.reshape(d, KV_GROUPS * GROUP_HEADS, 1, HEAD_DIM)
    wq_dup = jnp.broadcast_to(wq_h, (d, KV_GROUPS * GROUP_HEADS, 2, HEAD_DIM)).reshape(d, 2 * pw)
    wk_pair = jnp.stack([w_ks.reshape(d, KV_GROUPS, HEAD_DIM), w_kw.reshape(d, KV_GROUPS, HEAD_DIM)], axis=2)
    w_rope = jnp.concatenate([wq_dup, wk_pair.reshape(d, 2 * kw)], axis=1)
    scale = jnp.concatenate([jnp.full((1, 2 * pw), ATTN_SCALE * LOG2E, F32), jnp.ones((1, 2 * kw), F32)], axis=1)
    cos, sin_hi, sin_lo, _ = _rope_tables(positions.reshape(t))
    qk = _proj(hn, w_rope, tn=w_rope.shape[1] // 3, out_dtype=BF16,
               rope=(cos, sin_hi, sin_lo, scale), name="nsa_proj_rope")

    wv_pair = jnp.stack([w_vs.reshape(d, KV_GROUPS, HEAD_DIM), w_vw.reshape(d, KV_GROUPS, HEAD_DIM)], axis=2)
    wv_t = wv_pair.reshape(d, 2 * kw).T
    vt = _proj_t(hn, wv_t, jnp.zeros((2 * kw, 1), F32), tm=ATTN_TILE, tn=2 * kw,
                 out_dtype=BF16, gate=False, name="nsa_proj_values")

    per_group = 3 * GROUP_HEADS
    wg = jnp.pad(w_gl.reshape(d, KV_GROUPS, per_group), ((0, 0), (0, 0), (0, 16 - per_group)))
    bg = jnp.pad(gate_b.reshape(KV_GROUPS, per_group), ((0, 0), (0, 16 - per_group)))
    n_gate = LANES
    wg_t = jnp.pad(wg.reshape(d, KV_GROUPS * 16).T, ((0, n_gate - KV_GROUPS * 16), (0, 0)))
    bg_t = jnp.pad(bg.reshape(KV_GROUPS * 16, 1), ((0, n_gate - KV_GROUPS * 16), (0, 0)))
    gates = _proj_t(hn, wg_t, bg_t, tm=ATTN_TILE, tn=n_gate, out_dtype=F32, gate=True,
                    name="nsa_proj_gates")

    q_mem, raw_k, raw_v = _proj_parts(hn, [(w_qm, BF16), (w_kc, F32), (w_vc, F32)], "nsa_proj_plain")

    n_chunk = seq // CMP_STRIDE
    cmp_end = jnp.minimum(jnp.arange(n_chunk) * CMP_STRIDE + CMP_BLOCK - 1, seq - 1)
    cos_c, _, _, sin_c = _rope_tables(positions[:, cmp_end])
    kc, vct = _compress(raw_k, raw_v, pos_k, pos_v, k_w1, k_w2, v_w1, v_w2, cos_c, sin_c, batch, seq)
    prim = _nsa_attention(qk, vt, gates, kc, vct, batch, seq)
    return prim, q_mem


def _split_dot(x, w_bf16):
    hi = x.astype(BF16)
    lo = (x - hi.astype(F32)).astype(BF16)
    return _dot(hi, w_bf16) + _dot(lo, w_bf16)


def _head_ones(width):
    head = np.arange(width) // HEAD_DIM
    return jnp.asarray(head[:, None] == head[None, :], BF16)


def _rwkv_prep_kernel(hn_ref, w_ref, mu_ref, w0_ref, w2_ref, a0_ref, a2_ref, g2_ref, kk_ref, ka_ref, rk_ref,
                      ones_ref, r_ref, k_ref, v_ref, na_ref, b_ref, lw_ref, g_ref, bonus_ref, last_ref,
                      *, tiles_per_seq):
    pw = PRIMARY_WIDTH
    @pl.when(pl.program_id(0) == 0)
    def _():
        last_ref[...] = jnp.zeros_like(last_ref)

    x = _dot(hn_ref[...], w_ref[...])
    first_tile = (pl.program_id(0) % tiles_per_seq) == 0
    last_prev = jnp.where(first_tile, 0.0, last_ref[0:1, :])
    row = lax.broadcasted_iota(jnp.int32, x.shape, 0)
    prev = jnp.where(row == 0, last_prev, pltpu.roll(x, 1, 0))
    last_ref[0:1, :] = x[x.shape[0] - 1:, :]
    xs = x + (prev - x) * mu_ref[...]
    r = xs[:, 0:pw]
    k = xs[:, pw:2 * pw]
    v = xs[:, 2 * pw:3 * pw]
    lora = xs[:, 3 * pw:3 * pw + DECAY_LORA + AAA_LORA]
    gl = xs[:, 3 * pw + DECAY_LORA + AAA_LORA:]
    w_pre = w0_ref[...] + _dot(jnp.tanh(lora).astype(BF16), w2_ref[...])
    z = -w_pre
    w = -(jnp.maximum(z, 0.0) + jnp.log1p(jnp.exp(-jnp.abs(z)))) - 0.5
    lw_ref[...] = -jnp.exp(w)
    a = jax.nn.sigmoid(a0_ref[...] + _dot(lora.astype(BF16), a2_ref[...]))
    g_ref[...] = _dot(jax.nn.sigmoid(gl).astype(BF16), g2_ref[...]).astype(g_ref.dtype)
    ones = ones_ref[...]

    def head_sums(y):
        gw = GROUP_WIDTH
        return jnp.concatenate([_split_dot(y[:, i * gw:(i + 1) * gw], ones) for i in range(KV_GROUPS)], axis=1)

    kk = k * kk_ref[...]
    kk = kk * lax.rsqrt(jnp.maximum(head_sums(kk * kk), 1e-24))
    k2 = k * (1.0 + (a - 1.0) * ka_ref[...])
    r_ref[...] = r
    k_ref[...] = k2
    v_ref[...] = v.astype(v_ref.dtype)
    na_ref[...] = -kk
    b_ref[...] = kk * a
    bonus_ref[...] = (head_sums(r * k2 * rk_ref[...]) * v).astype(bonus_ref.dtype)


def _rwkv_prep(hn, w_in, mu, w0, w2, a0, a2, g2, k_k, k_a, r_k, seq):
    t, d = hn.shape
    width = w_in.shape[1]
    pw = PRIMARY_WIDTH
    tm = min(PREP_TILE, seq)
    row = lambda a: a.reshape(1, -1)
    lora_w = DECAY_LORA + AAA_LORA
    w2e = jnp.concatenate([w2, jnp.zeros((AAA_LORA, pw), F32)], axis=0).astype(BF16)
    a2e = jnp.concatenate([jnp.zeros((DECAY_LORA, pw), F32), a2], axis=0).astype(BF16)
    const = lambda shape: pl.BlockSpec(shape, lambda i: (0, 0))
    out_spec = pl.BlockSpec((tm, pw), lambda i: (i, 0))
    return pl.pallas_call(
        functools.partial(_rwkv_prep_kernel, tiles_per_seq=seq // tm),
        grid=(t // tm,),
        in_specs=[
            pl.BlockSpec((tm, d), lambda i: (i, 0)),
            const((d, width)),
            const((1, width)), const((1, pw)), const((lora_w, pw)), const((1, pw)), const((lora_w, pw)),
            const((GATE_LORA, pw)), const((1, pw)), const((1, pw)), const((1, pw)),
            const((GROUP_WIDTH, GROUP_WIDTH)),
        ],
        out_specs=[out_spec] * 8,
        out_shape=[jax.ShapeDtypeStruct((t, pw), dt) for dt in (F32, F32, BF16, F32, F32, F32, BF16, BF16)],
        scratch_shapes=[pltpu.VMEM((SUBLANES, width), F32)],
        compiler_params=_params("arbitrary"),
        name="rwkv_prep",
    )(hn, w_in.astype(BF16), row(mu), row(w0), w2e, row(a0), a2e, g2.astype(BF16), row(k_k), row(k_a), row(r_k),
      _head_ones(GROUP_WIDTH))


def _block_diag(x, mask):
    return jnp.where(mask, jnp.concatenate([x.astype(BF16)] * GROUP_HEADS, axis=0), jnp.zeros((), BF16))


def _rwkv_scan_kernel(r_ref, k_ref, v_ref, na_ref, b_ref, lw_ref, g_ref, bonus_ref, lng_ref, lnb_ref, o_ref, state_ref):
    n_batch, rows, _ = r_ref.shape
    c = SCAN_CHUNK
    gw = GROUP_WIDTH
    pairs = [(bi, grp) for bi in range(n_batch) for grp in range(KV_GROUPS)]
    units = [(bi, grp, sub) for sub in range(rows // c) for bi, grp in pairs]
    every = lambda fn, *lists: [fn(*args) for args in zip(*lists)]

    @pl.when(pl.program_id(0) == 0)
    def _():
        state_ref[...] = jnp.zeros_like(state_ref)

    rr = lax.broadcasted_iota(jnp.int32, (gw, gw), 0)
    cc = lax.broadcasted_iota(jnp.int32, (gw, gw), 1)
    bd_mask = (rr // HEAD_DIM) == (cc // HEAD_DIM)
    t_idx = lax.broadcasted_iota(jnp.int32, (c, gw), 0)
    s_idx = lax.broadcasted_iota(jnp.int32, (c, gw), 1) % HEAD_DIM
    strict = t_idx > s_idx
    incl = t_idx >= s_idx
    eye = jnp.where(t_idx == s_idx, 1.0, 0.0)
    same_block = {}
    size = INV_BASE
    while size <= c:
        same_block[size] = (t_idx // size) == (s_idx // size)
        size *= 2
    tril =jnp.where(lax.broadcasted_iota(jnp.int32, (c, c), 0) >= lax.broadcasted_iota(jnp.int32, (c, c), 1),
                     1.0, 0.0).astype(BF16)
    ones_bd = jnp.where(bd_mask, 1.0, 0.0).astype(BF16)

    def cumsum_rows(x):
        hi = x.astype(BF16)
        rem = x - hi.astype(F32)
        mid = rem.astype(BF16)
        lo = (rem - mid.astype(F32)).astype(BF16)
        return _dot(tril, hi) + _dot(tril, mid) + _dot(tril, lo)

    bd = lambda x: _block_diag(x, bd_mask)
    cols = lambda grp: slice(grp * gw, (grp + 1) * gw)
    rows_of = lambda sub: slice(sub * c, (sub + 1) * c)
    load = lambda ref: [ref[bi, rows_of(sub), cols(grp)] for bi, grp, sub in units]
    r, k, v, na, bv, lw = (load(ref) for ref in (r_ref, k_ref, v_ref, na_ref, b_ref, lw_ref))

    cum = every(cumsum_rows, lw)
    p_incl = every(jnp.exp, cum)
    inv_p = every(lambda x: jnp.exp(-x), cum)
    b_t = every(lambda x, s: (x * s).astype(BF16), bv, inv_p)
    k_t = every(lambda x, s: (x * s).astype(BF16), k, inv_p)
    x2 = every(lambda a, cu, l, rr_, p: jnp.concatenate([a * jnp.exp(cu - l), rr_ * p], axis=0).astype(BF16),
               na, cum, lw, r, p_incl)

    g_b = every(lambda x, y: _dot_nt(x, bd(y)), x2, b_t)
    g_k = every(lambda x, y: _dot_nt(x, bd(y)), x2, k_t)
    l_ab = every(lambda g: jnp.where(strict, g[:c], 0.0), g_b)
    m_rb = every(lambda g: jnp.where(incl, g[c:], 0.0).astype(BF16), g_b)
    l_ak = every(lambda g: jnp.where(strict, g[:c], 0.0).astype(BF16), g_k)
    m_rk = every(lambda g: jnp.where(incl, g[c:], 0.0).astype(BF16), g_k)

    power = every(lambda l: jnp.where(same_block[INV_BASE], l, 0.0), l_ab)
    t_inv = every(lambda p: eye + p, power)
    for _ in range(int(np.log2(INV_BASE)) - 1):
        power = every(lambda p: _dot(p.astype(BF16), bd(p)), power)
        t_inv = every(lambda t, p: t + _dot(t.astype(BF16), bd(p)), t_inv, power)
    size = 2 * INV_BASE
    while size <= c:
        off_mask = same_block[size] & ~same_block[size // 2]
        half = every(lambda t, l: _dot(t.astype(BF16), bd(jnp.where(off_mask, l, 0.0))), t_inv, l_ab)
        t_inv = every(lambda t, hf: t + _dot(hf.astype(BF16), bd(t)), t_inv, half)
        size *= 2

    v_bd = every(bd, v)
    t_inv = every(lambda t: t.astype(BF16), t_inv)
    lk_v = every(_dot, l_ak, v_bd)
    mk_v = every(_dot, m_rk, v_bd)

    state = [state_ref[bi * KV_GROUPS + grp] for bi, grp in pairs]
    for sub in range(rows // c):
        of = lambda lst: lst[sub * len(pairs):(sub + 1) * len(pairs)]
        xh = every(lambda x, s: _dot_nt(x, s.astype(BF16)), of(x2), state)
        u = every(lambda t, x, lv: _dot(t, bd(x[:c] + lv)), of(t_inv), xh, of(lk_v))
        out = every(lambda x, mb, uu, mv: x[c:] + _dot(mb, bd(uu)) + mv, xh, of(m_rb), u, of(mk_v))
        delta = every(lambda uu, vv, b, kk_: lax.dot_general(
            jnp.concatenate([uu.astype(BF16), vv.astype(BF16)], axis=0), jnp.concatenate([b, kk_], axis=0), _TN,
            preferred_element_type=F32), u, of(v), of(b_t), of(k_t))
        state = every(lambda s, dl, p: (s + jnp.where(bd_mask, dl, 0.0)) * p[c - 1:c, :], state, delta, of(p_incl))
        mean = every(lambda o: _split_dot(o, ones_bd) * (1.0 / HEAD_DIM), out)
        dev = every(lambda o, m: o - m, out, mean)
        var = every(lambda dv: _split_dot(dv * dv, ones_bd) * (1.0 / HEAD_DIM), dev)
        for i, (bi, grp) in enumerate(pairs):
            y = dev[i] * lax.rsqrt(var[i] + RWKV_GN_EPS) * lng_ref[:, cols(grp)] + lnb_ref[:, cols(grp)]
            o_ref[bi, rows_of(sub), cols(grp)] = (
                (y + bonus_ref[bi, rows_of(sub), cols(grp)]) * g_ref[bi, rows_of(sub), cols(grp)]).astype(o_ref.dtype)
    for i, (bi, grp) in enumerate(pairs):
        state_ref[bi * KV_GROUPS + grp] = state[i]


def _rwkv_scan(r, k, v, na, b, lw, g, bonus, ln_g, ln_b, batch, seq):
    c = SCAN_CHUNK
    n_chunks = seq // c
    pw = PRIMARY_WIDTH
    assert GROUP_HEADS * c == GROUP_WIDTH, "block-diagonal packing needs a 64-token chunk"
    assert n_chunks % SCAN_CHUNKS_PER_STEP == 0
    blk = pl.BlockSpec((batch, SCAN_CHUNKS_PER_STEP * c, pw), lambda ci: (0, ci, 0))
    const = pl.BlockSpec((1, pw), lambda ci: (0, 0))
    per_batch = lambda a: a.reshape(batch, seq, pw)
    out = pl.pallas_call(
        _rwkv_scan_kernel,
        grid=(n_chunks // SCAN_CHUNKS_PER_STEP,),
        in_specs=[blk] * 8 + [const, const],
        out_specs=blk,
        out_shape=jax.ShapeDtypeStruct((batch, seq, pw), BF16),
        scratch_shapes=[pltpu.VMEM((batch * KV_GROUPS, GROUP_WIDTH, GROUP_WIDTH), F32)],
        compiler_params=_params("arbitrary"),
        name="rwkv_scan",
    )(*(per_batch(a) for a in (r, k, v, na, b, lw, g, bonus)), ln_g.reshape(1, pw), ln_b.reshape(1, pw))
    return out.reshape(batch * seq, pw)


def _rwkv_layer(hn, w_in, mu, w0, w2, a0, a2, g2, k_k, k_a, r_k, ln_g, ln_b, batch, seq):
    q_mem = _proj(hn, w_in[:, RWKV_SHIFT_W:], tn=MEM_WIDTH, out_dtype=BF16, name="rwkv_proj_mem")
    r, k, v, na, b, lw, g, bonus = _rwkv_prep(hn, w_in[:, :RWKV_SHIFT_W], mu, w0, w2, a0, a2, g2, k_k, k_a,
                                              r_k.reshape(-1), seq)
    prim = _rwkv_scan(r, k, v, na, b, lw, g, bonus, ln_g, ln_b, batch, seq)
    return prim, q_mem


def kernel(x, mem, positions, mem_norm_g, w_mem_kv, pre_mix_g, post_mix_g, pre_ffn_g, post_ffn_g, w_out, w_ffn_in, w_ffn_out, nsa_w_in, nsa_gate_b, nsa_cmp_pos_k, nsa_cmp_pos_v, nsa_cmp_k_w1, nsa_cmp_k_w2, nsa_cmp_v_w1, nsa_cmp_v_w2, rwkv_w_in, rwkv_mu, rwkv_w0, rwkv_w2, rwkv_a0, rwkv_a2, rwkv_g2, rwkv_k_k, rwkv_k_a, rwkv_r_k, rwkv_ln_g, rwkv_ln_b):
    batch, seq, d = x.shape
    n_mem = mem.shape[1]
    depth = pre_mix_g.shape[0]
    mkv = _proj(_norm_cast(mem.reshape(batch * n_mem, d), mem_norm_g), w_mem_kv, tn=w_mem_kv.shape[1],
                out_dtype=BF16, name="mem_kv")
    mem_k = mkv[:, :MEM_WIDTH].reshape(batch, n_mem, MEM_WIDTH)
    mem_v = mkv[:, MEM_WIDTH:].reshape(batch, n_mem, MEM_WIDTH)
    h = x.reshape(batch * seq, d)
    hn = _norm_cast(h, pre_mix_g[0])
    for i in range(depth):
        j = i // 2
        if i % 2 == 0:
            prim, q_mem = _nsa_layer(hn, nsa_w_in[j], nsa_gate_b[j], nsa_cmp_pos_k[j], nsa_cmp_pos_v[j],
                                     nsa_cmp_k_w1[j], nsa_cmp_k_w2[j], nsa_cmp_v_w1[j], nsa_cmp_v_w2[j],
                                     positions, batch, seq)
        else:
            prim, q_mem = _rwkv_layer(hn, rwkv_w_in[j], rwkv_mu[j], rwkv_w0[j], rwkv_w2[j], rwkv_a0[j],
                                      rwkv_a2[j], rwkv_g2[j], rwkv_k_k[j], rwkv_k_a[j], rwkv_r_k[j], rwkv_ln_g[j],
                                      rwkv_ln_b[j], batch, seq)
        mo = _mem_attn(q_mem, mem_k, mem_v, seq)
        h = _out_proj(prim, mo, w_out[i], post_mix_g[i], h)
        if i + 1 < depth:
            h, hn = _ffn(h, pre_ffn_g[i], w_ffn_in[i], w_ffn_out[i], post_ffn_g[i], next_g=pre_mix_g[i + 1])
        else:
            h = _ffn(h, pre_ffn_g[i], w_ffn_in[i], w_ffn_out[i], post_ffn_g[i])
    return h.reshape(batch, seq, d)
```

```python
import functools

import numpy as np
import jax
import jax.numpy as jnp
from jax import lax
from jax.experimental import pallas as pl
from jax.experimental.pallas import tpu as pltpu

F32 = jnp.float32
BF16 = jnp.bfloat16

HEAD_DIM = 64
ROT_DIM = HEAD_DIM // 4
ROT_HALF = ROT_DIM // 2
ROPE_THETA = 500000.0
MEM_HEADS = 4
MEM_WIDTH = MEM_HEADS * HEAD_DIM
KV_GROUPS = 3
GROUP_HEADS = 4
GROUP_WIDTH = GROUP_HEADS * HEAD_DIM
PRIMARY_WIDTH = KV_GROUPS * GROUP_WIDTH
KV_WIDTH = KV_GROUPS * HEAD_DIM
CMP_BLOCK = 32
CMP_STRIDE = 16
SEL_BLOCK = 64
SEL_TOPK = 16
WINDOW = 512
DECAY_LORA = 64
AAA_LORA = 64
GATE_LORA = 128
RWKV_SHIFT_W = 3 * PRIMARY_WIDTH + DECAY_LORA + AAA_LORA + GATE_LORA
RWKV_GN_EPS = HEAD_DIM * 1e-5
NORM_EPS = 1e-6
NEG = -1e30
FORCE = 1e6
ATTN_SCALE = HEAD_DIM ** -0.5
LOG2E = float(np.log2(np.e))
SAFE_LOG2 = 80.0

LANES = 128
SUBLANES = 8
VMEM_LIMIT_BYTES = 48 * 1024 * 1024

ROW_TILE = 512
ATTN_TILE = 256
SCAN_CHUNK = 64
SCAN_CHUNKS_PER_STEP = 2
INV_BASE = 8
SEL_SPAN = 2
SEL_UNROLL = 4
WIN_TILES = WINDOW // ATTN_TILE + 1
N_FORCED = 3
ACC_ROWS = HEAD_DIM + 16
PREP_TILE = 256

_NT = (((1,), (1,)), ((), ()))
_TN = (((0,), (0,)), ((), ()))


def _params(*sem):
    return pltpu.CompilerParams(dimension_semantics=sem, vmem_limit_bytes=VMEM_LIMIT_BYTES)


def _rms(x, g):
    return x * lax.rsqrt(jnp.mean(x * x, axis=-1, keepdims=True) + NORM_EPS) * g


def _dot(a, b):
    return jnp.dot(a, b, preferred_element_type=F32)


def _dot_nt(a, b):
    return lax.dot_general(a, b, _NT, preferred_element_type=F32)


def _norm_cast_kernel(x_ref, g_ref, o_ref):
    o_ref[...] = _rms(x_ref[...], g_ref[...]).astype(o_ref.dtype)


def _norm_cast(x, g):
    t, d = x.shape
    tm = min(ROW_TILE, t)
    return pl.pallas_call(
        _norm_cast_kernel,
        grid=(t // tm,),
        in_specs=[pl.BlockSpec((tm, d), lambda i: (i, 0)), pl.BlockSpec((1, d), lambda i: (0, 0))],
        out_specs=pl.BlockSpec((tm, d), lambda i: (i, 0)),
        out_shape=jax.ShapeDtypeStruct((t, d), BF16),
        compiler_params=_params("parallel"),
        name="norm_cast",
    )(x, g.reshape(1, d))


def _proj_kernel(x_ref, w_ref, o_ref):
    o_ref[...] = _dot(x_ref[...], w_ref[...]).astype(o_ref.dtype)


def _proj_parts_kernel(x_ref, w_ref, *o_refs, starts):
    y = _dot(x_ref[...], w_ref[...])
    for o_ref, start in zip(o_refs, starts):
        o_ref[...] = y[:, start:start + o_ref.shape[1]].astype(o_ref.dtype)


def _proj_parts(x, parts, name):
    t, d = x.shape
    tm = min(ROW_TILE, t)
    padded = [jnp.pad(w, ((0, 0), (0, -w.shape[1] % LANES))) for w, _ in parts]
    starts = tuple(int(s) for s in np.cumsum([0] + [w.shape[1] for w in padded[:-1]]))
    w_all = jnp.concatenate(padded, axis=1).astype(BF16)
    return pl.pallas_call(
        functools.partial(_proj_parts_kernel, starts=starts),
        grid=(t // tm,),
        in_specs=[pl.BlockSpec((tm, d), lambda i: (i, 0)), pl.BlockSpec(w_all.shape, lambda i: (0, 0))],
        out_specs=[pl.BlockSpec((tm, w.shape[1]), lambda i: (i, 0)) for w, _ in parts],
        out_shape=[jax.ShapeDtypeStruct((t, w.shape[1]), dt) for w, dt in parts],
        compiler_params=_params("parallel"),
        name=name,
    )(x, w_all)


def _proj_rope_kernel(x_ref, w_ref, c_ref, sp_ref, sm_ref, scale_ref, o_ref):
    y = _dot(x_ref[...], w_ref[...])
    width = y.shape[1]
    reps = width // LANES
    cos = jnp.concatenate([c_ref[...]] * reps, axis=1)
    sin_hi = jnp.concatenate([sp_ref[...]] * reps, axis=1)
    sin_lo = jnp.concatenate([sm_ref[...]] * reps, axis=1)
    y = y * cos + pltpu.roll(y, ROT_HALF, 1) * sin_hi + pltpu.roll(y, width - ROT_HALF, 1) * sin_lo
    o_ref[...] = (y * scale_ref[...]).astype(o_ref.dtype)


def _proj_t_kernel(x_ref, wt_ref, b_ref, o_ref, *, gate):
    y = _dot_nt(wt_ref[...], x_ref[...])
    if gate:
        y = jax.nn.sigmoid(y + b_ref[...])
    o_ref[0] = y.astype(o_ref.dtype)


def _proj(x, w, *, tn, out_dtype, rope=None, name):
    t, d = x.shape
    n = w.shape[1]
    tm = min(ROW_TILE, t)
    in_specs = [
        pl.BlockSpec((tm, d), lambda i, j: (i, 0)),
        pl.BlockSpec((d, tn), lambda i, j: (0, j)),
    ]
    args = [x, w.astype(BF16)]
    if rope is None:
        body = _proj_kernel
    else:
        body = _proj_rope_kernel
        cos, sin_hi, sin_lo, scale = rope
        in_specs += [pl.BlockSpec((tm, LANES), lambda i, j: (i, 0))] * 3
        in_specs += [pl.BlockSpec((1, tn), lambda i, j: (0, j))]
        args += [cos, sin_hi, sin_lo, scale]
    return pl.pallas_call(
        body,
        grid=(t // tm, n // tn),
        in_specs=in_specs,
        out_specs=pl.BlockSpec((tm, tn), lambda i, j: (i, j)),
        out_shape=jax.ShapeDtypeStruct((t, n), out_dtype),
        compiler_params=_params("parallel", "parallel"),
        name=name,
    )(*args)


def _proj_t(x, wt, bias, *, tm, tn, out_dtype, gate, name):
    t, d = x.shape
    n = wt.shape[0]
    return pl.pallas_call(
        functools.partial(_proj_t_kernel, gate=gate),
        grid=(t // tm, n // tn),
        in_specs=[
            pl.BlockSpec((tm, d), lambda i, j: (i, 0)),
            pl.BlockSpec((tn, d), lambda i, j: (j, 0)),
            pl.BlockSpec((tn, 1), lambda i, j: (j, 0)),
        ],
        out_specs=pl.BlockSpec((1, tn, tm), lambda i, j: (i, j, 0)),
        out_shape=jax.ShapeDtypeStruct((t // tm, n, tm), out_dtype),
        compiler_params=_params("parallel", "parallel"),
        name=name,
    )(x, wt.astype(BF16), bias)


def _outproj_kernel(a_ref, b_ref, wa_ref, wb_ref, g_ref, h_ref, gn_ref, o_ref, on_ref):
    y = _dot(a_ref[...], wa_ref[...]) + _dot(b_ref[...], wb_ref[...])
    out = h_ref[...] + _rms(y, g_ref[...])
    o_ref[...] = out
    on_ref[...] = _rms(out, gn_ref[...]).astype(on_ref.dtype)


def _out_proj(prim, mo, w_out, g, h, ffn_g):
    t, d = h.shape
    tm = min(ROW_TILE, t)
    pw = prim.shape[1]
    return pl.pallas_call(
        _outproj_kernel,
        grid=(t // tm,),
        in_specs=[
            pl.BlockSpec((tm, pw), lambda i: (i, 0)),
            pl.BlockSpec((tm, MEM_WIDTH), lambda i: (i, 0)),
            pl.BlockSpec((pw, d), lambda i: (0, 0)),
            pl.BlockSpec((MEM_WIDTH, d), lambda i: (0, 0)),
            pl.BlockSpec((1, d), lambda i: (0, 0)),
            pl.BlockSpec((tm, d), lambda i: (i, 0)),
            pl.BlockSpec((1, d), lambda i: (0, 0)),
        ],
        out_specs=[pl.BlockSpec((tm, d), lambda i: (i, 0))] * 2,
        out_shape=[jax.ShapeDtypeStruct((t, d), F32), jax.ShapeDtypeStruct((t, d), BF16)],
        compiler_params=_params("parallel"),
        name="out_proj",
    )(prim, mo, w_out[:pw].astype(BF16), w_out[pw:].astype(BF16), g.reshape(1, d), h, ffn_g.reshape(1, d))


def _ffn_kernel(h_ref, hn_ref, wg_ref, wu_ref, wo_ref, g2_ref, *rest, feeds_next):
    if feeds_next:
        g3_ref, o_ref, next_ref, acc_ref = rest
    else:
        o_ref, acc_ref = rest
    j = pl.program_id(1)

    @pl.when(j == 0)
    def _():
        acc_ref[...] = jnp.zeros_like(acc_ref)

    hn = hn_ref[...]
    gate = _dot(hn, wg_ref[...])
    up = _dot(hn, wu_ref[...])
    act = (jax.nn.silu(gate) * up).astype(BF16)
    acc_ref[...] += _dot(act, wo_ref[...])

    @pl.when(j == pl.num_programs(1) - 1)
    def _():
        out = h_ref[...] + _rms(acc_ref[...], g2_ref[...])
        o_ref[...] = out
        if feeds_next:
            next_ref[...] = _rms(out, g3_ref[...]).astype(next_ref.dtype)


def _ffn_chunk(hidden):
    units = hidden // LANES
    for parts in range(2, units + 1):
        if units % parts == 0:
            return (units // parts) * LANES
    return hidden


def _ffn(h, hn, w_in, w_out, g2, next_g=None):
    t, d = h.shape
    hidden = w_out.shape[0]
    th = _ffn_chunk(hidden)
    nh = hidden // th
    tm = min(ROW_TILE, t)
    w_in = w_in.astype(BF16)
    row = pl.BlockSpec((tm, d), lambda i, j: (i, 0))
    gain = pl.BlockSpec((1, d), lambda i, j: (0, 0))
    feeds_next = next_g is not None
    in_specs = [row, row,
                pl.BlockSpec((d, th), lambda i, j: (0, j)),
                pl.BlockSpec((d, th), lambda i, j: (0, j + nh)),
                pl.BlockSpec((th, d), lambda i, j: (j, 0)),
                gain]
    args = [h, hn, w_in, w_in, w_out.astype(BF16), g2.reshape(1, d)]
    out_specs, out_shape = row, jax.ShapeDtypeStruct((t, d), F32)
    if feeds_next:
        in_specs.append(gain)
        args.append(next_g.reshape(1, d))
        out_specs, out_shape = [row, row], [out_shape, jax.ShapeDtypeStruct((t, d), BF16)]
    return pl.pallas_call(
        functools.partial(_ffn_kernel, feeds_next=feeds_next),
        grid=(t // tm, nh),
        in_specs=in_specs,
        out_specs=out_specs,
        out_shape=out_shape,
        scratch_shapes=[pltpu.VMEM((tm, d), F32)],
        compiler_params=_params("parallel", "arbitrary"),
        name="ffn",
    )(*args)


def _mem_attn_kernel(q_ref, mk_ref, mvt_ref, o_ref):
    q = q_ref[...]
    mk = mk_ref[0]
    mvt = mvt_ref[0]
    head_of_lane = lax.broadcasted_iota(jnp.int32, mk.shape, 1) // HEAD_DIM
    head_of_row = lax.broadcasted_iota(jnp.int32, mvt.shape, 0) // HEAD_DIM
    scores = [_dot_nt(jnp.where(head_of_lane == h, mk, 0), q) * ATTN_SCALE for h in range(MEM_HEADS)]
    acc = jnp.zeros((mvt.shape[0], q.shape[0]), F32)
    for h in range(MEM_HEADS):
        e = jnp.exp(scores[h] - jnp.max(scores[h], axis=0, keepdims=True))
        p = e * (1.0 / jnp.sum(e, axis=0, keepdims=True))
        acc = acc + _dot(jnp.where(head_of_row == h, mvt, 0), p.astype(BF16))
    o_ref[...] = acc.T.astype(o_ref.dtype)


def _mem_attn(q_mem, mem_k, mem_v, seq):
    t = q_mem.shape[0]
    m = mem_k.shape[1]
    tm = min(ROW_TILE, seq)
    per_seq = seq // tm
    return pl.pallas_call(
        _mem_attn_kernel,
        grid=(t // tm,),
        in_specs=[
            pl.BlockSpec((tm, MEM_WIDTH), lambda i: (i, 0)),
            pl.BlockSpec((1, m, MEM_WIDTH), lambda i: (i // per_seq, 0, 0)),
            pl.BlockSpec((1, MEM_WIDTH, m), lambda i: (i // per_seq, 0, 0)),
        ],
        out_specs=pl.BlockSpec((tm, MEM_WIDTH), lambda i: (i, 0)),
        out_shape=jax.ShapeDtypeStruct((t, MEM_WIDTH), BF16),
        compiler_params=_params("parallel"),
        name="mem_attn",
    )(q_mem, mem_k, mem_v.transpose(0, 2, 1))


def _compress_kernel(xk_ref, xv_ref, pk_ref, pv_ref, k1a_ref, k1b_ref, k2_ref, v1a_ref, v1b_ref, v2t_ref,
                     c_ref, s_ref, kc_ref, vct_ref):
    def hidden(x_ref, pos_ref, wa_ref, wb_ref):
        x = x_ref[0]
        n = x.shape[0]
        first = _dot((x + pos_ref[0:1, :]).astype(BF16), wa_ref[...])
        second = _dot((x + pos_ref[1:2, :]).astype(BF16), wb_ref[...])
        return jax.nn.gelu(first + pltpu.roll(second, n - 1, 0)).astype(BF16)

    hk = hidden(xk_ref, pk_ref, k1a_ref, k1b_ref)
    both = _dot(hk, k2_ref[...])
    kc_ref[0, 0] = (both[:, :LANES] * c_ref[0] + both[:, LANES:] * s_ref[0]).astype(kc_ref.dtype)
    hv = hidden(xv_ref, pv_ref, v1a_ref, v1b_ref)
    vct_ref[0, 0] = _dot_nt(v2t_ref[...], hv).astype(vct_ref.dtype)


def _rope_partner_cols(w):
    d = np.arange(w.shape[1]) % HEAD_DIM
    src = np.where(d < ROT_HALF, np.arange(w.shape[1]) + ROT_HALF, np.arange(w.shape[1]) - ROT_HALF)
    src = np.clip(src, 0, w.shape[1] - 1)
    sign = np.where(d < ROT_HALF, -1.0, np.where(d < ROT_DIM, 1.0, 0.0)).astype(np.float32)
    return w[:, src] * sign


def _compress(raw_k, raw_v, pos_k, pos_v, k_w1, k_w2, v_w1, v_w2, cos_c, sin_c, batch, seq):
    n_chunk = seq // CMP_STRIDE
    feat = CMP_STRIDE * KV_WIDTH
    hid = k_w1.shape[1]

    chunks = lambda raw: raw.reshape(batch, n_chunk, feat)

    def per_group(w_half):
        w = w_half.astype(BF16).reshape(CMP_STRIDE, 1, 1, HEAD_DIM, hid)
        own = (np.arange(KV_GROUPS)[:, None] == np.arange(KV_GROUPS)[None, :]).reshape(1, KV_GROUPS, KV_GROUPS, 1, 1)
        return jnp.where(own, w, jnp.zeros((), BF16)).transpose(1, 0, 2, 3, 4).reshape(KV_GROUPS, feat, hid)

    def pos_rows(pos):
        p = pos.reshape(2, CMP_STRIDE, 1, HEAD_DIM)
        return jnp.broadcast_to(p, (2, CMP_STRIDE, KV_GROUPS, HEAD_DIM)).reshape(2, feat)

    half_rows = CMP_STRIDE * HEAD_DIM
    zeros = jnp.zeros((hid, LANES - HEAD_DIM), F32)
    k2 = jnp.concatenate([k_w2, zeros, _rope_partner_cols(k_w2), zeros], axis=1).astype(BF16)
    x_spec = pl.BlockSpec((1, n_chunk, feat), lambda b, g: (b, 0, 0))
    pos_spec = pl.BlockSpec((2, feat), lambda b, g: (0, 0))
    w1_spec = pl.BlockSpec((None, feat, hid), lambda b, g: (g, 0, 0))
    tab_spec = pl.BlockSpec((1, n_chunk, LANES), lambda b, g: (b, 0, 0))
    return pl.pallas_call(
        _compress_kernel,
        grid=(batch, KV_GROUPS),
        in_specs=[x_spec, x_spec, pos_spec, pos_spec, w1_spec, w1_spec,
                  pl.BlockSpec((hid, 2 * LANES), lambda b, g: (0, 0)),
                  w1_spec, w1_spec,
                  pl.BlockSpec((HEAD_DIM, hid), lambda b, g: (0, 0)),
                  tab_spec, tab_spec],
        out_specs=[pl.BlockSpec((1, 1, n_chunk, LANES), lambda b, g: (b, g, 0, 0)),
                   pl.BlockSpec((1, 1, HEAD_DIM, n_chunk), lambda b, g: (b, g, 0, 0))],
        out_shape=[jax.ShapeDtypeStruct((batch, KV_GROUPS, n_chunk, LANES), BF16),
                   jax.ShapeDtypeStruct((batch, KV_GROUPS, HEAD_DIM, n_chunk), BF16)],
        compiler_params=_params("parallel", "parallel"),
        name="nsa_compress",
    )(chunks(raw_k), chunks(raw_v), pos_rows(pos_k), pos_rows(pos_v),
      per_group(k_w1[:half_rows]), per_group(k_w1[half_rows:]), k2,
      per_group(v_w1[:half_rows]), per_group(v_w1[half_rows:]), v_w2.T.astype(BF16), cos_c, sin_c)


def _split_dot_left(w_bf16, x):
    hi = x.astype(BF16)
    rem = x - hi.astype(F32)
    mid = rem.astype(BF16)
    lo = (rem - mid.astype(F32)).astype(BF16)
    return _dot(w_bf16, hi) + _dot(w_bf16, mid) + _dot(w_bf16, lo)


def _nsa_attn_kernel(q_ref, kk_ref, hot_ref, vt_ref, kc_ref, vct_ref, tap_ref, winb_ref, gate_ref, o_ref,
                     qx_ref, m_ref, acc_ref, kmax_ref):
    tq = q_ref.shape[0]
    qt = pl.program_id(2)
    t0 = qt * tq
    heads = range(GROUP_HEADS)

    q2 = q_ref[...]
    qs = jnp.concatenate([q2[:, h * LANES:(h + 1) * LANES] for h in heads], axis=0)
    lane = lax.broadcasted_iota(jnp.int32, qs.shape, 1)
    q_sel = jnp.where(lane < HEAD_DIM, qs, 0)
    q_win = jnp.where(lane >= HEAD_DIM, qs, 0)

    col_t = t0 + lax.broadcasted_iota(jnp.int32, (1, tq), 1)

    @pl.when(qt == 0)
    def _():
        half = lax.broadcasted_iota(jnp.int32, (1, LANES), 1) < HEAD_DIM
        k_abs = jnp.abs(kk_ref[...].astype(F32))
        kc_abs = jnp.abs(kc_ref[0, 0].astype(F32))
        row_sums = [jnp.sum(jnp.where(half, k_abs, 0.0), axis=1, keepdims=True),
                    jnp.sum(jnp.where(half, 0.0, k_abs), axis=1, keepdims=True),
                    jnp.sum(kc_abs, axis=1, keepdims=True)]
        kmax_ref[0] = functools.reduce(jnp.maximum, [jnp.max(s) for s in row_sums])
    bounded = jnp.max(jnp.abs(qs.astype(F32))) * kmax_ref[0] <= SAFE_LOG2

    @pl.when(bounded)
    def _():
        _nsa_attend(False, q_sel, q_win, col_t, qt, kk_ref, hot_ref, vt_ref, kc_ref, vct_ref, tap_ref, winb_ref,
                    gate_ref, o_ref, qx_ref, m_ref, acc_ref)

    @pl.when(jnp.logical_not(bounded))
    def _():
        _nsa_attend(True, q_sel, q_win, col_t, qt, kk_ref, hot_ref, vt_ref, kc_ref, vct_ref, tap_ref, winb_ref,
                    gate_ref, o_ref, qx_ref, m_ref, acc_ref)


def _nsa_attend(stabilized, q_sel, q_win, col_t, qt, kk_ref, hot_ref, vt_ref, kc_ref, vct_ref, tap_ref, winb_ref,
                gate_ref, o_ref, qx_ref, m_ref, acc_ref):
    tq = col_t.shape[1]
    tk = tq
    heads = range(GROUP_HEADS)
    head_rows = lambda h: slice(h * tq, (h + 1) * tq)

    def weights(scores):
        if not stabilized:
            return jnp.exp2(scores), None
        top = jnp.max(scores, axis=0, keepdims=True)
        return jnp.exp2(scores - top), top > 0.5 * NEG

    n_cmp = kc_ref.shape[2]
    blk_end = lax.broadcasted_iota(jnp.int32, (n_cmp, 1), 0) * CMP_STRIDE + (CMP_BLOCK - 1)
    valid = blk_end <= col_t
    cmp_scores = [_dot_nt(kc_ref[0, 0], q_sel[head_rows(h)]) for h in heads]
    o_cmp = []
    p_sum = jnp.zeros((n_cmp, tq), F32)
    for h in heads:
        e, live = weights(jnp.where(valid, cmp_scores[h], NEG))
        total = jnp.sum(e, axis=0, keepdims=True)
        live = total > 0.0 if live is None else live
        p = e * jnp.where(live, 1.0 / total, 0.0)
        o_cmp.append(_dot(vct_ref[0, 0], p.astype(BF16)))
        p_sum = p_sum + p
    imp = _split_dot_left(tap_ref[...], p_sum)

    n_sel = imp.shape[0]
    blk = lax.broadcasted_iota(jnp.int32, (n_sel, tq), 0)
    blk_f = blk.astype(F32)
    cur = col_t // SEL_BLOCK
    forced = (blk == 0) | (blk == cur) | (blk == cur - 1)
    work = jnp.where(forced, -jnp.inf, jnp.where(blk <= cur, imp, -FORCE))
    for _ in range(max(min(SEL_TOPK, n_sel) - N_FORCED, 0)):
        best = jnp.max(work, axis=0, keepdims=True)
        hit = blk_f == jnp.min(jnp.where(work == best, blk_f, float(n_sel)), axis=0, keepdims=True)
        work = jnp.where(hit, -jnp.inf, work)
    chosen = jnp.where(work == -jnp.inf, 1.0, 0.0)
    if n_sel < LANES:
        chosen = jnp.concatenate([chosen, jnp.zeros((LANES - n_sel, tq), F32)], axis=0)
    block_bias = ((chosen - 1.0) * (-NEG)).T.astype(BF16)
    qx_ref[:, 0:LANES] = q_sel
    qx_ref[:, LANES:2 * LANES] = jnp.concatenate([block_bias] * GROUP_HEADS, axis=0)

    def key_pos(kt, n):
        return kt * tk + lax.broadcasted_iota(jnp.int32, (n * tk, 1), 0)

    def key_rows(ref, kt, n):
        return ref[pl.ds(pl.multiple_of(kt * tk, tk), n * tk), :]

    def value_rows(kt, n, first_row):
        vals = jnp.concatenate([vt_ref[kt + i, first_row:first_row + HEAD_DIM, :] for i in range(n)], axis=1)
        return jnp.concatenate([vals, jnp.ones((ACC_ROWS - HEAD_DIM, n * tk), BF16)], axis=0)

    def normalized(acc):
        return acc[0:HEAD_DIM] / acc[HEAD_DIM:HEAD_DIM + 1]

    win_start = jnp.maximum(qt - (WIN_TILES - 1), 0)
    win_bias = winb_ref[qt - win_start]
    win_keys = key_rows(kk_ref, win_start, WIN_TILES)
    win_vals = value_rows(win_start, WIN_TILES, HEAD_DIM)
    win_scores = [_dot_nt(win_keys, q_win[head_rows(h)]) for h in heads]
    o_win = [normalized(_dot(win_vals, weights(win_scores[h] + win_bias)[0].astype(BF16))) for h in heads]

    acc_ref[...] = jnp.zeros(acc_ref.shape, F32)
    full_spans = qt // SEL_SPAN

    def score_span(i):
        kt = i * SEL_SPAN
        keys = jnp.concatenate([key_rows(kk_ref, kt, SEL_SPAN), key_rows(hot_ref, kt, SEL_SPAN)], axis=1)
        return tuple(_dot_nt(keys, qx_ref[head_rows(h), :]) for h in heads)

    def causal_scores(i, scores):
        return jnp.where(key_pos(i * SEL_SPAN, SEL_SPAN) <= col_t, scores, NEG)

    if not stabilized:
        def plain_spans(first, count, causal_last):
            scores = [score_span(first + s) for s in range(count)]
            for s in range(count):
                v_ext = value_rows((first + s) * SEL_SPAN, SEL_SPAN, 0)
                for h in heads:
                    masked = causal_last and s == count - 1
                    e = jnp.exp2(causal_scores(first + s, scores[s][h]) if masked else scores[s][h]).astype(BF16)
                    acc_ref[:, head_rows(h)] += _dot(v_ext, e)

        def plain_step(i, carry):
            plain_spans(i * SEL_UNROLL, SEL_UNROLL, False)
            return carry

        whole = full_spans // SEL_UNROLL
        lax.fori_loop(0, whole, plain_step, 0)
        for left in range(SEL_UNROLL):
            @pl.when(full_spans - whole * SEL_UNROLL == left)
            def _():
                plain_spans(whole * SEL_UNROLL, left + 1, True)
    else:
        m_ref[...] = jnp.full(m_ref.shape, NEG, F32)

        def flash_span(i, span_scores, causal):
            v_ext = value_rows(i * SEL_SPAN, SEL_SPAN, 0)
            for h in heads:
                cols = head_rows(h)
                scores = causal_scores(i, span_scores[h]) if causal else span_scores[h]
                m_old = m_ref[:, cols]
                m_new = jnp.maximum(m_old, jnp.max(scores, axis=0, keepdims=True))
                e = jnp.exp2(scores - m_new).astype(BF16)
                acc_ref[:, cols] = jnp.exp2(m_old - m_new) * acc_ref[:, cols] + _dot(v_ext, e)
                m_ref[:, cols] = m_new

        def sel_step(i, span_scores):
            following = score_span(i + 1)
            flash_span(i, span_scores, False)
            return following

        last_scores = lax.fori_loop(0, full_spans, sel_step, score_span(0))
        flash_span(full_spans, last_scores, True)

    gates = gate_ref[...]
    outs = []
    for h in heads:
        outs.append(gates[3 * h:3 * h + 1, :] * o_cmp[h]
                    + gates[3 * h + 1:3 * h + 2, :] * normalized(acc_ref[:, head_rows(h)])
                    + gates[3 * h + 2:3 * h + 3, :] * o_win[h])
    o_ref[...] = jnp.concatenate(outs, axis=0).T.astype(o_ref.dtype)


def _tap_matrix(n_sel, n_chunk):
    ratio = SEL_BLOCK // CMP_STRIDE
    tap = np.zeros((n_sel, n_chunk), np.float32)
    n_cmp = n_chunk - (CMP_BLOCK // CMP_STRIDE - 1)
    for j in range(n_sel):
        for n in range(n_cmp):
            lo = max(n * CMP_STRIDE, j * SEL_BLOCK)
            hi = min(n * CMP_STRIDE + CMP_BLOCK, (j + 1) * SEL_BLOCK)
            if hi > lo:
                tap[j, n] = (hi - lo) / CMP_STRIDE
    return jnp.asarray(tap, BF16)


def _nsa_attention(qk, vt, gates, kc, vct, batch, seq):
    tq = ATTN_TILE
    n_tiles = seq // tq
    n_chunk = kc.shape[2]
    n_sel = seq // SEL_BLOCK
    q_blocks = GROUP_HEADS * LANES
    k_col0 = KV_GROUPS * q_blocks // LANES
    vt5 = vt.reshape(batch, n_tiles, KV_GROUPS, 2 * HEAD_DIM, tq)
    gates5 = gates.reshape(batch, n_tiles, gates.shape[1] // 16, 16, tq)
    width = GROUP_HEADS * tq
    assert n_sel <= LANES, "block one-hot is one lane tile wide"
    assert n_tiles % SEL_SPAN == 0 and n_tiles >= WIN_TILES
    hot = jnp.asarray((np.arange(seq)[:, None] // SEL_BLOCK) == np.arange(LANES)[None, :], BF16)
    dist = (np.arange(WIN_TILES)[:, None, None] * tq + np.arange(tq)[None, None, :]
            - np.arange(WIN_TILES * tq)[None, :, None])
    win_bias = jnp.asarray(np.where((dist >= 0) & (dist < WINDOW), 0.0, NEG), F32)
    return pl.pallas_call(
        _nsa_attn_kernel,
        grid=(batch, KV_GROUPS, n_tiles),
        in_specs=[
            pl.BlockSpec((tq, q_blocks), lambda b, g, i: (b * n_tiles + i, g)),
            pl.BlockSpec((seq, LANES), lambda b, g, i: (b, k_col0 + g)),
            pl.BlockSpec((seq, LANES), lambda b, g, i: (0, 0)),
            pl.BlockSpec((None, n_tiles, None, 2 * HEAD_DIM, tq), lambda b, g, i: (b, 0, g, 0, 0)),
            pl.BlockSpec((1, 1, n_chunk, LANES), lambda b, g, i: (b, g, 0, 0)),
            pl.BlockSpec((1, 1, HEAD_DIM, n_chunk), lambda b, g, i: (b, g, 0, 0)),
            pl.BlockSpec((n_sel, n_chunk), lambda b, g, i: (0, 0)),
            pl.BlockSpec((WIN_TILES, WIN_TILES * tq, tq), lambda b, g, i: (0, 0, 0)),
            pl.BlockSpec((None, None, None, 16, tq), lambda b, g, i: (b, i, g, 0, 0)),
        ],
        out_specs=pl.BlockSpec((tq, GROUP_WIDTH), lambda b, g, i: (b * n_tiles + i, g)),
        out_shape=jax.ShapeDtypeStruct((batch * seq, PRIMARY_WIDTH), BF16),
        scratch_shapes=[
            pltpu.VMEM((width, 2 * LANES), BF16),
            pltpu.VMEM((1, width), F32),
            pltpu.VMEM((ACC_ROWS, width), F32),
            pltpu.SMEM((1,), F32),
        ],
        compiler_params=_params("parallel", "parallel", "arbitrary"),
        name="nsa_attention",
    )(qk, qk, hot, vt5, kc, vct, _tap_matrix(n_sel, n_chunk), win_bias, gates5)


def _rope_tables(positions):
    inv = ROPE_THETA ** (-jnp.arange(0, ROT_DIM, 2, dtype=F32) / ROT_DIM)
    ang = positions.astype(F32)[..., None] * inv
    cos, sin = lax.optimization_barrier((jnp.cos(ang), jnp.sin(ang)))
    reps = (1,) * (ang.ndim - 1) + (LANES // ROT_HALF,)
    cos_t, sin_t = jnp.tile(cos, reps), jnp.tile(sin, reps)
    dim = jnp.arange(LANES) % HEAD_DIM
    lo, hi = dim < ROT_HALF, (dim >= ROT_HALF) & (dim < ROT_DIM)
    return (jnp.where(lo | hi, cos_t, 1.0), jnp.where(hi, sin_t, 0.0), jnp.where(lo, -sin_t, 0.0),
            jnp.where(lo | hi, sin_t, 0.0))


def _nsa_layer(hn, w_in, gate_b, pos_k, pos_v, k_w1, k_w2, v_w1, v_w2, positions, batch, seq):
    t, d = hn.shape
    pw, kw = PRIMARY_WIDTH, KV_WIDTH
    offs = np.cumsum([0, pw] + [kw] * 6 + [3 * KV_GROUPS * GROUP_HEADS, MEM_WIDTH])
    col = lambda i: w_in[:, offs[i]:offs[i + 1]]
    w_q, w_kc, w_vc, w_ks, w_vs, w_kw, w_vw, w_gl, w_qm = [col(i) for i in range(9)]

    wq_h = w_q.reshape(d, KV_GROUPS * GROUP_HEADS, 1, HEAD_DIM)
    wq_dup = jnp.broadcast_to(wq_h, (d, KV_GROUPS * GROUP_HEADS, 2, HEAD_DIM)).reshape(d, 2 * pw)
    wk_pair = jnp.stack([w_ks.reshape(d, KV_GROUPS, HEAD_DIM), w_kw.reshape(d, KV_GROUPS, HEAD_DIM)], axis=2)
    w_rope = jnp.concatenate([wq_dup, wk_pair.reshape(d, 2 * kw)], axis=1)
    scale = jnp.concatenate([jnp.full((1, 2 * pw), ATTN_SCALE * LOG2E, F32), jnp.ones((1, 2 * kw), F32)], axis=1)
    cos, sin_hi, sin_lo, _ = _rope_tables(positions.reshape(t))
    qk = _proj(hn, w_rope, tn=w_rope.shape[1] // 3, out_dtype=BF16,
               rope=(cos, sin_hi, sin_lo, scale), name="nsa_proj_rope")

    wv_pair = jnp.stack([w_vs.reshape(d, KV_GROUPS, HEAD_DIM), w_vw.reshape(d, KV_GROUPS, HEAD_DIM)], axis=2)
    wv_t = wv_pair.reshape(d, 2 * kw).T
    vt = _proj_t(hn, wv_t, jnp.zeros((2 * kw, 1), F32), tm=ATTN_TILE, tn=2 * kw,
                 out_dtype=BF16, gate=False, name="nsa_proj_values")

    per_group = 3 * GROUP_HEADS
    wg = jnp.pad(w_gl.reshape(d, KV_GROUPS, per_group), ((0, 0), (0, 0), (0, 16 - per_group)))
    bg = jnp.pad(gate_b.reshape(KV_GROUPS, per_group), ((0, 0), (0, 16 - per_group)))
    n_gate = LANES
    wg_t = jnp.pad(wg.reshape(d, KV_GROUPS * 16).T, ((0, n_gate - KV_GROUPS * 16), (0, 0)))
    bg_t = jnp.pad(bg.reshape(KV_GROUPS * 16, 1), ((0, n_gate - KV_GROUPS * 16), (0, 0)))
    gates = _proj_t(hn, wg_t, bg_t, tm=ATTN_TILE, tn=n_gate, out_dtype=F32, gate=True,
                    name="nsa_proj_gates")

    q_mem, raw_k, raw_v = _proj_parts(hn, [(w_qm, BF16), (w_kc, F32), (w_vc, F32)], "nsa_proj_plain")

    n_chunk = seq // CMP_STRIDE
    cmp_end = jnp.minimum(jnp.arange(n_chunk) * CMP_STRIDE + CMP_BLOCK - 1, seq - 1)
    cos_c, _, _, sin_c = _rope_tables(positions[:, cmp_end])
    kc, vct = _compress(raw_k, raw_v, pos_k, pos_v, k_w1, k_w2, v_w1, v_w2, cos_c, sin_c, batch, seq)
    prim = _nsa_attention(qk, vt, gates, kc, vct, batch, seq)
    return prim, q_mem


def _split_dot(x, w_bf16):
    hi = x.astype(BF16)
    lo = (x - hi.astype(F32)).astype(BF16)
    return _dot(hi, w_bf16) + _dot(lo, w_bf16)


def _head_ones(width):
    head = np.arange(width) // HEAD_DIM
    return jnp.asarray(head[:, None] == head[None, :], BF16)


def _rwkv_prep_kernel(hn_ref, w_ref, mu_ref, w0_ref, w2_ref, a0_ref, a2_ref, g2_ref, kk_ref, ka_ref, rk_ref,
                      ones_ref, r_ref, k_ref, v_ref, na_ref, b_ref, lw_ref, g_ref, bonus_ref, last_ref,
                      *, tiles_per_seq):
    pw = PRIMARY_WIDTH
    @pl.when(pl.program_id(0) == 0)
    def _():
        last_ref[...] = jnp.zeros_like(last_ref)

    x = _dot(hn_ref[...], w_ref[...])
    first_tile = (pl.program_id(0) % tiles_per_seq) == 0
    last_prev = jnp.where(first_tile, 0.0, last_ref[0:1, :])
    row = lax.broadcasted_iota(jnp.int32, x.shape, 0)
    prev = jnp.where(row == 0, last_prev, pltpu.roll(x, 1, 0))
    last_ref[0:1, :] = x[x.shape[0] - 1:, :]
    xs = x + (prev - x) * mu_ref[...]
    r = xs[:, 0:pw]
    k = xs[:, pw:2 * pw]
    v = xs[:, 2 * pw:3 * pw]
    lora = xs[:, 3 * pw:3 * pw + DECAY_LORA + AAA_LORA]
    gl = xs[:, 3 * pw + DECAY_LORA + AAA_LORA:]
    w_pre = w0_ref[...] + _dot(jnp.tanh(lora).astype(BF16), w2_ref[...])
    z = -w_pre
    w = -(jnp.maximum(z, 0.0) + jnp.log1p(jnp.exp(-jnp.abs(z)))) - 0.5
    lw_ref[...] = -jnp.exp(w)
    a = jax.nn.sigmoid(a0_ref[...] + _dot(lora.astype(BF16), a2_ref[...]))
    g_ref[...] = _dot(jax.nn.sigmoid(gl).astype(BF16), g2_ref[...]).astype(g_ref.dtype)
    ones = ones_ref[...]

    def head_sums(y):
        gw = GROUP_WIDTH
        return jnp.concatenate([_split_dot(y[:, i * gw:(i + 1) * gw], ones) for i in range(KV_GROUPS)], axis=1)

    kk = k * kk_ref[...]
    kk = kk * lax.rsqrt(jnp.maximum(head_sums(kk * kk), 1e-24))
    k2 = k * (1.0 + (a - 1.0) * ka_ref[...])
    r_ref[...] = r
    k_ref[...] = k2
    v_ref[...] = v.astype(v_ref.dtype)
    na_ref[...] = -kk
    b_ref[...] = kk * a
    bonus_ref[...] = (head_sums(r * k2 * rk_ref[...]) * v).astype(bonus_ref.dtype)


def _rwkv_prep(hn, w_in, mu, w0, w2, a0, a2, g2, k_k, k_a, r_k, seq):
    t, d = hn.shape
    width = w_in.shape[1]
    pw = PRIMARY_WIDTH
    tm = min(PREP_TILE, seq)
    row = lambda a: a.reshape(1, -1)
    lora_w = DECAY_LORA + AAA_LORA
    w2e = jnp.concatenate([w2, jnp.zeros((AAA_LORA, pw), F32)], axis=0).astype(BF16)
    a2e = jnp.concatenate([jnp.zeros((DECAY_LORA, pw), F32), a2], axis=0).astype(BF16)
    const = lambda shape: pl.BlockSpec(shape, lambda i: (0, 0))
    out_spec = pl.BlockSpec((tm, pw), lambda i: (i, 0))
    return pl.pallas_call(
        functools.partial(_rwkv_prep_kernel, tiles_per_seq=seq // tm),
        grid=(t // tm,),
        in_specs=[
            pl.BlockSpec((tm, d), lambda i: (i, 0)),
            const((d, width)),
            const((1, width)), const((1, pw)), const((lora_w, pw)), const((1, pw)), const((lora_w, pw)),
            const((GATE_LORA, pw)), const((1, pw)), const((1, pw)), const((1, pw)),
            const((GROUP_WIDTH, GROUP_WIDTH)),
        ],
        out_specs=[out_spec] * 8,
        out_shape=[jax.ShapeDtypeStruct((t, pw), dt) for dt in (F32, F32, BF16, F32, F32, F32, BF16, BF16)],
        scratch_shapes=[pltpu.VMEM((SUBLANES, width), F32)],
        compiler_params=_params("arbitrary"),
        name="rwkv_prep",
    )(hn, w_in.astype(BF16), row(mu), row(w0), w2e, row(a0), a2e, g2.astype(BF16), row(k_k), row(k_a), row(r_k),
      _head_ones(GROUP_WIDTH))


def _block_diag(x, mask):
    return jnp.where(mask, jnp.concatenate([x.astype(BF16)] * GROUP_HEADS, axis=0), jnp.zeros((), BF16))


def _rwkv_scan_kernel(r_ref, k_ref, v_ref, na_ref, b_ref, lw_ref, g_ref, bonus_ref, lng_ref, lnb_ref, o_ref, state_ref):
    n_batch, rows, _ = r_ref.shape
    c = SCAN_CHUNK
    gw = GROUP_WIDTH
    pairs = [(bi, grp) for bi in range(n_batch) for grp in range(KV_GROUPS)]
    units = [(bi, grp, sub) for sub in range(rows // c) for bi, grp in pairs]
    every = lambda fn, *lists: [fn(*args) for args in zip(*lists)]

    @pl.when(pl.program_id(0) == 0)
    def _():
        state_ref[...] = jnp.zeros_like(state_ref)

    rr = lax.broadcasted_iota(jnp.int32, (gw, gw), 0)
    cc = lax.broadcasted_iota(jnp.int32, (gw, gw), 1)
    bd_mask = (rr // HEAD_DIM) == (cc // HEAD_DIM)
    t_idx = lax.broadcasted_iota(jnp.int32, (c, gw), 0)
    s_idx = lax.broadcasted_iota(jnp.int32, (c, gw), 1) % HEAD_DIM
    strict = t_idx > s_idx
    incl = t_idx >= s_idx
    eye = jnp.where(t_idx == s_idx, 1.0, 0.0)
    same_block = {}
    size = INV_BASE
    while size <= c:
        same_block[size] = (t_idx // size) == (s_idx // size)
        size *= 2
    tril =jnp.where(lax.broadcasted_iota(jnp.int32, (c, c), 0) >= lax.broadcasted_iota(jnp.int32, (c, c), 1),
                     1.0, 0.0).astype(BF16)
    ones_bd = jnp.where(bd_mask, 1.0, 0.0).astype(BF16)

    def cumsum_rows(x):
        hi = x.astype(BF16)
        rem = x - hi.astype(F32)
        mid = rem.astype(BF16)
        lo = (rem - mid.astype(F32)).astype(BF16)
        return _dot(tril, hi) + _dot(tril, mid) + _dot(tril, lo)

    bd = lambda x: _block_diag(x, bd_mask)
    cols = lambda grp: slice(grp * gw, (grp + 1) * gw)
    rows_of = lambda sub: slice(sub * c, (sub + 1) * c)
    load = lambda ref: [ref[bi, rows_of(sub), cols(grp)] for bi, grp, sub in units]
    r, k, v, na, bv, lw = (load(ref) for ref in (r_ref, k_ref, v_ref, na_ref, b_ref, lw_ref))

    cum = every(cumsum_rows, lw)
    p_incl = every(jnp.exp, cum)
    inv_p = every(lambda x: jnp.exp(-x), cum)
    b_t = every(lambda x, s: (x * s).astype(BF16), bv, inv_p)
    k_t = every(lambda x, s: (x * s).astype(BF16), k, inv_p)
    x2 = every(lambda a, cu, l, rr_, p: jnp.concatenate([a * jnp.exp(cu - l), rr_ * p], axis=0).astype(BF16),
               na, cum, lw, r, p_incl)

    g_b = every(lambda x, y: _dot_nt(x, bd(y)), x2, b_t)
    g_k = every(lambda x, y: _dot_nt(x, bd(y)), x2, k_t)
    l_ab = every(lambda g: jnp.where(strict, g[:c], 0.0), g_b)
    m_rb = every(lambda g: jnp.where(incl, g[c:], 0.0).astype(BF16), g_b)
    l_ak = every(lambda g: jnp.where(strict, g[:c], 0.0).astype(BF16), g_k)
    m_rk = every(lambda g: jnp.where(incl, g[c:], 0.0).astype(BF16), g_k)

    power = every(lambda l: jnp.where(same_block[INV_BASE], l, 0.0), l_ab)
    t_inv = every(lambda p: eye + p, power)
    for _ in range(int(np.log2(INV_BASE)) - 1):
        power = every(lambda p: _dot(p.astype(BF16), bd(p)), power)
        t_inv = every(lambda t, p: t + _dot(t.astype(BF16), bd(p)), t_inv, power)
    size = 2 * INV_BASE
    while size <= c:
        off_mask = same_block[size] & ~same_block[size // 2]
        half = every(lambda t, l: _dot(t.astype(BF16), bd(jnp.where(off_mask, l, 0.0))), t_inv, l_ab)
        t_inv = every(lambda t, hf: t + _dot(hf.astype(BF16), bd(t)), t_inv, half)
        size *= 2

    v_bd = every(bd, v)
    t_inv = every(lambda t: t.astype(BF16), t_inv)
    lk_v = every(_dot, l_ak, v_bd)
    mk_v = every(_dot, m_rk, v_bd)

    state = [state_ref[bi * KV_GROUPS + grp] for bi, grp in pairs]
    for sub in range(rows // c):
        of = lambda lst: lst[sub * len(pairs):(sub + 1) * len(pairs)]
        xh = every(lambda x, s: _dot_nt(x, s.astype(BF16)), of(x2), state)
        u = every(lambda t, x, lv: _dot(t, bd(x[:c] + lv)), of(t_inv), xh, of(lk_v))
        out = every(lambda x, mb, uu, mv: x[c:] + _dot(mb, bd(uu)) + mv, xh, of(m_rb), u, of(mk_v))
        delta = every(lambda uu, vv, b, kk_: lax.dot_general(
            jnp.concatenate([uu.astype(BF16), vv.astype(BF16)], axis=0), jnp.concatenate([b, kk_], axis=0), _TN,
            preferred_element_type=F32), u, of(v), of(b_t), of(k_t))
        state = every(lambda s, dl, p: (s + jnp.where(bd_mask, dl, 0.0)) * p[c - 1:c, :], state, delta, of(p_incl))
        mean = every(lambda o: _split_dot(o, ones_bd) * (1.0 / HEAD_DIM), out)
        dev = every(lambda o, m: o - m, out, mean)
        var = every(lambda dv: _split_dot(dv * dv, ones_bd) * (1.0 / HEAD_DIM), dev)
        for i, (bi, grp) in enumerate(pairs):
            y = dev[i] * lax.rsqrt(var[i] + RWKV_GN_EPS) * lng_ref[:, cols(grp)] + lnb_ref[:, cols(grp)]
            o_ref[bi, rows_of(sub), cols(grp)] = (
                (y + bonus_ref[bi, rows_of(sub), cols(grp)]) * g_ref[bi, rows_of(sub), cols(grp)]).astype(o_ref.dtype)
    for i, (bi, grp) in enumerate(pairs):
        state_ref[bi * KV_GROUPS + grp] = state[i]


def _rwkv_scan(r, k, v, na, b, lw, g, bonus, ln_g, ln_b, batch, seq):
    c = SCAN_CHUNK
    n_chunks = seq // c
    pw = PRIMARY_WIDTH
    assert GROUP_HEADS * c == GROUP_WIDTH, "block-diagonal packing needs a 64-token chunk"
    assert n_chunks % SCAN_CHUNKS_PER_STEP == 0
    blk = pl.BlockSpec((batch, SCAN_CHUNKS_PER_STEP * c, pw), lambda ci: (0, ci, 0))
    const = pl.BlockSpec((1, pw), lambda ci: (0, 0))
    per_batch = lambda a: a.reshape(batch, seq, pw)
    out = pl.pallas_call(
        _rwkv_scan_kernel,
        grid=(n_chunks // SCAN_CHUNKS_PER_STEP,),
        in_specs=[blk] * 8 + [const, const],
        out_specs=blk,
        out_shape=jax.ShapeDtypeStruct((batch, seq, pw), BF16),
        scratch_shapes=[pltpu.VMEM((batch * KV_GROUPS, GROUP_WIDTH, GROUP_WIDTH), F32)],
        compiler_params=_params("arbitrary"),
        name="rwkv_scan",
    )(*(per_batch(a) for a in (r, k, v, na, b, lw, g, bonus)), ln_g.reshape(1, pw), ln_b.reshape(1, pw))
    return out.reshape(batch * seq, pw)


def _rwkv_layer(hn, w_in, mu, w0, w2, a0, a2, g2, k_k, k_a, r_k, ln_g, ln_b, batch, seq):
    q_mem = _proj(hn, w_in[:, RWKV_SHIFT_W:], tn=MEM_WIDTH, out_dtype=BF16, name="rwkv_proj_mem")
    r, k, v, na, b, lw, g, bonus = _rwkv_prep(hn, w_in[:, :RWKV_SHIFT_W], mu, w0, w2, a0, a2, g2, k_k, k_a,
                                              r_k.reshape(-1), seq)
    prim = _rwkv_scan(r, k, v, na, b, lw, g, bonus, ln_g, ln_b, batch, seq)
    return prim, q_mem


def kernel(x, mem, positions, mem_norm_g, w_mem_kv, pre_mix_g, post_mix_g, pre_ffn_g, post_ffn_g, w_out, w_ffn_in, w_ffn_out, nsa_w_in, nsa_gate_b, nsa_cmp_pos_k, nsa_cmp_pos_v, nsa_cmp_k_w1, nsa_cmp_k_w2, nsa_cmp_v_w1, nsa_cmp_v_w2, rwkv_w_in, rwkv_mu, rwkv_w0, rwkv_w2, rwkv_a0, rwkv_a2, rwkv_g2, rwkv_k_k, rwkv_k_a, rwkv_r_k, rwkv_ln_g, rwkv_ln_b):
    batch, seq, d = x.shape
    n_mem = mem.shape[1]
    depth = pre_mix_g.shape[0]
    mkv = _proj(_norm_cast(mem.reshape(batch * n_mem, d), mem_norm_g), w_mem_kv, tn=w_mem_kv.shape[1],
                out_dtype=BF16, name="mem_kv")
    mem_k = mkv[:, :MEM_WIDTH].reshape(batch, n_mem, MEM_WIDTH)
    mem_v = mkv[:, MEM_WIDTH:].reshape(batch, n_mem, MEM_WIDTH)
    h = x.reshape(batch * seq, d)
    hn = _norm_cast(h, pre_mix_g[0])
    for i in range(depth):
        j = i // 2
        if i % 2 == 0:
            prim, q_mem = _nsa_layer(hn, nsa_w_in[j], nsa_gate_b[j], nsa_cmp_pos_k[j], nsa_cmp_pos_v[j],
                                     nsa_cmp_k_w1[j], nsa_cmp_k_w2[j], nsa_cmp_v_w1[j], nsa_cmp_v_w2[j],
                                     positions, batch, seq)
        else:
            prim, q_mem = _rwkv_layer(hn, rwkv_w_in[j], rwkv_mu[j], rwkv_w0[j], rwkv_w2[j], rwkv_a0[j],
                                      rwkv_a2[j], rwkv_g2[j], rwkv_k_k[j], rwkv_k_a[j], rwkv_r_k[j], rwkv_ln_g[j],
                                      rwkv_ln_b[j], batch, seq)
        mo = _mem_attn(q_mem, mem_k, mem_v, seq)
        h, hn_ffn = _out_proj(prim, mo, w_out[i], post_mix_g[i], h, pre_ffn_g[i])
        if i + 1 < depth:
            h, hn = _ffn(h, hn_ffn, w_ffn_in[i], w_ffn_out[i], post_ffn_g[i], next_g=pre_mix_g[i + 1])
        else:
            h = _ffn(h, hn_ffn, w_ffn_in[i], w_ffn_out[i], post_ffn_g[i])
    return h.reshape(batch, seq, d)
```

```python
import functools

import numpy as np
import jax
import jax.numpy as jnp
from jax import lax
from jax.experimental import pallas as pl
from jax.experimental.pallas import tpu as pltpu

F32 = jnp.float32
BF16 = jnp.bfloat16

HEAD_DIM = 64
ROT_DIM = HEAD_DIM // 4
ROT_HALF = ROT_DIM // 2
ROPE_THETA = 500000.0
MEM_HEADS = 4
MEM_WIDTH = MEM_HEADS * HEAD_DIM
KV_GROUPS = 3
GROUP_HEADS = 4
GROUP_WIDTH = GROUP_HEADS * HEAD_DIM
PRIMARY_WIDTH = KV_GROUPS * GROUP_WIDTH
KV_WIDTH = KV_GROUPS * HEAD_DIM
CMP_BLOCK = 32
CMP_STRIDE = 16
SEL_BLOCK = 64
SEL_TOPK = 16
WINDOW = 512
DECAY_LORA = 64
AAA_LORA = 64
GATE_LORA = 128
RWKV_SHIFT_W = 3 * PRIMARY_WIDTH + DECAY_LORA + AAA_LORA + GATE_LORA
RWKV_GN_EPS = HEAD_DIM * 1e-5
NORM_EPS = 1e-6
NEG = -1e30
FORCE = 1e6
ATTN_SCALE = HEAD_DIM ** -0.5
LOG2E = float(np.log2(np.e))
SAFE_LOG2 = 80.0

LANES = 128
SUBLANES = 8
VMEM_LIMIT_BYTES = 48 * 1024 * 1024

ROW_TILE = 512
ATTN_TILE = 256
SCAN_CHUNK = 64
SCAN_CHUNKS_PER_STEP = 2
INV_BASE = 8
SEL_SPAN = 2
SEL_UNROLL = 8
WIN_TILES = WINDOW // ATTN_TILE + 1
N_FORCED = 3
ACC_ROWS = HEAD_DIM + 16
PREP_TILE = 256

_NT = (((1,), (1,)), ((), ()))
_TN = (((0,), (0,)), ((), ()))


def _params(*sem):
    return pltpu.CompilerParams(dimension_semantics=sem, vmem_limit_bytes=VMEM_LIMIT_BYTES)


def _rms(x, g):
    return x * lax.rsqrt(jnp.mean(x * x, axis=-1, keepdims=True) + NORM_EPS) * g


def _dot(a, b):
    return jnp.dot(a, b, preferred_element_type=F32)


def _dot_nt(a, b):
    return lax.dot_general(a, b, _NT, preferred_element_type=F32)


def _norm_cast_kernel(x_ref, g_ref, o_ref):
    o_ref[...] = _rms(x_ref[...], g_ref[...]).astype(o_ref.dtype)


def _norm_cast(x, g):
    t, d = x.shape
    tm = min(ROW_TILE, t)
    return pl.pallas_call(
        _norm_cast_kernel,
        grid=(t // tm,),
        in_specs=[pl.BlockSpec((tm, d), lambda i: (i, 0)), pl.BlockSpec((1, d), lambda i: (0, 0))],
        out_specs=pl.BlockSpec((tm, d), lambda i: (i, 0)),
        out_shape=jax.ShapeDtypeStruct((t, d), BF16),
        compiler_params=_params("parallel"),
        name="norm_cast",
    )(x, g.reshape(1, d))


def _proj_kernel(x_ref, w_ref, o_ref):
    o_ref[...] = _dot(x_ref[...], w_ref[...]).astype(o_ref.dtype)


def _proj_parts_kernel(x_ref, w_ref, *o_refs, starts):
    y = _dot(x_ref[...], w_ref[...])
    for o_ref, start in zip(o_refs, starts):
        o_ref[...] = y[:, start:start + o_ref.shape[1]].astype(o_ref.dtype)


def _proj_parts(x, parts, name):
    t, d = x.shape
    tm = min(ROW_TILE, t)
    padded = [jnp.pad(w, ((0, 0), (0, -w.shape[1] % LANES))) for w, _ in parts]
    starts = tuple(int(s) for s in np.cumsum([0] + [w.shape[1] for w in padded[:-1]]))
    w_all = jnp.concatenate(padded, axis=1).astype(BF16)
    return pl.pallas_call(
        functools.partial(_proj_parts_kernel, starts=starts),
        grid=(t // tm,),
        in_specs=[pl.BlockSpec((tm, d), lambda i: (i, 0)), pl.BlockSpec(w_all.shape, lambda i: (0, 0))],
        out_specs=[pl.BlockSpec((tm, w.shape[1]), lambda i: (i, 0)) for w, _ in parts],
        out_shape=[jax.ShapeDtypeStruct((t, w.shape[1]), dt) for w, dt in parts],
        compiler_params=_params("parallel"),
        name=name,
    )(x, w_all)


def _proj_rope_kernel(x_ref, w_ref, c_ref, sp_ref, sm_ref, scale_ref, o_ref):
    y = _dot(x_ref[...], w_ref[...])
    width = y.shape[1]
    reps = width // LANES
    cos = jnp.concatenate([c_ref[...]] * reps, axis=1)
    sin_hi = jnp.concatenate([sp_ref[...]] * reps, axis=1)
    sin_lo = jnp.concatenate([sm_ref[...]] * reps, axis=1)
    y = y * cos + pltpu.roll(y, ROT_HALF, 1) * sin_hi + pltpu.roll(y, width - ROT_HALF, 1) * sin_lo
    o_ref[...] = (y * scale_ref[...]).astype(o_ref.dtype)


def _proj_t_kernel(x_ref, wt_ref, b_ref, o_ref, *, gate):
    y = _dot_nt(wt_ref[...], x_ref[...])
    if gate:
        y = jax.nn.sigmoid(y + b_ref[...])
    o_ref[0] = y.astype(o_ref.dtype)


def _proj(x, w, *, tn, out_dtype, rope=None, name):
    t, d = x.shape
    n = w.shape[1]
    tm = min(ROW_TILE, t)
    in_specs = [
        pl.BlockSpec((tm, d), lambda i, j: (i, 0)),
        pl.BlockSpec((d, tn), lambda i, j: (0, j)),
    ]
    args = [x, w.astype(BF16)]
    if rope is None:
        body = _proj_kernel
    else:
        body = _proj_rope_kernel
        cos, sin_hi, sin_lo, scale = rope
        in_specs += [pl.BlockSpec((tm, LANES), lambda i, j: (i, 0))] * 3
        in_specs += [pl.BlockSpec((1, tn), lambda i, j: (0, j))]
        args += [cos, sin_hi, sin_lo, scale]
    return pl.pallas_call(
        body,
        grid=(t // tm, n // tn),
        in_specs=in_specs,
        out_specs=pl.BlockSpec((tm, tn), lambda i, j: (i, j)),
        out_shape=jax.ShapeDtypeStruct((t, n), out_dtype),
        compiler_params=_params("parallel", "parallel"),
        name=name,
    )(*args)


def _proj_t(x, wt, bias, *, tm, tn, out_dtype, gate, name):
    t, d = x.shape
    n = wt.shape[0]
    return pl.pallas_call(
        functools.partial(_proj_t_kernel, gate=gate),
        grid=(t // tm, n // tn),
        in_specs=[
            pl.BlockSpec((tm, d), lambda i, j: (i, 0)),
            pl.BlockSpec((tn, d), lambda i, j: (j, 0)),
            pl.BlockSpec((tn, 1), lambda i, j: (j, 0)),
        ],
        out_specs=pl.BlockSpec((1, tn, tm), lambda i, j: (i, j, 0)),
        out_shape=jax.ShapeDtypeStruct((t // tm, n, tm), out_dtype),
        compiler_params=_params("parallel", "parallel"),
        name=name,
    )(x, wt.astype(BF16), bias)


def _outproj_kernel(a_ref, b_ref, wa_ref, wb_ref, g_ref, h_ref, gn_ref, o_ref, on_ref):
    y = _dot(a_ref[...], wa_ref[...]) + _dot(b_ref[...], wb_ref[...])
    out = h_ref[...] + _rms(y, g_ref[...])
    o_ref[...] = out
    on_ref[...] = _rms(out, gn_ref[...]).astype(on_ref.dtype)


def _out_proj(prim, mo, w_out, g, h, ffn_g):
    t, d = h.shape
    tm = min(ROW_TILE, t)
    pw = prim.shape[1]
    return pl.pallas_call(
        _outproj_kernel,
        grid=(t // tm,),
        in_specs=[
            pl.BlockSpec((tm, pw), lambda i: (i, 0)),
            pl.BlockSpec((tm, MEM_WIDTH), lambda i: (i, 0)),
            pl.BlockSpec((pw, d), lambda i: (0, 0)),
            pl.BlockSpec((MEM_WIDTH, d), lambda i: (0, 0)),
            pl.BlockSpec((1, d), lambda i: (0, 0)),
            pl.BlockSpec((tm, d), lambda i: (i, 0)),
            pl.BlockSpec((1, d), lambda i: (0, 0)),
        ],
        out_specs=[pl.BlockSpec((tm, d), lambda i: (i, 0))] * 2,
        out_shape=[jax.ShapeDtypeStruct((t, d), F32), jax.ShapeDtypeStruct((t, d), BF16)],
        compiler_params=_params("parallel"),
        name="out_proj",
    )(prim, mo, w_out[:pw].astype(BF16), w_out[pw:].astype(BF16), g.reshape(1, d), h, ffn_g.reshape(1, d))


def _ffn_kernel(h_ref, hn_ref, wg_ref, wu_ref, wo_ref, g2_ref, *rest, feeds_next):
    if feeds_next:
        g3_ref, o_ref, next_ref, acc_ref = rest
    else:
        o_ref, acc_ref = rest
    j = pl.program_id(1)

    @pl.when(j == 0)
    def _():
        acc_ref[...] = jnp.zeros_like(acc_ref)

    hn = hn_ref[...]
    gate = _dot(hn, wg_ref[...])
    up = _dot(hn, wu_ref[...])
    act = (jax.nn.silu(gate) * up).astype(BF16)
    acc_ref[...] += _dot(act, wo_ref[...])

    @pl.when(j == pl.num_programs(1) - 1)
    def _():
        out = h_ref[...] + _rms(acc_ref[...], g2_ref[...])
        o_ref[...] = out
        if feeds_next:
            next_ref[...] = _rms(out, g3_ref[...]).astype(next_ref.dtype)


def _ffn_chunk(hidden):
    units = hidden // LANES
    for parts in range(2, units + 1):
        if units % parts == 0:
            return (units // parts) * LANES
    return hidden


def _ffn(h, hn, w_in, w_out, g2, next_g=None):
    t, d = h.shape
    hidden = w_out.shape[0]
    th = _ffn_chunk(hidden)
    nh = hidden // th
    tm = min(ROW_TILE, t)
    w_in = w_in.astype(BF16)
    row = pl.BlockSpec((tm, d), lambda i, j: (i, 0))
    gain = pl.BlockSpec((1, d), lambda i, j: (0, 0))
    feeds_next = next_g is not None
    in_specs = [row, row,
                pl.BlockSpec((d, th), lambda i, j: (0, j)),
                pl.BlockSpec((d, th), lambda i, j: (0, j + nh)),
                pl.BlockSpec((th, d), lambda i, j: (j, 0)),
                gain]
    args = [h, hn, w_in, w_in, w_out.astype(BF16), g2.reshape(1, d)]
    out_specs, out_shape = row, jax.ShapeDtypeStruct((t, d), F32)
    if feeds_next:
        in_specs.append(gain)
        args.append(next_g.reshape(1, d))
        out_specs, out_shape = [row, row], [out_shape, jax.ShapeDtypeStruct((t, d), BF16)]
    return pl.pallas_call(
        functools.partial(_ffn_kernel, feeds_next=feeds_next),
        grid=(t // tm, nh),
        in_specs=in_specs,
        out_specs=out_specs,
        out_shape=out_shape,
        scratch_shapes=[pltpu.VMEM((tm, d), F32)],
        compiler_params=_params("parallel", "arbitrary"),
        name="ffn",
    )(*args)


def _mem_attn_kernel(q_ref, mk_ref, mvt_ref, o_ref):
    q = q_ref[...]
    mk = mk_ref[0]
    mvt = mvt_ref[0]
    head_of_lane = lax.broadcasted_iota(jnp.int32, mk.shape, 1) // HEAD_DIM
    head_of_row = lax.broadcasted_iota(jnp.int32, mvt.shape, 0) // HEAD_DIM
    scores = [_dot_nt(jnp.where(head_of_lane == h, mk, 0), q) * ATTN_SCALE for h in range(MEM_HEADS)]
    acc = jnp.zeros((mvt.shape[0], q.shape[0]), F32)
    for h in range(MEM_HEADS):
        e = jnp.exp(scores[h] - jnp.max(scores[h], axis=0, keepdims=True))
        p = e * (1.0 / jnp.sum(e, axis=0, keepdims=True))
        acc = acc + _dot(jnp.where(head_of_row == h, mvt, 0), p.astype(BF16))
    o_ref[...] = acc.T.astype(o_ref.dtype)


def _mem_attn(q_mem, mem_k, mem_v, seq):
    t = q_mem.shape[0]
    m = mem_k.shape[1]
    tm = min(ROW_TILE, seq)
    per_seq = seq // tm
    return pl.pallas_call(
        _mem_attn_kernel,
        grid=(t // tm,),
        in_specs=[
            pl.BlockSpec((tm, MEM_WIDTH), lambda i: (i, 0)),
            pl.BlockSpec((1, m, MEM_WIDTH), lambda i: (i // per_seq, 0, 0)),
            pl.BlockSpec((1, MEM_WIDTH, m), lambda i: (i // per_seq, 0, 0)),
        ],
        out_specs=pl.BlockSpec((tm, MEM_WIDTH), lambda i: (i, 0)),
        out_shape=jax.ShapeDtypeStruct((t, MEM_WIDTH), BF16),
        compiler_params=_params("parallel"),
        name="mem_attn",
    )(q_mem, mem_k, mem_v.transpose(0, 2, 1))


def _compress_kernel(xk_ref, xv_ref, pk_ref, pv_ref, k1a_ref, k1b_ref, k2_ref, v1a_ref, v1b_ref, v2t_ref,
                     c_ref, s_ref, kc_ref, vct_ref):
    def hidden(x_ref, pos_ref, wa_ref, wb_ref):
        x = x_ref[0]
        n = x.shape[0]
        first = _dot((x + pos_ref[0:1, :]).astype(BF16), wa_ref[...])
        second = _dot((x + pos_ref[1:2, :]).astype(BF16), wb_ref[...])
        return jax.nn.gelu(first + pltpu.roll(second, n - 1, 0)).astype(BF16)

    hk = hidden(xk_ref, pk_ref, k1a_ref, k1b_ref)
    both = _dot(hk, k2_ref[...])
    kc_ref[0, 0] = (both[:, :LANES] * c_ref[0] + both[:, LANES:] * s_ref[0]).astype(kc_ref.dtype)
    hv = hidden(xv_ref, pv_ref, v1a_ref, v1b_ref)
    vct_ref[0, 0] = _dot_nt(v2t_ref[...], hv).astype(vct_ref.dtype)


def _rope_partner_cols(w):
    d = np.arange(w.shape[1]) % HEAD_DIM
    src = np.where(d < ROT_HALF, np.arange(w.shape[1]) + ROT_HALF, np.arange(w.shape[1]) - ROT_HALF)
    src = np.clip(src, 0, w.shape[1] - 1)
    sign = np.where(d < ROT_HALF, -1.0, np.where(d < ROT_DIM, 1.0, 0.0)).astype(np.float32)
    return w[:, src] * sign


def _compress(raw_k, raw_v, pos_k, pos_v, k_w1, k_w2, v_w1, v_w2, cos_c, sin_c, batch, seq):
    n_chunk = seq // CMP_STRIDE
    feat = CMP_STRIDE * KV_WIDTH
    hid = k_w1.shape[1]

    chunks = lambda raw: raw.reshape(batch, n_chunk, feat)

    def per_group(w_half):
        w = w_half.astype(BF16).reshape(CMP_STRIDE, 1, 1, HEAD_DIM, hid)
        own = (np.arange(KV_GROUPS)[:, None] == np.arange(KV_GROUPS)[None, :]).reshape(1, KV_GROUPS, KV_GROUPS, 1, 1)
        return jnp.where(own, w, jnp.zeros((), BF16)).transpose(1, 0, 2, 3, 4).reshape(KV_GROUPS, feat, hid)

    def pos_rows(pos):
        p = pos.reshape(2, CMP_STRIDE, 1, HEAD_DIM)
        return jnp.broadcast_to(p, (2, CMP_STRIDE, KV_GROUPS, HEAD_DIM)).reshape(2, feat)

    half_rows = CMP_STRIDE * HEAD_DIM
    zeros = jnp.zeros((hid, LANES - HEAD_DIM), F32)
    k2 = jnp.concatenate([k_w2, zeros, _rope_partner_cols(k_w2), zeros], axis=1).astype(BF16)
    x_spec = pl.BlockSpec((1, n_chunk, feat), lambda b, g: (b, 0, 0))
    pos_spec = pl.BlockSpec((2, feat), lambda b, g: (0, 0))
    w1_spec = pl.BlockSpec((None, feat, hid), lambda b, g: (g, 0, 0))
    tab_spec = pl.BlockSpec((1, n_chunk, LANES), lambda b, g: (b, 0, 0))
    return pl.pallas_call(
        _compress_kernel,
        grid=(batch, KV_GROUPS),
        in_specs=[x_spec, x_spec, pos_spec, pos_spec, w1_spec, w1_spec,
                  pl.BlockSpec((hid, 2 * LANES), lambda b, g: (0, 0)),
                  w1_spec, w1_spec,
                  pl.BlockSpec((HEAD_DIM, hid), lambda b, g: (0, 0)),
                  tab_spec, tab_spec],
        out_specs=[pl.BlockSpec((1, 1, n_chunk, LANES), lambda b, g: (b, g, 0, 0)),
                   pl.BlockSpec((1, 1, HEAD_DIM, n_chunk), lambda b, g: (b, g, 0, 0))],
        out_shape=[jax.ShapeDtypeStruct((batch, KV_GROUPS, n_chunk, LANES), BF16),
                   jax.ShapeDtypeStruct((batch, KV_GROUPS, HEAD_DIM, n_chunk), BF16)],
        compiler_params=_params("parallel", "parallel"),
        name="nsa_compress",
    )(chunks(raw_k), chunks(raw_v), pos_rows(pos_k), pos_rows(pos_v),
      per_group(k_w1[:half_rows]), per_group(k_w1[half_rows:]), k2,
      per_group(v_w1[:half_rows]), per_group(v_w1[half_rows:]), v_w2.T.astype(BF16), cos_c, sin_c)


def _split_dot_left(w_bf16, x):
    hi = x.astype(BF16)
    rem = x - hi.astype(F32)
    mid = rem.astype(BF16)
    lo = (rem - mid.astype(F32)).astype(BF16)
    return _dot(w_bf16, hi) + _dot(w_bf16, mid) + _dot(w_bf16, lo)


def _nsa_attn_kernel(q_ref, kk_ref, hot_ref, vt_ref, kc_ref, vct_ref, tap_ref, winb_ref, gate_ref, o_ref,
                     qx_ref, m_ref, acc_ref, kmax_ref):
    tq = q_ref.shape[0]
    qt = pl.program_id(2)
    t0 = qt * tq
    heads = range(GROUP_HEADS)

    q2 = q_ref[...]
    qs = jnp.concatenate([q2[:, h * LANES:(h + 1) * LANES] for h in heads], axis=0)
    lane = lax.broadcasted_iota(jnp.int32, qs.shape, 1)
    q_sel = jnp.where(lane < HEAD_DIM, qs, 0)
    q_win = jnp.where(lane >= HEAD_DIM, qs, 0)

    col_t = t0 + lax.broadcasted_iota(jnp.int32, (1, tq), 1)

    @pl.when(qt == 0)
    def _():
        half = lax.broadcasted_iota(jnp.int32, (1, LANES), 1) < HEAD_DIM
        k_abs = jnp.abs(kk_ref[...].astype(F32))
        kc_abs = jnp.abs(kc_ref[0, 0].astype(F32))
        row_sums = [jnp.sum(jnp.where(half, k_abs, 0.0), axis=1, keepdims=True),
                    jnp.sum(jnp.where(half, 0.0, k_abs), axis=1, keepdims=True),
                    jnp.sum(kc_abs, axis=1, keepdims=True)]
        kmax_ref[0] = functools.reduce(jnp.maximum, [jnp.max(s) for s in row_sums])
    bounded = jnp.max(jnp.abs(qs.astype(F32))) * kmax_ref[0] <= SAFE_LOG2

    @pl.when(bounded)
    def _():
        _nsa_attend(False, q_sel, q_win, col_t, qt, kk_ref, hot_ref, vt_ref, kc_ref, vct_ref, tap_ref, winb_ref,
                    gate_ref, o_ref, qx_ref, m_ref, acc_ref)

    @pl.when(jnp.logical_not(bounded))
    def _():
        _nsa_attend(True, q_sel, q_win, col_t, qt, kk_ref, hot_ref, vt_ref, kc_ref, vct_ref, tap_ref, winb_ref,
                    gate_ref, o_ref, qx_ref, m_ref, acc_ref)


def _nsa_attend(stabilized, q_sel, q_win, col_t, qt, kk_ref, hot_ref, vt_ref, kc_ref, vct_ref, tap_ref, winb_ref,
                gate_ref, o_ref, qx_ref, m_ref, acc_ref):
    tq = col_t.shape[1]
    tk = tq
    heads = range(GROUP_HEADS)
    head_rows = lambda h: slice(h * tq, (h + 1) * tq)

    def weights(scores):
        if not stabilized:
            return jnp.exp2(scores), None
        top = jnp.max(scores, axis=0, keepdims=True)
        return jnp.exp2(scores - top), top > 0.5 * NEG

    n_cmp = kc_ref.shape[2]
    blk_end = lax.broadcasted_iota(jnp.int32, (n_cmp, 1), 0) * CMP_STRIDE + (CMP_BLOCK - 1)
    valid = blk_end <= col_t
    cmp_scores = [_dot_nt(kc_ref[0, 0], q_sel[head_rows(h)]) for h in heads]
    o_cmp = []
    p_sum = jnp.zeros((n_cmp, tq), F32)
    for h in heads:
        e, live = weights(jnp.where(valid, cmp_scores[h], NEG))
        total = jnp.sum(e, axis=0, keepdims=True)
        live = total > 0.0 if live is None else live
        p = e * jnp.where(live, 1.0 / total, 0.0)
        o_cmp.append(_dot(vct_ref[0, 0], p.astype(BF16)))
        p_sum = p_sum + p
    imp = _split_dot_left(tap_ref[...], p_sum)

    n_sel = imp.shape[0]
    blk = lax.broadcasted_iota(jnp.int32, (n_sel, tq), 0)
    blk_f = blk.astype(F32)
    cur = col_t // SEL_BLOCK
    forced = (blk == 0) | (blk == cur) | (blk == cur - 1)
    work = jnp.where(forced, -jnp.inf, jnp.where(blk <= cur, imp, -FORCE))
    for _ in range(max(min(SEL_TOPK, n_sel) - N_FORCED, 0)):
        best = jnp.max(work, axis=0, keepdims=True)
        hit = blk_f == jnp.min(jnp.where(work == best, blk_f, float(n_sel)), axis=0, keepdims=True)
        work = jnp.where(hit, -jnp.inf, work)
    chosen = jnp.where(work == -jnp.inf, 1.0, 0.0)
    if n_sel < LANES:
        chosen = jnp.concatenate([chosen, jnp.zeros((LANES - n_sel, tq), F32)], axis=0)
    block_bias = ((chosen - 1.0) * (-NEG)).T.astype(BF16)
    qx_ref[:, 0:LANES] = q_sel
    qx_ref[:, LANES:2 * LANES] = jnp.concatenate([block_bias] * GROUP_HEADS, axis=0)

    def key_pos(kt, n):
        return kt * tk + lax.broadcasted_iota(jnp.int32, (n * tk, 1), 0)

    def key_rows(ref, kt, n):
        return ref[pl.ds(pl.multiple_of(kt * tk, tk), n * tk), :]

    def value_rows(kt, n, first_row):
        vals = jnp.concatenate([vt_ref[kt + i, first_row:first_row + HEAD_DIM, :] for i in range(n)], axis=1)
        return jnp.concatenate([vals, jnp.ones((ACC_ROWS - HEAD_DIM, n * tk), BF16)], axis=0)

    def normalized(acc):
        return acc[0:HEAD_DIM] / acc[HEAD_DIM:HEAD_DIM + 1]

    win_start = jnp.maximum(qt - (WIN_TILES - 1), 0)
    win_bias = winb_ref[qt - win_start]
    win_keys = key_rows(kk_ref, win_start, WIN_TILES)
    win_vals = value_rows(win_start, WIN_TILES, HEAD_DIM)
    win_scores = [_dot_nt(win_keys, q_win[head_rows(h)]) for h in heads]
    o_win = [normalized(_dot(win_vals, weights(win_scores[h] + win_bias)[0].astype(BF16))) for h in heads]

    acc_ref[...] = jnp.zeros(acc_ref.shape, F32)
    full_spans = qt // SEL_SPAN

    def score_span(i):
        kt = i * SEL_SPAN
        keys = jnp.concatenate([key_rows(kk_ref, kt, SEL_SPAN), key_rows(hot_ref, kt, SEL_SPAN)], axis=1)
        return tuple(_dot_nt(keys, qx_ref[head_rows(h), :]) for h in heads)

    def causal_scores(i, scores):
        return jnp.where(key_pos(i * SEL_SPAN, SEL_SPAN) <= col_t, scores, NEG)

    if not stabilized:
        def plain_spans(first, count, causal_last):
            scores = [score_span(first + s) for s in range(count)]
            for s in range(count):
                v_ext = value_rows((first + s) * SEL_SPAN, SEL_SPAN, 0)
                for h in heads:
                    masked = causal_last and s == count - 1
                    e = jnp.exp2(causal_scores(first + s, scores[s][h]) if masked else scores[s][h]).astype(BF16)
                    acc_ref[:, head_rows(h)] += _dot(v_ext, e)

        def plain_step(i, carry):
            plain_spans(i * SEL_UNROLL, SEL_UNROLL, False)
            return carry

        whole = full_spans // SEL_UNROLL
        lax.fori_loop(0, whole, plain_step, 0)
        for left in range(SEL_UNROLL):
            @pl.when(full_spans - whole * SEL_UNROLL == left)
            def _():
                plain_spans(whole * SEL_UNROLL, left + 1, True)
    else:
        m_ref[...] = jnp.full(m_ref.shape, NEG, F32)

        def flash_span(i, span_scores, causal):
            v_ext = value_rows(i * SEL_SPAN, SEL_SPAN, 0)
            for h in heads:
                cols = head_rows(h)
                scores = causal_scores(i, span_scores[h]) if causal else span_scores[h]
                m_old = m_ref[:, cols]
                m_new = jnp.maximum(m_old, jnp.max(scores, axis=0, keepdims=True))
                e = jnp.exp2(scores - m_new).astype(BF16)
                acc_ref[:, cols] = jnp.exp2(m_old - m_new) * acc_ref[:, cols] + _dot(v_ext, e)
                m_ref[:, cols] = m_new

        def sel_step(i, span_scores):
            following = score_span(i + 1)
            flash_span(i, span_scores, False)
            return following

        last_scores = lax.fori_loop(0, full_spans, sel_step, score_span(0))
        flash_span(full_spans, last_scores, True)

    gates = gate_ref[...]
    outs = []
    for h in heads:
        outs.append(gates[3 * h:3 * h + 1, :] * o_cmp[h]
                    + gates[3 * h + 1:3 * h + 2, :] * normalized(acc_ref[:, head_rows(h)])
                    + gates[3 * h + 2:3 * h + 3, :] * o_win[h])
    o_ref[...] = jnp.concatenate(outs, axis=0).T.astype(o_ref.dtype)


def _tap_matrix(n_sel, n_chunk):
    ratio = SEL_BLOCK // CMP_STRIDE
    tap = np.zeros((n_sel, n_chunk), np.float32)
    n_cmp = n_chunk - (CMP_BLOCK // CMP_STRIDE - 1)
    for j in range(n_sel):
        for n in range(n_cmp):
            lo = max(n * CMP_STRIDE, j * SEL_BLOCK)
            hi = min(n * CMP_STRIDE + CMP_BLOCK, (j + 1) * SEL_BLOCK)
            if hi > lo:
                tap[j, n] = (hi - lo) / CMP_STRIDE
    return jnp.asarray(tap, BF16)


def _nsa_attention(qk, vt, gates, kc, vct, batch, seq):
    tq = ATTN_TILE
    n_tiles = seq // tq
    n_chunk = kc.shape[2]
    n_sel = seq // SEL_BLOCK
    q_blocks = GROUP_HEADS * LANES
    k_col0 = KV_GROUPS * q_blocks // LANES
    vt5 = vt.reshape(batch, n_tiles, KV_GROUPS, 2 * HEAD_DIM, tq)
    gates5 = gates.reshape(batch, n_tiles, gates.shape[1] // 16, 16, tq)
    width = GROUP_HEADS * tq
    assert n_sel <= LANES, "block one-hot is one lane tile wide"
    assert n_tiles % SEL_SPAN == 0 and n_tiles >= WIN_TILES
    hot = jnp.asarray((np.arange(seq)[:, None] // SEL_BLOCK) == np.arange(LANES)[None, :], BF16)
    dist = (np.arange(WIN_TILES)[:, None, None] * tq + np.arange(tq)[None, None, :]
            - np.arange(WIN_TILES * tq)[None, :, None])
    win_bias = jnp.asarray(np.where((dist >= 0) & (dist < WINDOW), 0.0, NEG), F32)
    return pl.pallas_call(
        _nsa_attn_kernel,
        grid=(batch, KV_GROUPS, n_tiles),
        in_specs=[
            pl.BlockSpec((tq, q_blocks), lambda b, g, i: (b * n_tiles + i, g)),
            pl.BlockSpec((seq, LANES), lambda b, g, i: (b, k_col0 + g)),
            pl.BlockSpec((seq, LANES), lambda b, g, i: (0, 0)),
            pl.BlockSpec((None, n_tiles, None, 2 * HEAD_DIM, tq), lambda b, g, i: (b, 0, g, 0, 0)),
            pl.BlockSpec((1, 1, n_chunk, LANES), lambda b, g, i: (b, g, 0, 0)),
            pl.BlockSpec((1, 1, HEAD_DIM, n_chunk), lambda b, g, i: (b, g, 0, 0)),
            pl.BlockSpec((n_sel, n_chunk), lambda b, g, i: (0, 0)),
            pl.BlockSpec((WIN_TILES, WIN_TILES * tq, tq), lambda b, g, i: (0, 0, 0)),
            pl.BlockSpec((None, None, None, 16, tq), lambda b, g, i: (b, i, g, 0, 0)),
        ],
        out_specs=pl.BlockSpec((tq, GROUP_WIDTH), lambda b, g, i: (b * n_tiles + i, g)),
        out_shape=jax.ShapeDtypeStruct((batch * seq, PRIMARY_WIDTH), BF16),
        scratch_shapes=[
            pltpu.VMEM((width, 2 * LANES), BF16),
            pltpu.VMEM((1, width), F32),
            pltpu.VMEM((ACC_ROWS, width), F32),
            pltpu.SMEM((1,), F32),
        ],
        compiler_params=_params("parallel", "parallel", "arbitrary"),
        name="nsa_attention",
    )(qk, qk, hot, vt5, kc, vct, _tap_matrix(n_sel, n_chunk), win_bias, gates5)


def _rope_tables(positions):
    inv = ROPE_THETA ** (-jnp.arange(0, ROT_DIM, 2, dtype=F32) / ROT_DIM)
    ang = positions.astype(F32)[..., None] * inv
    cos, sin = lax.optimization_barrier((jnp.cos(ang), jnp.sin(ang)))
    reps = (1,) * (ang.ndim - 1) + (LANES // ROT_HALF,)
    cos_t, sin_t = jnp.tile(cos, reps), jnp.tile(sin, reps)
    dim = jnp.arange(LANES) % HEAD_DIM
    lo, hi = dim < ROT_HALF, (dim >= ROT_HALF) & (dim < ROT_DIM)
    return (jnp.where(lo | hi, cos_t, 1.0), jnp.where(hi, sin_t, 0.0), jnp.where(lo, -sin_t, 0.0),
            jnp.where(lo | hi, sin_t, 0.0))


def _nsa_layer(hn, w_in, gate_b, pos_k, pos_v, k_w1, k_w2, v_w1, v_w2, positions, batch, seq):
    t, d = hn.shape
    pw, kw = PRIMARY_WIDTH, KV_WIDTH
    offs = np.cumsum([0, pw] + [kw] * 6 + [3 * KV_GROUPS * GROUP_HEADS, MEM_WIDTH])
    col = lambda i: w_in[:, offs[i]:offs[i + 1]]
    w_q, w_kc, w_vc, w_ks, w_vs, w_kw, w_vw, w_gl, w_qm = [col(i) for i in range(9)]

    wq_h = w_q.reshape(d, KV_GROUPS * GROUP_HEADS, 1, HEAD_DIM)
    wq_dup = jnp.broadcast_to(wq_h, (d, KV_GROUPS * GROUP_HEADS, 2, HEAD_DIM)).reshape(d, 2 * pw)
    wk_pair = jnp.stack([w_ks.reshape(d, KV_GROUPS, HEAD_DIM), w_kw.reshape(d, KV_GROUPS, HEAD_DIM)], axis=2)
    w_rope = jnp.concatenate([wq_dup, wk_pair.reshape(d, 2 * kw)], axis=1)
    scale = jnp.concatenate([jnp.full((1, 2 * pw), ATTN_SCALE * LOG2E, F32), jnp.ones((1, 2 * kw), F32)], axis=1)
    cos, sin_hi, sin_lo, _ = _rope_tables(positions.reshape(t))
    qk = _proj(hn, w_rope, tn=w_rope.shape[1] // 3, out_dtype=BF16,
               rope=(cos, sin_hi, sin_lo, scale), name="nsa_proj_rope")

    wv_pair = jnp.stack([w_vs.reshape(d, KV_GROUPS, HEAD_DIM), w_vw.reshape(d, KV_GROUPS, HEAD_DIM)], axis=2)
    wv_t = wv_pair.reshape(d, 2 * kw).T
    vt = _proj_t(hn, wv_t, jnp.zeros((2 * kw, 1), F32), tm=ATTN_TILE, tn=2 * kw,
                 out_dtype=BF16, gate=False, name="nsa_proj_values")

    per_group = 3 * GROUP_HEADS
    wg = jnp.pad(w_gl.reshape(d, KV_GROUPS, per_group), ((0, 0), (0, 0), (0, 16 - per_group)))
    bg = jnp.pad(gate_b.reshape(KV_GROUPS, per_group), ((0, 0), (0, 16 - per_group)))
    n_gate = LANES
    wg_t = jnp.pad(wg.reshape(d, KV_GROUPS * 16).T, ((0, n_gate - KV_GROUPS * 16), (0, 0)))
    bg_t = jnp.pad(bg.reshape(KV_GROUPS * 16, 1), ((0, n_gate - KV_GROUPS * 16), (0, 0)))
    gates = _proj_t(hn, wg_t, bg_t, tm=ATTN_TILE, tn=n_gate, out_dtype=F32, gate=True,
                    name="nsa_proj_gates")

    q_mem, raw_k, raw_v = _proj_parts(hn, [(w_qm, BF16), (w_kc, F32), (w_vc, F32)], "nsa_proj_plain")

    n_chunk = seq // CMP_STRIDE
    cmp_end = jnp.minimum(jnp.arange(n_chunk) * CMP_STRIDE + CMP_BLOCK - 1, seq - 1)
    cos_c, _, _, sin_c = _rope_tables(positions[:, cmp_end])
    kc, vct = _compress(raw_k, raw_v, pos_k, pos_v, k_w1, k_w2, v_w1, v_w2, cos_c, sin_c, batch, seq)
    prim = _nsa_attention(qk, vt, gates, kc, vct, batch, seq)
    return prim, q_mem


def _split_dot(x, w_bf16):
    hi = x.astype(BF16)
    lo = (x - hi.astype(F32)).astype(BF16)
    return _dot(hi, w_bf16) + _dot(lo, w_bf16)


def _head_ones(width):
    head = np.arange(width) // HEAD_DIM
    return jnp.asarray(head[:, None] == head[None, :], BF16)


def _rwkv_prep_kernel(hn_ref, w_ref, mu_ref, w0_ref, w2_ref, a0_ref, a2_ref, g2_ref, kk_ref, ka_ref, rk_ref,
                      ones_ref, r_ref, k_ref, v_ref, na_ref, b_ref, lw_ref, g_ref, bonus_ref, last_ref,
                      *, tiles_per_seq):
    pw = PRIMARY_WIDTH
    @pl.when(pl.program_id(0) == 0)
    def _():
        last_ref[...] = jnp.zeros_like(last_ref)

    x = _dot(hn_ref[...], w_ref[...])
    first_tile = (pl.program_id(0) % tiles_per_seq) == 0
    last_prev = jnp.where(first_tile, 0.0, last_ref[0:1, :])
    row = lax.broadcasted_iota(jnp.int32, x.shape, 0)
    prev = jnp.where(row == 0, last_prev, pltpu.roll(x, 1, 0))
    last_ref[0:1, :] = x[x.shape[0] - 1:, :]
    xs = x + (prev - x) * mu_ref[...]
    r = xs[:, 0:pw]
    k = xs[:, pw:2 * pw]
    v = xs[:, 2 * pw:3 * pw]
    lora = xs[:, 3 * pw:3 * pw + DECAY_LORA + AAA_LORA]
    gl = xs[:, 3 * pw + DECAY_LORA + AAA_LORA:]
    w_pre = w0_ref[...] + _dot(jnp.tanh(lora).astype(BF16), w2_ref[...])
    z = -w_pre
    w = -(jnp.maximum(z, 0.0) + jnp.log1p(jnp.exp(-jnp.abs(z)))) - 0.5
    lw_ref[...] = -jnp.exp(w)
    a = jax.nn.sigmoid(a0_ref[...] + _dot(lora.astype(BF16), a2_ref[...]))
    g_ref[...] = _dot(jax.nn.sigmoid(gl).astype(BF16), g2_ref[...]).astype(g_ref.dtype)
    ones = ones_ref[...]

    def head_sums(y):
        gw = GROUP_WIDTH
        return jnp.concatenate([_split_dot(y[:, i * gw:(i + 1) * gw], ones) for i in range(KV_GROUPS)], axis=1)

    kk = k * kk_ref[...]
    kk = kk * lax.rsqrt(jnp.maximum(head_sums(kk * kk), 1e-24))
    k2 = k * (1.0 + (a - 1.0) * ka_ref[...])
    r_ref[...] = r
    k_ref[...] = k2
    v_ref[...] = v.astype(v_ref.dtype)
    na_ref[...] = -kk
    b_ref[...] = kk * a
    bonus_ref[...] = (head_sums(r * k2 * rk_ref[...]) * v).astype(bonus_ref.dtype)


def _rwkv_prep(hn, w_in, mu, w0, w2, a0, a2, g2, k_k, k_a, r_k, seq):
    t, d = hn.shape
    width = w_in.shape[1]
    pw = PRIMARY_WIDTH
    tm = min(PREP_TILE, seq)
    row = lambda a: a.reshape(1, -1)
    lora_w = DECAY_LORA + AAA_LORA
    w2e = jnp.concatenate([w2, jnp.zeros((AAA_LORA, pw), F32)], axis=0).astype(BF16)
    a2e = jnp.concatenate([jnp.zeros((DECAY_LORA, pw), F32), a2], axis=0).astype(BF16)
    const = lambda shape: pl.BlockSpec(shape, lambda i: (0, 0))
    out_spec = pl.BlockSpec((tm, pw), lambda i: (i, 0))
    return pl.pallas_call(
        functools.partial(_rwkv_prep_kernel, tiles_per_seq=seq // tm),
        grid=(t // tm,),
        in_specs=[
            pl.BlockSpec((tm, d), lambda i: (i, 0)),
            const((d, width)),
            const((1, width)), const((1, pw)), const((lora_w, pw)), const((1, pw)), const((lora_w, pw)),
            const((GATE_LORA, pw)), const((1, pw)), const((1, pw)), const((1, pw)),
            const((GROUP_WIDTH, GROUP_WIDTH)),
        ],
        out_specs=[out_spec] * 8,
        out_shape=[jax.ShapeDtypeStruct((t, pw), dt) for dt in (F32, F32, BF16, F32, F32, F32, BF16, BF16)],
        scratch_shapes=[pltpu.VMEM((SUBLANES, width), F32)],
        compiler_params=_params("arbitrary"),
        name="rwkv_prep",
    )(hn, w_in.astype(BF16), row(mu), row(w0), w2e, row(a0), a2e, g2.astype(BF16), row(k_k), row(k_a), row(r_k),
      _head_ones(GROUP_WIDTH))


def _block_diag(x, mask):
    return jnp.where(mask, jnp.concatenate([x.astype(BF16)] * GROUP_HEADS, axis=0), jnp.zeros((), BF16))


def _rwkv_scan_kernel(r_ref, k_ref, v_ref, na_ref, b_ref, lw_ref, g_ref, bonus_ref, lng_ref, lnb_ref, o_ref, state_ref):
    n_batch, rows, _ = r_ref.shape
    c = SCAN_CHUNK
    gw = GROUP_WIDTH
    pairs = [(bi, grp) for bi in range(n_batch) for grp in range(KV_GROUPS)]
    units = [(bi, grp, sub) for sub in range(rows // c) for bi, grp in pairs]
    every = lambda fn, *lists: [fn(*args) for args in zip(*lists)]

    @pl.when(pl.program_id(0) == 0)
    def _():
        state_ref[...] = jnp.zeros_like(state_ref)

    rr = lax.broadcasted_iota(jnp.int32, (gw, gw), 0)
    cc = lax.broadcasted_iota(jnp.int32, (gw, gw), 1)
    bd_mask = (rr // HEAD_DIM) == (cc // HEAD_DIM)
    t_idx = lax.broadcasted_iota(jnp.int32, (c, gw), 0)
    s_idx = lax.broadcasted_iota(jnp.int32, (c, gw), 1) % HEAD_DIM
    strict = t_idx > s_idx
    incl = t_idx >= s_idx
    eye = jnp.where(t_idx == s_idx, 1.0, 0.0)
    same_block = {}
    size = INV_BASE
    while size <= c:
        same_block[size] = (t_idx // size) == (s_idx // size)
        size *= 2
    tril =jnp.where(lax.broadcasted_iota(jnp.int32, (c, c), 0) >= lax.broadcasted_iota(jnp.int32, (c, c), 1),
                     1.0, 0.0).astype(BF16)
    ones_bd = jnp.where(bd_mask, 1.0, 0.0).astype(BF16)

    def cumsum_rows(x):
        hi = x.astype(BF16)
        rem = x - hi.astype(F32)
        mid = rem.astype(BF16)
        lo = (rem - mid.astype(F32)).astype(BF16)
        return _dot(tril, hi) + _dot(tril, mid) + _dot(tril, lo)

    bd = lambda x: _block_diag(x, bd_mask)
    cols = lambda grp: slice(grp * gw, (grp + 1) * gw)
    rows_of = lambda sub: slice(sub * c, (sub + 1) * c)
    load = lambda ref: [ref[bi, rows_of(sub), cols(grp)] for bi, grp, sub in units]
    r, k, v, na, bv, lw = (load(ref) for ref in (r_ref, k_ref, v_ref, na_ref, b_ref, lw_ref))

    cum = every(cumsum_rows, lw)
    p_incl = every(jnp.exp, cum)
    inv_p = every(lambda x: jnp.exp(-x), cum)
    b_t = every(lambda x, s: (x * s).astype(BF16), bv, inv_p)
    k_t = every(lambda x, s: (x * s).astype(BF16), k, inv_p)
    x2 = every(lambda a, cu, l, rr_, p: jnp.concatenate([a * jnp.exp(cu - l), rr_ * p], axis=0).astype(BF16),
               na, cum, lw, r, p_incl)

    g_b = every(lambda x, y: _dot_nt(x, bd(y)), x2, b_t)
    g_k = every(lambda x, y: _dot_nt(x, bd(y)), x2, k_t)
    l_ab = every(lambda g: jnp.where(strict, g[:c], 0.0), g_b)
    m_rb = every(lambda g: jnp.where(incl, g[c:], 0.0).astype(BF16), g_b)
    l_ak = every(lambda g: jnp.where(strict, g[:c], 0.0).astype(BF16), g_k)
    m_rk = every(lambda g: jnp.where(incl, g[c:], 0.0).astype(BF16), g_k)

    power = every(lambda l: jnp.where(same_block[INV_BASE], l, 0.0), l_ab)
    t_inv = every(lambda p: eye + p, power)
    for _ in range(int(np.log2(INV_BASE)) - 1):
        power = every(lambda p: _dot(p.astype(BF16), bd(p)), power)
        t_inv = every(lambda t, p: t + _dot(t.astype(BF16), bd(p)), t_inv, power)
    size = 2 * INV_BASE
    while size <= c:
        off_mask = same_block[size] & ~same_block[size // 2]
        half = every(lambda t, l: _dot(t.astype(BF16), bd(jnp.where(off_mask, l, 0.0))), t_inv, l_ab)
        t_inv = every(lambda t, hf: t + _dot(hf.astype(BF16), bd(t)), t_inv, half)
        size *= 2

    v_bd = every(bd, v)
    t_inv = every(lambda t: t.astype(BF16), t_inv)
    lk_v = every(_dot, l_ak, v_bd)
    mk_v = every(_dot, m_rk, v_bd)

    state = [state_ref[bi * KV_GROUPS + grp] for bi, grp in pairs]
    for sub in range(rows // c):
        of = lambda lst: lst[sub * len(pairs):(sub + 1) * len(pairs)]
        xh = every(lambda x, s: _dot_nt(x, s.astype(BF16)), of(x2), state)
        u = every(lambda t, x, lv: _dot(t, bd(x[:c] + lv)), of(t_inv), xh, of(lk_v))
        out = every(lambda x, mb, uu, mv: x[c:] + _dot(mb, bd(uu)) + mv, xh, of(m_rb), u, of(mk_v))
        delta = every(lambda uu, vv, b, kk_: lax.dot_general(
            jnp.concatenate([uu.astype(BF16), vv.astype(BF16)], axis=0), jnp.concatenate([b, kk_], axis=0), _TN,
            preferred_element_type=F32), u, of(v), of(b_t), of(k_t))
        state = every(lambda s, dl, p: (s + jnp.where(bd_mask, dl, 0.0)) * p[c - 1:c, :], state, delta, of(p_incl))
        mean = every(lambda o: _split_dot(o, ones_bd) * (1.0 / HEAD_DIM), out)
        dev = every(lambda o, m: o - m, out, mean)
        var = every(lambda dv: _split_dot(dv * dv, ones_bd) * (1.0 / HEAD_DIM), dev)
        for i, (bi, grp) in enumerate(pairs):
            y = dev[i] * lax.rsqrt(var[i] + RWKV_GN_EPS) * lng_ref[:, cols(grp)] + lnb_ref[:, cols(grp)]
            o_ref[bi, rows_of(sub), cols(grp)] = (
                (y + bonus_ref[bi, rows_of(sub), cols(grp)]) * g_ref[bi, rows_of(sub), cols(grp)]).astype(o_ref.dtype)
    for i, (bi, grp) in enumerate(pairs):
        state_ref[bi * KV_GROUPS + grp] = state[i]


def _rwkv_scan(r, k, v, na, b, lw, g, bonus, ln_g, ln_b, batch, seq):
    c = SCAN_CHUNK
    n_chunks = seq // c
    pw = PRIMARY_WIDTH
    assert GROUP_HEADS * c == GROUP_WIDTH, "block-diagonal packing needs a 64-token chunk"
    assert n_chunks % SCAN_CHUNKS_PER_STEP == 0
    blk = pl.BlockSpec((batch, SCAN_CHUNKS_PER_STEP * c, pw), lambda ci: (0, ci, 0))
    const = pl.BlockSpec((1, pw), lambda ci: (0, 0))
    per_batch = lambda a: a.reshape(batch, seq, pw)
    out = pl.pallas_call(
        _rwkv_scan_kernel,
        grid=(n_chunks // SCAN_CHUNKS_PER_STEP,),
        in_specs=[blk] * 8 + [const, const],
        out_specs=blk,
        out_shape=jax.ShapeDtypeStruct((batch, seq, pw), BF16),
        scratch_shapes=[pltpu.VMEM((batch * KV_GROUPS, GROUP_WIDTH, GROUP_WIDTH), F32)],
        compiler_params=_params("arbitrary"),
        name="rwkv_scan",
    )(*(per_batch(a) for a in (r, k, v, na, b, lw, g, bonus)), ln_g.reshape(1, pw), ln_b.reshape(1, pw))
    return out.reshape(batch * seq, pw)


def _rwkv_layer(hn, w_in, mu, w0, w2, a0, a2, g2, k_k, k_a, r_k, ln_g, ln_b, batch, seq):
    q_mem = _proj(hn, w_in[:, RWKV_SHIFT_W:], tn=MEM_WIDTH, out_dtype=BF16, name="rwkv_proj_mem")
    r, k, v, na, b, lw, g, bonus = _rwkv_prep(hn, w_in[:, :RWKV_SHIFT_W], mu, w0, w2, a0, a2, g2, k_k, k_a,
                                              r_k.reshape(-1), seq)
    prim = _rwkv_scan(r, k, v, na, b, lw, g, bonus, ln_g, ln_b, batch, seq)
    return prim, q_mem


def kernel(x, mem, positions, mem_norm_g, w_mem_kv, pre_mix_g, post_mix_g, pre_ffn_g, post_ffn_g, w_out, w_ffn_in, w_ffn_out, nsa_w_in, nsa_gate_b, nsa_cmp_pos_k, nsa_cmp_pos_v, nsa_cmp_k_w1, nsa_cmp_k_w2, nsa_cmp_v_w1, nsa_cmp_v_w2, rwkv_w_in, rwkv_mu, rwkv_w0, rwkv_w2, rwkv_a0, rwkv_a2, rwkv_g2, rwkv_k_k, rwkv_k_a, rwkv_r_k, rwkv_ln_g, rwkv_ln_b):
    batch, seq, d = x.shape
    n_mem = mem.shape[1]
    depth = pre_mix_g.shape[0]
    mkv = _proj(_norm_cast(mem.reshape(batch * n_mem, d), mem_norm_g), w_mem_kv, tn=w_mem_kv.shape[1],
                out_dtype=BF16, name="mem_kv")
    mem_k = mkv[:, :MEM_WIDTH].reshape(batch, n_mem, MEM_WIDTH)
    mem_v = mkv[:, MEM_WIDTH:].reshape(batch, n_mem, MEM_WIDTH)
    h = x.reshape(batch * seq, d)
    hn = _norm_cast(h, pre_mix_g[0])
    for i in range(depth):
        j = i // 2
        if i % 2 == 0:
            prim, q_mem = _nsa_layer(hn, nsa_w_in[j], nsa_gate_b[j], nsa_cmp_pos_k[j], nsa_cmp_pos_v[j],
                                     nsa_cmp_k_w1[j], nsa_cmp_k_w2[j], nsa_cmp_v_w1[j], nsa_cmp_v_w2[j],
                                     positions, batch, seq)
        else:
            prim, q_mem = _rwkv_layer(hn, rwkv_w_in[j], rwkv_mu[j], rwkv_w0[j], rwkv_w2[j], rwkv_a0[j],
                                      rwkv_a2[j], rwkv_g2[j], rwkv_k_k[j], rwkv_k_a[j], rwkv_r_k[j], rwkv_ln_g[j],
                                      rwkv_ln_b[j], batch, seq)
        mo = _mem_attn(q_mem, mem_k, mem_v, seq)
        h, hn_ffn = _out_proj(prim, mo, w_out[i], post_mix_g[i], h, pre_ffn_g[i])
        if i + 1 < depth:
            h, hn = _ffn(h, hn_ffn, w_ffn_in[i], w_ffn_out[i], post_ffn_g[i], next_g=pre_mix_g[i + 1])
        else:
            h = _ffn(h, hn_ffn, w_ffn_in[i], w_ffn_out[i], post_ffn_g[i])
    return h.reshape(batch, seq, d)
```
